```python
import math
import jax, jax.numpy as jnp
from jax import lax
import numpy as np

D_MODEL = 1024
BATCH = 4
SEQ = 8192
DEPTH = 2

HEAD_DIM = 64
CONV_CH = 256
CONV_K = 31
SC_CH = 256
SC_K = 3
SWA_Q_HEADS = 4
SWA_KV_HEADS = 2
WINDOW = 128
BLOCK = 128
FOX_HEADS = 4
N_BUCKETS = 32
MAX_DISTANCE = 128
D_FF = 2816
N_BRANCH = 4
EPS = 1e-6
NEG_INF = -1e30

SWA_COLS = (SWA_Q_HEADS + 2 * SWA_KV_HEADS) * HEAD_DIM
FOX_COLS = 3 * FOX_HEADS * HEAD_DIM + FOX_HEADS
_SIZES = (CONV_CH, CONV_CH,
          SC_CH, SC_CH, SC_CH,
          SWA_Q_HEADS * HEAD_DIM, SWA_KV_HEADS * HEAD_DIM, SWA_KV_HEADS * HEAD_DIM,
          FOX_HEADS * HEAD_DIM, FOX_HEADS * HEAD_DIM, FOX_HEADS * HEAD_DIM, FOX_HEADS,
          N_BRANCH * D_MODEL)
SPLITS = tuple(int(s) for s in np.cumsum(_SIZES)[:-1])
IN_COLS = int(sum(_SIZES))

kernel_name = "hybrid_gated_conformer_shortconv_swa_fox"


def _rmsnorm(x, g):
    xf = x.astype(jnp.float32)
    y = xf * lax.rsqrt(jnp.mean(xf * xf, axis=-1, keepdims=True) + EPS)
    return (y * g.astype(jnp.float32)).astype(x.dtype)


def _layernorm(x, g, b):
    xf = x.astype(jnp.float32)
    mu = jnp.mean(xf, axis=-1, keepdims=True)
    var = jnp.mean(jnp.square(xf - mu), axis=-1, keepdims=True)
    y = (xf - mu) * lax.rsqrt(var + EPS)
    return (y * g.astype(jnp.float32) + b.astype(jnp.float32)).astype(x.dtype)


def _swiglu(x, w_gate, w_up, w_down):
    return (jax.nn.silu(x @ w_gate) * (x @ w_up)) @ w_down


def _causal_depthwise_conv(x, w):
    k = w.shape[0]
    return lax.conv_general_dilated(
        x, w[:, None, :].astype(x.dtype), window_strides=(1,), padding=[(k - 1, 0)],
        dimension_numbers=("NWC", "WIO", "NWC"), feature_group_count=x.shape[-1])


def _t5_bucket(dist):
    max_exact = N_BUCKETS // 2
    d = jnp.maximum(dist, 1).astype(jnp.float32)
    large = max_exact + (jnp.log(d / max_exact) / math.log(MAX_DISTANCE / max_exact)
                         * (N_BUCKETS - max_exact)).astype(jnp.int32)
    large = jnp.minimum(large, N_BUCKETS - 1)
    return jnp.where(dist < max_exact, dist, large)


def _sliding_window_attention(q, k, v, sink, rel_bias):
    b, t, hq, dh = q.shape
    hkv = k.shape[2]
    grp = hq // hkv
    nb = t // BLOCK
    qb = q.reshape(b, nb, BLOCK, hkv, grp, dh)
    kb = k.reshape(b, nb, BLOCK, hkv, dh)
    vb = v.reshape(b, nb, BLOCK, hkv, dh)
    pad = ((0, 0), (1, 0), (0, 0), (0, 0), (0, 0))
    kk = jnp.concatenate([jnp.pad(kb, pad)[:, :-1], kb], axis=2)
    vv = jnp.concatenate([jnp.pad(vb, pad)[:, :-1], vb], axis=2)
    s = jnp.einsum("bnqhgd,bnkhd->bnhgqk", qb, kk,
                   preferred_element_type=jnp.float32) * (dh ** -0.5)
    qi = jnp.arange(BLOCK)[:, None] + BLOCK
    ki = jnp.arange(2 * BLOCK)[None, :]
    dist = qi - ki
    local_ok = (dist >= 0) & (dist < WINDOW)
    blk_ok = (jnp.arange(nb)[:, None, None] > 0) | (ki[None] >= BLOCK)
    mask = local_ok[None] & blk_ok
    bias = rel_bias[_t5_bucket(jnp.maximum(dist, 0))]
    bias = bias.transpose(2, 0, 1).reshape(hkv, grp, BLOCK, 2 * BLOCK).astype(jnp.float32)
    s = jnp.where(mask[None, :, None, None], s + bias, NEG_INF)
    sk = sink.astype(jnp.float32).reshape(hkv, grp)[:, :, None, None]
    m = jnp.maximum(jnp.max(s, axis=-1, keepdims=True), sk)
    p = jnp.exp(s - m)
    p = p / (jnp.sum(p, axis=-1, keepdims=True) + jnp.exp(sk - m))
    o = jnp.einsum("bnhgqk,bnkhd->bnqhgd", p.astype(v.dtype), vv)
    return o.reshape(b, t, hq * dh)


def _forgetting_attention(q, k, v, log_f):
    b, t, h, dh = q.shape
    nb = t // BLOCK
    cum = jnp.cumsum(log_f, axis=1)
    cum_k = cum.transpose(0, 2, 1)
    qb = q.reshape(b, nb, BLOCK, h, dh).transpose(1, 0, 2, 3, 4)
    cq = cum.reshape(b, nb, BLOCK, h).transpose(1, 0, 3, 2)
    kpos = jnp.arange(t)

    def one_block(args):
        q_blk, c_blk, bi = args
        s = jnp.einsum("bqhd,bkhd->bhqk", q_blk, k,
                       preferred_element_type=jnp.float32) * (dh ** -0.5)
        s = s + (c_blk[..., :, None] - cum_k[:, :, None, :])
        qpos = bi * BLOCK + jnp.arange(BLOCK)
        s = jnp.where(kpos[None, :] <= qpos[:, None], s, NEG_INF)
        p = jax.nn.softmax(s, axis=-1)
        return jnp.einsum("bhqk,bkhd->bqhd", p.astype(v.dtype), v)

    o = lax.map(one_block, (qb, cq, jnp.arange(nb)))
    return o.transpose(1, 0, 2, 3, 4).reshape(b, t, h * dh)


def _normal(k, shape, scale):
    return scale * jax.random.normal(k, shape, jnp.float32)


def setup_inputs(seed: int = 0) -> dict:
    key = jax.random.key(seed)
    ks = jax.random.split(key, 32)
    L, D = DEPTH, D_MODEL
    return {
        "x": _normal(ks[0], (BATCH, SEQ, D), 1.0),
        "rel_bias": _normal(ks[1], (N_BUCKETS, SWA_Q_HEADS), 0.5),
        "ffn1_norm": 1.0 + _normal(ks[2], (L, D), 0.05),
        "ffn1_w_gate": _normal(ks[3], (L, D, D_FF), D ** -0.5),
        "ffn1_w_up": _normal(ks[4], (L, D, D_FF), D ** -0.5),
        "ffn1_w_down": _normal(ks[5], (L, D_FF, D), D_FF ** -0.5),
        "mix_norm": 1.0 + _normal(ks[6], (L, D), 0.05),
        "w_in": _normal(ks[7], (L, D, IN_COLS), D ** -0.5),
        "b_forget": 3.0 + _normal(ks[8], (L, FOX_HEADS), 0.5),
        "conf_dw": _normal(ks[9], (L, CONV_K, CONV_CH), CONV_K ** -0.5),
        "conf_dw_b": _normal(ks[10], (L, CONV_CH), 0.02),
        "conf_ln_g": 1.0 + _normal(ks[11], (L, CONV_CH), 0.05),
        "conf_ln_b": _normal(ks[12], (L, CONV_CH), 0.02),
        "conf_w_out": _normal(ks[13], (L, CONV_CH, D), CONV_CH ** -0.5),
        "sc_conv": _normal(ks[14], (L, SC_K, SC_CH), SC_K ** -0.5),
        "sc_w_out": _normal(ks[15], (L, SC_CH, D), SC_CH ** -0.5),
        "swa_q_norm": 1.0 + _normal(ks[16], (L, HEAD_DIM), 0.05),
        "swa_k_norm": 1.0 + _normal(ks[17], (L, HEAD_DIM), 0.05),
        "swa_sink": _normal(ks[18], (L, SWA_Q_HEADS), 0.5),
        "swa_w_o": _normal(ks[19], (L, SWA_Q_HEADS * HEAD_DIM, D), (SWA_Q_HEADS * HEAD_DIM) ** -0.5),
        "fox_q_norm": 1.0 + _normal(ks[20], (L, HEAD_DIM), 0.05),
        "fox_k_norm": 1.0 + _normal(ks[21], (L, HEAD_DIM), 0.05),
        "fox_w_o": _normal(ks[22], (L, FOX_HEADS * HEAD_DIM, D), (FOX_HEADS * HEAD_DIM) ** -0.5),
        "w_out": _normal(ks[23], (L, D, D), D ** -0.5),
        "ffn2_norm": 1.0 + _normal(ks[24], (L, D), 0.05),
        "ffn2_w_gate": _normal(ks[25], (L, D, D_FF), D ** -0.5),
        "ffn2_w_up": _normal(ks[26], (L, D, D_FF), D ** -0.5),
        "ffn2_w_down": _normal(ks[27], (L, D_FF, D), D_FF ** -0.5),
    }


def reference(x, rel_bias, ffn1_norm, ffn1_w_gate, ffn1_w_up, ffn1_w_down, mix_norm, w_in,
              b_forget, conf_dw, conf_dw_b, conf_ln_g, conf_ln_b, conf_w_out, sc_conv, sc_w_out,
              swa_q_norm, swa_k_norm, swa_sink, swa_w_o, fox_q_norm, fox_k_norm, fox_w_o, w_out,
              ffn2_norm, ffn2_w_gate, ffn2_w_up, ffn2_w_down):
    bsz, t = x.shape[0], x.shape[1]
    for l in range(DEPTH):
        x = x + 0.5 * _swiglu(_rmsnorm(x, ffn1_norm[l]), ffn1_w_gate[l], ffn1_w_up[l], ffn1_w_down[l])

        h = _rmsnorm(x, mix_norm[l])
        z = h @ w_in[l]
        (c_a, c_b, s_b, s_c, s_x, a_q, a_k, a_v,
         f_q, f_k, f_v, f_f, z_gate) = jnp.split(z, SPLITS, axis=-1)

        u = c_a * jax.nn.sigmoid(c_b)
        u = _causal_depthwise_conv(u, conf_dw[l]) + conf_dw_b[l]
        u = jax.nn.silu(_layernorm(u, conf_ln_g[l], conf_ln_b[l]))
        p_conf = u @ conf_w_out[l]

        p_sc = (s_b * _causal_depthwise_conv(s_c * s_x, sc_conv[l])) @ sc_w_out[l]

        q = _rmsnorm(a_q.reshape(bsz, t, SWA_Q_HEADS, HEAD_DIM), swa_q_norm[l])
        k = _rmsnorm(a_k.reshape(bsz, t, SWA_KV_HEADS, HEAD_DIM), swa_k_norm[l])
        v = a_v.reshape(bsz, t, SWA_KV_HEADS, HEAD_DIM)
        p_swa = _sliding_window_attention(q, k, v, swa_sink[l], rel_bias) @ swa_w_o[l]

        q2 = _rmsnorm(f_q.reshape(bsz, t, FOX_HEADS, HEAD_DIM), fox_q_norm[l])
        k2 = _rmsnorm(f_k.reshape(bsz, t, FOX_HEADS, HEAD_DIM), fox_k_norm[l])
        v2 = f_v.reshape(bsz, t, FOX_HEADS, HEAD_DIM)
        log_f = jax.nn.log_sigmoid(f_f.astype(jnp.float32) + b_forget[l].astype(jnp.float32))
        p_fox = _forgetting_attention(q2, k2, v2, log_f) @ fox_w_o[l]

        g = jax.nn.sigmoid(z_gate).reshape(bsz, t, N_BRANCH, D_MODEL)
        merged = (g[:, :, 0] * p_conf + g[:, :, 1] * p_sc
                  + g[:, :, 2] * p_swa + g[:, :, 3] * p_fox)
        x = x + merged @ w_out[l]

        x = x + 0.5 * _swiglu(_rmsnorm(x, ffn2_norm[l]), ffn2_w_gate[l], ffn2_w_up[l], ffn2_w_down[l])
    return x
```

```python
import functools
import math

import numpy as np
import jax
import jax.numpy as jnp
from jax import lax
from jax.experimental import pallas as pl
from jax.experimental.pallas import tpu as pltpu

F32 = jnp.float32
BF16 = jnp.bfloat16

D_MODEL = 1024
HEAD_DIM = 64
CONV_CH = 256
CONV_K = 31
SC_CH = 256
SC_K = 3
SWA_Q_HEADS = 4
SWA_KV_HEADS = 2
WINDOW = 128
FOX_HEADS = 4
N_BUCKETS = 32
MAX_DISTANCE = 128
D_FF = 2816
N_BRANCH = 4
EPS = 1e-6
NEG_INF = -1e30

LANES = 128
MAIN_COLS = 2560
FGATE_COLS = LANES
PROJ_COLS = MAIN_COLS + FGATE_COLS
GATE_COL0 = MAIN_COLS + FOX_HEADS
FOX_QK_DIM = 2 * HEAD_DIM
VMEM_LIMIT = 56 * 1024 * 1024

ROW_TILE = 512
CONV_HALO = 32
SC_HALO = 8
SWA_TILE = 512
FOX_TQ = 512
FOX_TK = 512


def _params(n_axes):
    return pltpu.CompilerParams(dimension_semantics=("arbitrary",) * n_axes,
                                vmem_limit_bytes=VMEM_LIMIT)


def _resident(shape):
    zeros = (0,) * len(shape)
    return pl.BlockSpec(shape, lambda *_: zeros, pipeline_mode=pl.Buffered(1))


def _rms(x):
    return x * lax.rsqrt(jnp.mean(x * x, axis=-1, keepdims=True) + EPS)


def _sigmoid(x):
    return 1.0 / (1.0 + jnp.exp(-x))


def _ffn_kernel(x_ref, g_ref, wg_ref, wu_ref, wd_ref, o_ref):
    x = x_ref[...]
    xn = (_rms(x) * g_ref[...]).astype(BF16)
    gate = jnp.dot(xn, wg_ref[...], preferred_element_type=F32)
    up = jnp.dot(xn, wu_ref[...], preferred_element_type=F32)
    act = (gate * _sigmoid(gate) * up).astype(BF16)
    y = jnp.dot(act, wd_ref[...], preferred_element_type=F32)
    o_ref[...] = x + 0.5 * y


def _ffn(x2d, g, wg, wu, wd):
    n, d = x2d.shape
    row = pl.BlockSpec((ROW_TILE, d), lambda i: (i, 0))
    return pl.pallas_call(
        _ffn_kernel,
        grid=(n // ROW_TILE,),
        in_specs=[row, _resident((1, d)), _resident(wg.shape), _resident(wu.shape),
                  _resident(wd.shape)],
        out_specs=row,
        out_shape=jax.ShapeDtypeStruct((n, d), F32),
        compiler_params=_params(1),
        name="ffn",
    )(x2d, g.reshape(1, d), wg, wu, wd)


def _split3(v):
    hi = v.astype(BF16).astype(F32)
    r = v - hi
    mid = r.astype(BF16).astype(F32)
    return hi, mid, r - mid


def _inproj_kernel(x_ref, g_ref, w_ref, aqg_ref, akg_ref, fqg_ref, fkg_ref, bf_ref,
                   zc_ref, zs_ref, qa_ref, ka_ref, va_ref, qf_ref, kf_ref, vf_ref, carry_ref):
    tm = x_ref.shape[1]

    @pl.when(pl.program_id(1) == 0)
    def _():
        carry_ref[...] = jnp.zeros_like(carry_ref)

    h = (_rms(x_ref[0]) * g_ref[...]).astype(BF16)
    z = jnp.dot(h, w_ref[...], preferred_element_type=F32)

    zc_ref[0] = z[:, 0:2 * CONV_CH]
    zs_ref[0] = z[:, 2 * CONV_CH:2 * CONV_CH + 3 * SC_CH]

    def head(col0, i):
        return z[:, col0 + i * HEAD_DIM: col0 + (i + 1) * HEAD_DIM]

    c0 = 2 * CONV_CH + 3 * SC_CH
    scale = HEAD_DIM ** -0.5
    for i in range(SWA_Q_HEADS):
        qa_ref[0, i] = (_rms(head(c0, i)) * (aqg_ref[...] * scale)).astype(BF16)
    c0 += SWA_Q_HEADS * HEAD_DIM
    for i in range(SWA_KV_HEADS):
        ka_ref[0, i] = (_rms(head(c0, i)) * akg_ref[...]).astype(BF16)
    c0 += SWA_KV_HEADS * HEAD_DIM
    for i in range(SWA_KV_HEADS):
        va_ref[0, i] = head(c0, i).astype(BF16)
    c0 += SWA_KV_HEADS * HEAD_DIM

    ff = z[:, MAIN_COLS:PROJ_COLS] + bf_ref[...]
    logf = jnp.minimum(ff, 0.0) - jnp.log1p(jnp.exp(-jnp.abs(ff)))
    rows = lax.broadcasted_iota(jnp.int32, logf.shape, 0)
    cum = logf
    step = 1
    while step < tm:
        cum = cum + jnp.where(rows >= step, pltpu.roll(cum, step, axis=0), 0.0)
        step *= 2
    cum = cum + carry_ref[...]
    carry_ref[...] = cum[tm - 1:tm, :]

    lane = lax.broadcasted_iota(jnp.int32, (tm, HEAD_DIM), 1)
    for i in range(FOX_HEADS):
        hi, mid, lo = _split3(cum[:, i:i + 1])
        q_extra = jnp.where(lane == 0, hi, jnp.where(lane == 1, mid, jnp.where(
            lane == 2, lo, jnp.where(lane < 6, 1.0, 0.0))))
        k_extra = jnp.where(lane < 3, 1.0, jnp.where(lane == 3, -hi, jnp.where(
            lane == 4, -mid, jnp.where(lane == 5, -lo, 0.0))))
        qn = _rms(head(c0, i)) * (fqg_ref[...] * scale)
        kn = _rms(head(c0 + FOX_HEADS * HEAD_DIM, i)) * fkg_ref[...]
        qf_ref[0, i] = jnp.concatenate([qn, q_extra], axis=-1).astype(BF16)
        kf_ref[0, i] = jnp.concatenate([kn, k_extra], axis=-1).astype(BF16)
        vf_ref[0, i] = head(c0 + 2 * FOX_HEADS * HEAD_DIM, i).astype(BF16)


def _inproj(x, g, w, aqg, akg, fqg, fkg, bf):
    b, t, d = x.shape
    tm = ROW_TILE

    def heads(n, width):
        return pl.BlockSpec((1, n, tm, width), lambda bi, ti: (bi, 0, ti, 0))

    def rows(width):
        return pl.BlockSpec((1, tm, width), lambda bi, ti: (bi, ti, 0))

    def hshape(n, width):
        return jax.ShapeDtypeStruct((b, n, t, width), BF16)

    return pl.pallas_call(
        _inproj_kernel,
        grid=(b, t // tm),
        in_specs=[rows(d), _resident((1, d)), _resident(w.shape)]
                 + [_resident((1, HEAD_DIM))] * 4 + [_resident((1, FGATE_COLS))],
        out_specs=[rows(2 * CONV_CH), rows(3 * SC_CH),
                   heads(SWA_Q_HEADS, HEAD_DIM), heads(SWA_KV_HEADS, HEAD_DIM),
                   heads(SWA_KV_HEADS, HEAD_DIM),
                   heads(FOX_HEADS, FOX_QK_DIM), heads(FOX_HEADS, FOX_QK_DIM),
                   heads(FOX_HEADS, HEAD_DIM)],
        out_shape=[jax.ShapeDtypeStruct((b, t, 2 * CONV_CH), F32),
                   jax.ShapeDtypeStruct((b, t, 3 * SC_CH), F32),
                   hshape(SWA_Q_HEADS, HEAD_DIM), hshape(SWA_KV_HEADS, HEAD_DIM),
                   hshape(SWA_KV_HEADS, HEAD_DIM),
                   hshape(FOX_HEADS, FOX_QK_DIM), hshape(FOX_HEADS, FOX_QK_DIM),
                   hshape(FOX_HEADS, HEAD_DIM)],
        scratch_shapes=[pltpu.VMEM((1, FGATE_COLS), F32)],
        compiler_params=_params(2),
        name="inproj",
    )(x, g.reshape(1, d), w, aqg.reshape(1, -1), akg.reshape(1, -1), fqg.reshape(1, -1),
      fkg.reshape(1, -1), bf)


CONV_CHUNK = 64


def _conv_kernel(zc_ref, zch_ref, zs_ref, zsh_ref, dw_ref, dwb_ref, lng_ref, lnb_ref, scw_ref,
                 uc_ref, us_ref, ext_ref, ext2_ref):
    tt = zc_ref.shape[1]
    keep = (pl.program_id(1) > 0).astype(F32)

    def glu(v):
        return v[:, :CONV_CH] * _sigmoid(v[:, CONV_CH:])

    ext_ref[0:CONV_HALO, :] = glu(zch_ref[0]) * keep
    ext_ref[CONV_HALO:, :] = glu(zc_ref[0])
    zsh = zsh_ref[0]
    ext2_ref[0:SC_HALO, :] = zsh[:, SC_CH:2 * SC_CH] * zsh[:, 2 * SC_CH:] * keep
    zs = zs_ref[0]
    ext2_ref[SC_HALO:, :] = zs[:, SC_CH:2 * SC_CH] * zs[:, 2 * SC_CH:]

    for r0 in range(0, tt, CONV_CHUNK):
        acc = jnp.broadcast_to(dwb_ref[...], (CONV_CHUNK, CONV_CH))
        for k in range(CONV_K):
            start = r0 + CONV_HALO - (CONV_K - 1) + k
            acc = acc + dw_ref[k:k + 1, :] * ext_ref[start:start + CONV_CHUNK, :]
        mu = jnp.mean(acc, axis=-1, keepdims=True)
        cen = acc - mu
        var = jnp.mean(cen * cen, axis=-1, keepdims=True)
        y = cen * lax.rsqrt(var + EPS) * lng_ref[...] + lnb_ref[...]
        uc_ref[0, r0:r0 + CONV_CHUNK, :] = (y * _sigmoid(y)).astype(BF16)

        acc2 = jnp.zeros((CONV_CHUNK, SC_CH), F32)
        for k in range(SC_K):
            start = r0 + SC_HALO - (SC_K - 1) + k
            acc2 = acc2 + scw_ref[k:k + 1, :] * ext2_ref[start:start + CONV_CHUNK, :]
        us_ref[0, r0:r0 + CONV_CHUNK, :] = (zs[r0:r0 + CONV_CHUNK, :SC_CH] * acc2).astype(BF16)


def _conv(zc, zs, dw, dwb, lng, lnb, scw):
    b, t, _ = zc.shape
    tt = ROW_TILE

    def rows(width):
        return pl.BlockSpec((1, tt, width), lambda bi, ti: (bi, ti, 0))

    def halo(rows_, width):
        per = tt // rows_
        return pl.BlockSpec((1, rows_, width),
                            lambda bi, ti: (bi, jnp.maximum(ti * per - 1, 0), 0))

    return pl.pallas_call(
        _conv_kernel,
        grid=(b, t // tt),
        in_specs=[rows(2 * CONV_CH), halo(CONV_HALO, 2 * CONV_CH),
                  rows(3 * SC_CH), halo(SC_HALO, 3 * SC_CH),
                  _resident(dw.shape), _resident((1, CONV_CH)), _resident((1, CONV_CH)),
                  _resident((1, CONV_CH)), _resident(scw.shape)],
        out_specs=[rows(CONV_CH), rows(SC_CH)],
        out_shape=[jax.ShapeDtypeStruct((b, t, CONV_CH), BF16),
                   jax.ShapeDtypeStruct((b, t, SC_CH), BF16)],
        scratch_shapes=[pltpu.VMEM((CONV_HALO + tt, CONV_CH), F32),
                        pltpu.VMEM((SC_HALO + tt, SC_CH), F32)],
        compiler_params=_params(2),
        name="conv",
    )(zc, zc, zs, zs, dw, dwb.reshape(1, -1), lng.reshape(1, -1), lnb.reshape(1, -1), scw)


def _t5_bucket_table():
    max_exact = N_BUCKETS // 2
    dist = np.maximum(np.arange(WINDOW)[:, None] + WINDOW - np.arange(2 * WINDOW)[None, :], 0)
    d = np.maximum(dist, 1).astype(np.float32)
    large = max_exact + (np.log(d / np.float32(max_exact)) / np.float32(
        math.log(MAX_DISTANCE / max_exact)) * np.float32(N_BUCKETS - max_exact)).astype(np.int32)
    large = np.minimum(large, N_BUCKETS - 1)
    return np.where(dist < max_exact, dist, large).astype(np.int32)


def _swa_kernel(rb_ref, sink_ref, bucket_ref, q_ref, k_ref, kh_ref, v_ref, vh_ref, o_ref,
                bias_ref, kk_ref, vv_ref):
    tq = q_ref.shape[2]
    first = (pl.program_id(0) == 0) & (pl.program_id(1) == 0)

    @pl.when(first)
    def _():
        bucket = bucket_ref[...]
        qi = lax.broadcasted_iota(jnp.int32, bucket.shape, 0) + WINDOW
        ki = lax.broadcasted_iota(jnp.int32, bucket.shape, 1)
        dist = qi - ki
        ok = (dist >= 0) & (dist < WINDOW)
        for h in range(SWA_Q_HEADS):
            bias = jnp.zeros(bucket.shape, F32)
            for bk in range(N_BUCKETS):
                bias = jnp.where(bucket == bk, rb_ref[bk, h], bias)
            bias_ref[h] = jnp.where(ok, bias, NEG_INF)

    kk_ref[:, 0:WINDOW, :] = kh_ref[0]
    kk_ref[:, WINDOW:, :] = k_ref[0]
    vv_ref[:, 0:WINDOW, :] = vh_ref[0]
    vv_ref[:, WINDOW:, :] = v_ref[0]
    grp = SWA_Q_HEADS // SWA_KV_HEADS
    seq_start = pl.program_id(1) == 0
    prev_half = lax.broadcasted_iota(jnp.int32, (WINDOW, 2 * WINDOW), 1) < WINDOW

    for sb in range(tq // WINDOW):
        r0 = sb * WINDOW
        for h in range(SWA_Q_HEADS):
            q = q_ref[0, h, r0:r0 + WINDOW, :]
            kk = kk_ref[h // grp, r0:r0 + 2 * WINDOW, :]
            vv = vv_ref[h // grp, r0:r0 + 2 * WINDOW, :]
            s = lax.dot_general(q, kk, (((1,), (1,)), ((), ())), preferred_element_type=F32)
            s = s + bias_ref[h]
            if sb == 0:
                s = jnp.where(seq_start & prev_half, NEG_INF, s)
            sink = sink_ref[h]
            m = jnp.maximum(jnp.max(s, axis=-1, keepdims=True), sink)
            p = jnp.exp(s - m)
            denom = jnp.sum(p, axis=-1, keepdims=True) + jnp.exp(sink - m)
            o = jnp.dot(p.astype(BF16), vv, preferred_element_type=F32)
            o_ref[0, h, r0:r0 + WINDOW, :] = (o / denom).astype(BF16)


def _swa(rel_bias, sink, qa, ka, va):
    b, _, t, dh = qa.shape
    tq = SWA_TILE
    per = tq // WINDOW
    smem = pl.BlockSpec(memory_space=pltpu.SMEM)

    def cur(n):
        return pl.BlockSpec((1, n, tq, dh), lambda bi, ti: (bi, 0, ti, 0))

    def halo(n):
        return pl.BlockSpec((1, n, WINDOW, dh),
                            lambda bi, ti: (bi, 0, jnp.maximum(ti * per - 1, 0), 0))

    return pl.pallas_call(
        _swa_kernel,
        grid=(b, t // tq),
        in_specs=[smem, smem, _resident((WINDOW, 2 * WINDOW)),
                  cur(SWA_Q_HEADS), cur(SWA_KV_HEADS), halo(SWA_KV_HEADS),
                  cur(SWA_KV_HEADS), halo(SWA_KV_HEADS)],
        out_specs=cur(SWA_Q_HEADS),
        out_shape=jax.ShapeDtypeStruct((b, SWA_Q_HEADS, t, dh), BF16),
        scratch_shapes=[pltpu.VMEM((SWA_Q_HEADS, WINDOW, 2 * WINDOW), F32),
                        pltpu.VMEM((SWA_KV_HEADS, WINDOW + tq, dh), BF16),
                        pltpu.VMEM((SWA_KV_HEADS, WINDOW + tq, dh), BF16)],
        compiler_params=_params(2),
        name="swa",
    )(rel_bias, sink, jnp.asarray(_t5_bucket_table()), qa, ka, ka, va, va)


def _fox_kernel(q_ref, k_ref, v_ref, o_ref):
    tq = q_ref.shape[2]
    qi = pl.program_id(2)
    q = q_ref[0, 0]

    def step(j, carry, masked):
        m, l, acc = carry
        k0 = pl.multiple_of(j * FOX_TK, FOX_TK)
        ks = k_ref[0, 0, pl.ds(k0, FOX_TK), :]
        vs = v_ref[0, 0, pl.ds(k0, FOX_TK), :]
        s = lax.dot_general(q, ks, (((1,), (1,)), ((), ())), preferred_element_type=F32)
        if masked:
            row = lax.broadcasted_iota(jnp.int32, s.shape, 0)
            col = lax.broadcasted_iota(jnp.int32, s.shape, 1)
            s = jnp.where(col <= row, s, NEG_INF)
        m_new = jnp.maximum(m, jnp.max(s, axis=-1, keepdims=True))
        alpha = jnp.exp(m - m_new)
        p = jnp.exp(s - m_new)
        l = alpha * l + jnp.sum(p, axis=-1, keepdims=True)
        acc = alpha * acc + jnp.dot(p.astype(BF16), vs, preferred_element_type=F32)
        return m_new, l, acc

    init = (jnp.full((tq, 1), NEG_INF, F32), jnp.zeros((tq, 1), F32),
            jnp.zeros((tq, HEAD_DIM), F32))
    carry = lax.fori_loop(0, qi, lambda j, c: step(j, c, False), init)
    _, l, acc = step(qi, carry, True)
    o_ref[0, 0] = (acc / l).astype(BF16)


def _fox(qf, kf, vf):
    b, h, t, _ = qf.shape
    assert FOX_TQ == FOX_TK
    return pl.pallas_call(
        _fox_kernel,
        grid=(b, h, t // FOX_TQ),
        in_specs=[pl.BlockSpec((1, 1, FOX_TQ, FOX_QK_DIM), lambda bi, hi, qi: (bi, hi, qi, 0)),
                  pl.BlockSpec((1, 1, t, FOX_QK_DIM), lambda bi, hi, qi: (bi, hi, 0, 0)),
                  pl.BlockSpec((1, 1, t, HEAD_DIM), lambda bi, hi, qi: (bi, hi, 0, 0))],
        out_specs=pl.BlockSpec((1, 1, FOX_TQ, HEAD_DIM), lambda bi, hi, qi: (bi, hi, qi, 0)),
        out_shape=jax.ShapeDtypeStruct((b, h, t, HEAD_DIM), BF16),
        compiler_params=_params(3),
        name="fox",
    )(qf, kf, vf)


def _merge_kernel(x_ref, g_ref, uc_ref, us_ref, oa_ref, of_ref, wb_ref, wgate_ref, wout_ref, o_ref):
    x = x_ref[0]
    h = (_rms(x) * g_ref[...]).astype(BF16)
    branches = [uc_ref[0], us_ref[0],
                jnp.concatenate([oa_ref[0, i] for i in range(SWA_Q_HEADS)], axis=-1),
                jnp.concatenate([of_ref[0, i] for i in range(FOX_HEADS)], axis=-1)]
    merged = jnp.zeros(x.shape, F32)
    for i, u in enumerate(branches):
        p = jnp.dot(u, wb_ref[i], preferred_element_type=F32)
        gate = jnp.dot(h, wgate_ref[:, i * D_MODEL:(i + 1) * D_MODEL], preferred_element_type=F32)
        merged = merged + _sigmoid(gate) * p
    o_ref[0] = x + jnp.dot(merged.astype(BF16), wout_ref[...], preferred_element_type=F32)


def _merge(x, g, uc, us, oa, of, wb, wgate, wout):
    b, t, d = x.shape
    tm = ROW_TILE

    def rows(width):
        return pl.BlockSpec((1, tm, width), lambda bi, ti: (bi, ti, 0))

    def heads(n):
        return pl.BlockSpec((1, n, tm, HEAD_DIM), lambda bi, ti: (bi, 0, ti, 0))

    return pl.pallas_call(
        _merge_kernel,
        grid=(b, t // tm),
        in_specs=[rows(d), _resident((1, d)), rows(CONV_CH), rows(SC_CH),
                  heads(SWA_Q_HEADS), heads(FOX_HEADS),
                  _resident(wb.shape), _resident(wgate.shape), _resident(wout.shape)],
        out_specs=rows(d),
        out_shape=jax.ShapeDtypeStruct((b, t, d), F32),
        compiler_params=_params(2),
        name="merge",
    )(x, g.reshape(1, d), uc, us, oa, of, wb, wgate, wout)


def kernel(x, rel_bias, ffn1_norm, ffn1_w_gate, ffn1_w_up, ffn1_w_down, mix_norm, w_in, b_forget, conf_dw, conf_dw_b, conf_ln_g, conf_ln_b, conf_w_out, sc_conv, sc_w_out, swa_q_norm, swa_k_norm, swa_sink, swa_w_o, fox_q_norm, fox_k_norm, fox_w_o, w_out, ffn2_norm, ffn2_w_gate, ffn2_w_up, ffn2_w_down):
    b, t, d = x.shape
    depth = w_in.shape[0]
    for l in range(depth):
        x = _ffn(x.reshape(b * t, d), ffn1_norm[l], ffn1_w_gate[l].astype(BF16),
                 ffn1_w_up[l].astype(BF16), ffn1_w_down[l].astype(BF16)).reshape(b, t, d)

        w_proj = jnp.pad(w_in[l][:, :GATE_COL0], ((0, 0), (0, PROJ_COLS - GATE_COL0))).astype(BF16)
        w_gate = w_in[l][:, GATE_COL0:].astype(BF16)
        bf = jnp.pad(b_forget[l], (0, FGATE_COLS - FOX_HEADS)).reshape(1, FGATE_COLS)
        zc, zs, qa, ka, va, qf, kf, vf = _inproj(
            x, mix_norm[l], w_proj, swa_q_norm[l], swa_k_norm[l], fox_q_norm[l], fox_k_norm[l], bf)

        uc, us = _conv(zc, zs, conf_dw[l], conf_dw_b[l], conf_ln_g[l], conf_ln_b[l], sc_conv[l])
        oa = _swa(rel_bias, swa_sink[l], qa, ka, va)
        of = _fox(qf, kf, vf)

        wb = jnp.stack([conf_w_out[l], sc_w_out[l], swa_w_o[l], fox_w_o[l]]).astype(BF16)
        x = _merge(x, mix_norm[l], uc, us, oa, of, wb, w_gate, w_out[l].astype(BF16))

        x = _ffn(x.reshape(b * t, d), ffn2_norm[l], ffn2_w_gate[l].astype(BF16),
                 ffn2_w_up[l].astype(BF16), ffn2_w_down[l].astype(BF16)).reshape(b, t, d)
    return x
```

```python
import math

import numpy as np
import jax
import jax.numpy as jnp
from jax import lax
from jax.experimental import pallas as pl
from jax.experimental.pallas import tpu as pltpu

F32 = jnp.float32
BF16 = jnp.bfloat16

D_MODEL = 1024
HEAD_DIM = 64
CONV_CH = 256
CONV_K = 31
SC_CH = 256
SC_K = 3
SWA_Q_HEADS = 4
SWA_KV_HEADS = 2
WINDOW = 128
FOX_HEADS = 4
N_BUCKETS = 32
MAX_DISTANCE = 128
D_FF = 2816
N_BRANCH = 4
EPS = 1e-6
NEG_INF = -1e30

LANES = 128
VMEM_LIMIT = 56 * 1024 * 1024

COL_CONV = 0
COL_SC = COL_CONV + 2 * CONV_CH
COL_AQ = COL_SC + 3 * SC_CH
COL_AK = COL_AQ + SWA_Q_HEADS * HEAD_DIM
COL_AV = COL_AK + SWA_KV_HEADS * HEAD_DIM
COL_FQ = COL_AV + SWA_KV_HEADS * HEAD_DIM
COL_FK = COL_FQ + FOX_HEADS * HEAD_DIM
COL_FV = COL_FK + FOX_HEADS * HEAD_DIM
COL_FF = COL_FV + FOX_HEADS * HEAD_DIM
FOX_PAIRS = FOX_HEADS // 2
FF_REP = 6
PROJ_COLS = COL_FF + FOX_PAIRS * LANES
SWA_Q_ORDER = (0, 2, 1, 3)

ROW_TILE = 512
CONV_HALO = 32
SC_HALO = 8
CONV_CHUNK = 64
SWA_TILE = 512
FOX_TQ = 512
FOX_TK = 512


def _params(n_axes):
    return pltpu.CompilerParams(dimension_semantics=("arbitrary",) * n_axes,
                                vmem_limit_bytes=VMEM_LIMIT)


def _resident(shape, layer=None):
    if layer is None:
        return pl.BlockSpec(shape, lambda *_: (0,) * len(shape), pipeline_mode=pl.Buffered(1))
    return pl.BlockSpec((None,) + tuple(shape), lambda *_: (layer,) + (0,) * len(shape),
                        pipeline_mode=pl.Buffered(1))


def _rms(x):
    return x * lax.rsqrt(jnp.mean(x * x, axis=-1, keepdims=True) + EPS)


def _sigmoid(x):
    return 1.0 / (1.0 + jnp.exp(-x))


def _ffn_kernel(x_ref, g_ref, wg_ref, wu_ref, wd_ref, o_ref):
    x = x_ref[...]
    xn = (_rms(x) * g_ref[...]).astype(BF16)
    gate = jnp.dot(xn, wg_ref[...], preferred_element_type=F32)
    up = jnp.dot(xn, wu_ref[...], preferred_element_type=F32)
    act = (gate * _sigmoid(gate) * up).astype(BF16)
    y = jnp.dot(act, wd_ref[...], preferred_element_type=F32)
    o_ref[...] = x + 0.5 * y


def _ffn(x2d, layer, g, wg, wu, wd):
    n, d = x2d.shape
    row = pl.BlockSpec((ROW_TILE, d), lambda i: (i, 0))
    return pl.pallas_call(
        _ffn_kernel,
        grid=(n // ROW_TILE,),
        in_specs=[row, _resident((1, d), layer), _resident(wg.shape[1:], layer),
                  _resident(wu.shape[1:], layer), _resident(wd.shape[1:], layer)],
        out_specs=row,
        out_shape=jax.ShapeDtypeStruct((n, d), F32),
        compiler_params=_params(1),
        name="ffn",
    )(x2d, g.reshape(g.shape[0], 1, d), wg, wu, wd)


def _split3(v):
    hi = v.astype(BF16).astype(F32)
    r = v - hi
    mid = r.astype(BF16).astype(F32)
    return hi, mid, r - mid


def _pair_rms(zb, first):
    sq = zb * zb
    s0 = jnp.sum(jnp.where(first, sq, 0.0), axis=-1, keepdims=True)
    s1 = jnp.sum(jnp.where(first, 0.0, sq), axis=-1, keepdims=True)
    inv = jnp.where(first, lax.rsqrt(s0 * (1.0 / HEAD_DIM) + EPS),
                    lax.rsqrt(s1 * (1.0 / HEAD_DIM) + EPS))
    return zb * inv


def _inproj_kernel(x_ref, g_ref, w_ref, qkg_ref, bf_ref,
                   zc_ref, zs_ref, qa_ref, ka_ref, va_ref, qf_ref, kf_ref, vft_ref, carry_ref):
    tm = x_ref.shape[1]

    @pl.when(pl.program_id(1) == 0)
    def _():
        carry_ref[...] = jnp.zeros_like(carry_ref)

    h = (_rms(x_ref[0]) * g_ref[...]).astype(BF16)
    z = jnp.dot(h, w_ref[...], preferred_element_type=F32)

    zc_ref[0] = z[:, COL_CONV:COL_SC]
    zs_ref[0] = z[:, COL_SC:COL_AQ]
    va_ref[0] = z[:, COL_AV:COL_FQ].astype(BF16)
    vft_ref[0] = z[:, COL_FV:COL_FF].T.astype(BF16)

    first = lax.broadcasted_iota(jnp.int32, (tm, LANES), 1) < HEAD_DIM

    def normed(col):
        g0 = col - COL_AQ
        return _pair_rms(z[:, col:col + LANES], first) * qkg_ref[:, g0:g0 + LANES]

    for i in range(SWA_Q_HEADS // 2):
        qa_ref[0, :, i * LANES:(i + 1) * LANES] = normed(COL_AQ + i * LANES).astype(BF16)
    ka_ref[0] = normed(COL_AK).astype(BF16)

    ff = z[:, COL_FF:PROJ_COLS] + bf_ref[...]
    logf = jnp.minimum(ff, 0.0) - jnp.log1p(jnp.exp(-jnp.abs(ff)))
    rows = lax.broadcasted_iota(jnp.int32, logf.shape, 0)
    cum = logf
    step = 1
    while step < tm:
        cum = cum + jnp.where(rows >= step, pltpu.roll(cum, step, axis=0), 0.0)
        step *= 2
    cum = cum + carry_ref[...]
    carry_ref[...] = cum[tm - 1:tm, :]

    lane = lax.broadcasted_iota(jnp.int32, (tm, LANES), 1)
    slot = lane % FF_REP
    used = lane < 2 * FF_REP
    for p in range(FOX_PAIRS):
        hi, mid, lo = _split3(cum[:, p * LANES:(p + 1) * LANES])
        q_extra = jnp.where(slot == 0, hi, jnp.where(slot == 1, mid, jnp.where(slot == 2, lo, 1.0)))
        k_extra = jnp.where(slot < 3, 1.0, jnp.where(slot == 3, -hi, jnp.where(slot == 4, -mid, -lo)))
        base = 2 * p * LANES
        qf_ref[0, :, base:base + LANES] = normed(COL_FQ + p * LANES).astype(BF16)
        qf_ref[0, :, base + LANES:base + 2 * LANES] = jnp.where(used, q_extra, 0.0).astype(BF16)
        kf_ref[0, :, base:base + LANES] = normed(COL_FK + p * LANES).astype(BF16)
        kf_ref[0, :, base + LANES:base + 2 * LANES] = jnp.where(used, k_extra, 0.0).astype(BF16)


def _inproj(x, layer, g, w, qkg, bf):
    b, t, d = x.shape
    tm = ROW_TILE

    def rows(width):
        return pl.BlockSpec((1, tm, width), lambda bi, ti: (bi, ti, 0))

    def oshape(width, dtype=BF16):
        return jax.ShapeDtypeStruct((b, t, width), dtype)

    n_v = FOX_HEADS * HEAD_DIM
    return pl.pallas_call(
        _inproj_kernel,
        grid=(b, t // tm),
        in_specs=[rows(d), _resident((1, d), layer), _resident(w.shape[1:], layer),
                  _resident(qkg.shape[1:], layer), _resident(bf.shape[1:], layer)],
        out_specs=[rows(2 * CONV_CH), rows(3 * SC_CH),
                   rows(SWA_Q_HEADS * HEAD_DIM), rows(SWA_KV_HEADS * HEAD_DIM),
                   rows(SWA_KV_HEADS * HEAD_DIM),
                   rows(2 * FOX_PAIRS * LANES), rows(2 * FOX_PAIRS * LANES),
                   pl.BlockSpec((1, n_v, tm), lambda bi, ti: (bi, 0, ti))],
        out_shape=[oshape(2 * CONV_CH, F32), oshape(3 * SC_CH, F32),
                   oshape(SWA_Q_HEADS * HEAD_DIM), oshape(SWA_KV_HEADS * HEAD_DIM),
                   oshape(SWA_KV_HEADS * HEAD_DIM),
                   oshape(2 * FOX_PAIRS * LANES), oshape(2 * FOX_PAIRS * LANES),
                   jax.ShapeDtypeStruct((b, n_v, t), BF16)],
        scratch_shapes=[pltpu.VMEM((1, FOX_PAIRS * LANES), F32)],
        compiler_params=_params(2),
        name="inproj",
    )(x, g.reshape(g.shape[0], 1, d), w, qkg, bf)


def _conv_kernel(zc_ref, zch_ref, zs_ref, zsh_ref, dw_ref, dwb_ref, lng_ref, lnb_ref, scw_ref,
                 uc_ref, us_ref, ext_ref, ext2_ref):
    tt = zc_ref.shape[1]
    keep = (pl.program_id(1) > 0).astype(F32)

    def glu(v):
        return v[:, :CONV_CH] * _sigmoid(v[:, CONV_CH:])

    ext_ref[0:CONV_HALO, :] = glu(zch_ref[0]) * keep
    ext_ref[CONV_HALO:, :] = glu(zc_ref[0])
    zsh = zsh_ref[0]
    ext2_ref[0:SC_HALO, :] = zsh[:, SC_CH:2 * SC_CH] * zsh[:, 2 * SC_CH:] * keep
    zs = zs_ref[0]
    ext2_ref[SC_HALO:, :] = zs[:, SC_CH:2 * SC_CH] * zs[:, 2 * SC_CH:]

    for r0 in range(0, tt, CONV_CHUNK):
        acc = jnp.broadcast_to(dwb_ref[...], (CONV_CHUNK, CONV_CH))
        for k in range(CONV_K):
            start = r0 + CONV_HALO - (CONV_K - 1) + k
            acc = acc + dw_ref[k:k + 1, :] * ext_ref[start:start + CONV_CHUNK, :]
        mu = jnp.mean(acc, axis=-1, keepdims=True)
        cen = acc - mu
        var = jnp.mean(cen * cen, axis=-1, keepdims=True)
        y = cen * lax.rsqrt(var + EPS) * lng_ref[...] + lnb_ref[...]
        uc_ref[0, r0:r0 + CONV_CHUNK, :] = (y * _sigmoid(y)).astype(BF16)

        acc2 = jnp.zeros((CONV_CHUNK, SC_CH), F32)
        for k in range(SC_K):
            start = r0 + SC_HALO - (SC_K - 1) + k
            acc2 = acc2 + scw_ref[k:k + 1, :] * ext2_ref[start:start + CONV_CHUNK, :]
        us_ref[0, r0:r0 + CONV_CHUNK, :] = (zs[r0:r0 + CONV_CHUNK, :SC_CH] * acc2).astype(BF16)


def _conv(zc, zs, layer, dw, dwb, lng, lnb, scw):
    b, t, _ = zc.shape
    tt = ROW_TILE

    def rows(width):
        return pl.BlockSpec((1, tt, width), lambda bi, ti: (bi, ti, 0))

    def halo(rows_, width):
        per = tt // rows_
        return pl.BlockSpec((1, rows_, width),
                            lambda bi, ti: (bi, jnp.maximum(ti * per - 1, 0), 0))

    def vec(a):
        return a.reshape(a.shape[0], 1, a.shape[1])

    return pl.pallas_call(
        _conv_kernel,
        grid=(b, t // tt),
        in_specs=[rows(2 * CONV_CH), halo(CONV_HALO, 2 * CONV_CH),
                  rows(3 * SC_CH), halo(SC_HALO, 3 * SC_CH),
                  _resident(dw.shape[1:], layer), _resident((1, CONV_CH), layer),
                  _resident((1, CONV_CH), layer), _resident((1, CONV_CH), layer),
                  _resident(scw.shape[1:], layer)],
        out_specs=[rows(CONV_CH), rows(SC_CH)],
        out_shape=[jax.ShapeDtypeStruct((b, t, CONV_CH), BF16),
                   jax.ShapeDtypeStruct((b, t, SC_CH), BF16)],
        scratch_shapes=[pltpu.VMEM((CONV_HALO + tt, CONV_CH), F32),
                        pltpu.VMEM((SC_HALO + tt, SC_CH), F32)],
        compiler_params=_params(2),
        name="conv",
    )(zc, zc, zs, zs, dw, vec(dwb), vec(lng), vec(lnb), scw)


def _t5_bucket_table():
    max_exact = N_BUCKETS // 2
    dist = np.maximum(np.arange(WINDOW)[:, None] + WINDOW - np.arange(2 * WINDOW)[None, :], 0)
    d = np.maximum(dist, 1).astype(np.float32)
    large = max_exact + (np.log(d / np.float32(max_exact)) / np.float32(
        math.log(MAX_DISTANCE / max_exact)) * np.float32(N_BUCKETS - max_exact)).astype(np.int32)
    large = np.minimum(large, N_BUCKETS - 1)
    return np.where(dist < max_exact, dist, large).astype(np.int32)


def _swa_kernel(rb_ref, sink_ref, bucket_ref, q_ref, k_ref, kh_ref, v_ref, vh_ref, o_ref,
                bias_ref, kk_ref, vv_ref):
    tq = q_ref.shape[1]
    first_step = (pl.program_id(0) == 0) & (pl.program_id(1) == 0)

    @pl.when(first_step)
    def _():
        bucket = bucket_ref[...]
        qi = lax.broadcasted_iota(jnp.int32, bucket.shape, 0) + WINDOW
        ki = lax.broadcasted_iota(jnp.int32, bucket.shape, 1)
        dist = qi - ki
        ok = (dist >= 0) & (dist < WINDOW)
        for h in range(SWA_Q_HEADS):
            bias = jnp.zeros(bucket.shape, F32)
            for bk in range(N_BUCKETS):
                bias = jnp.where(bucket == bk, rb_ref[bk, h], bias)
            bias_ref[h] = jnp.where(ok, bias, NEG_INF)

    kk_ref[0:WINDOW, :] = kh_ref[0]
    kk_ref[WINDOW:, :] = k_ref[0]
    vv_ref[0:WINDOW, :] = vh_ref[0]
    vv_ref[WINDOW:, :] = v_ref[0]
    seq_start = pl.program_id(1) == 0
    prev_half = lax.broadcasted_iota(jnp.int32, (WINDOW, 2 * WINDOW), 1) < WINDOW
    first = lax.broadcasted_iota(jnp.int32, (WINDOW, LANES), 1) < HEAD_DIM

    for sb in range(tq // WINDOW):
        r0 = sb * WINDOW
        kk = kk_ref[r0:r0 + 2 * WINDOW, :]
        vv = vv_ref[r0:r0 + 2 * WINDOW, :]
        for blk in range(SWA_Q_HEADS // 2):
            qb = q_ref[0, r0:r0 + WINDOW, blk * LANES:(blk + 1) * LANES]
            outs = []
            for half in range(2):
                h = SWA_Q_ORDER[2 * blk + half]
                keep = first if half == 0 else jnp.logical_not(first)
                q = jnp.where(keep, qb, jnp.zeros_like(qb))
                s = lax.dot_general(q, kk, (((1,), (1,)), ((), ())), preferred_element_type=F32)
                s = s + bias_ref[h]
                if sb == 0:
                    s = jnp.where(seq_start & prev_half, NEG_INF, s)
                sink = sink_ref[h]
                m = jnp.maximum(jnp.max(s, axis=-1, keepdims=True), sink)
                p = jnp.exp(s - m)
                denom = jnp.sum(p, axis=-1, keepdims=True) + jnp.exp(sink - m)
                outs.append(jnp.dot(p.astype(BF16), vv, preferred_element_type=F32) / denom)
            o_ref[0, r0:r0 + WINDOW, blk * LANES:(blk + 1) * LANES] = jnp.where(
                first, outs[0], outs[1]).astype(BF16)


def _swa(rel_bias, sink, qa, ka, va):
    b, t, _ = qa.shape
    tq = SWA_TILE
    per = tq // WINDOW
    kvw = SWA_KV_HEADS * HEAD_DIM
    smem = pl.BlockSpec(memory_space=pltpu.SMEM)
    cur_q = pl.BlockSpec((1, tq, SWA_Q_HEADS * HEAD_DIM), lambda bi, ti: (bi, ti, 0))
    cur_kv = pl.BlockSpec((1, tq, kvw), lambda bi, ti: (bi, ti, 0))
    halo = pl.BlockSpec((1, WINDOW, kvw), lambda bi, ti: (bi, jnp.maximum(ti * per - 1, 0), 0))
    return pl.pallas_call(
        _swa_kernel,
        grid=(b, t // tq),
        in_specs=[smem, smem, _resident((WINDOW, 2 * WINDOW)),
                  cur_q, cur_kv, halo, cur_kv, halo],
        out_specs=cur_q,
        out_shape=jax.ShapeDtypeStruct(qa.shape, BF16),
        scratch_shapes=[pltpu.VMEM((SWA_Q_HEADS, WINDOW, 2 * WINDOW), F32),
                        pltpu.VMEM((WINDOW + tq, kvw), BF16),
                        pltpu.VMEM((WINDOW + tq, kvw), BF16)],
        compiler_params=_params(2),
        name="swa",
    )(rel_bias, sink, jnp.asarray(_t5_bucket_table()), qa, ka, ka, va, va)


def _fox_kernel(q_ref, k_ref, vt_ref, o_ref):
    tq = q_ref.shape[1]
    qi = pl.program_id(2)
    qp = q_ref[0]
    lane = lax.broadcasted_iota(jnp.int32, qp.shape, 1)
    zero = jnp.zeros_like(qp)
    q_heads = [
        jnp.where((lane < HEAD_DIM) | ((lane >= LANES) & (lane < LANES + FF_REP)), qp, zero),
        jnp.where(((lane >= HEAD_DIM) & (lane < LANES)) |
                  ((lane >= LANES + FF_REP) & (lane < LANES + 2 * FF_REP)), qp, zero)]

    def step(j, carry, masked):
        k0 = pl.multiple_of(j * FOX_TK, FOX_TK)
        ks = k_ref[0, pl.ds(k0, FOX_TK), :]
        vt = vt_ref[0, :, pl.ds(k0, FOX_TK)]
        out = []
        for hd in range(2):
            m, l, acc = carry[hd]
            st = lax.dot_general(ks, q_heads[hd], (((1,), (1,)), ((), ())),
                                 preferred_element_type=F32)
            if masked:
                key = lax.broadcasted_iota(jnp.int32, st.shape, 0)
                qry = lax.broadcasted_iota(jnp.int32, st.shape, 1)
                st = jnp.where(key <= qry, st, NEG_INF)
            m_new = jnp.maximum(m, jnp.max(st, axis=0, keepdims=True))
            alpha = jnp.exp(m - m_new)
            p = jnp.exp(st - m_new)
            l = alpha * l + jnp.sum(p, axis=0, keepdims=True)
            acc = alpha * acc + jnp.dot(vt, p.astype(BF16), preferred_element_type=F32)
            out.append((m_new, l, acc))
        return tuple(out)

    init = tuple((jnp.full((1, tq), NEG_INF, F32), jnp.zeros((1, tq), F32),
                  jnp.zeros((LANES, tq), F32)) for _ in range(2))
    carry = lax.fori_loop(0, qi, lambda j, c: step(j, c, False), init)
    (_, l0, acc0), (_, l1, acc1) = step(qi, carry, True)
    row = lax.broadcasted_iota(jnp.int32, acc0.shape, 0)
    out_t = jnp.where(row < HEAD_DIM, acc0 / l0, acc1 / l1)
    o_ref[0] = out_t.T.astype(BF16)


def _fox(qf, kf, vft):
    b, t, _ = qf.shape
    assert FOX_TQ == FOX_TK
    return pl.pallas_call(
        _fox_kernel,
        grid=(b, FOX_PAIRS, t // FOX_TQ),
        in_specs=[pl.BlockSpec((1, FOX_TQ, 2 * LANES), lambda bi, pi, qi: (bi, qi, pi)),
                  pl.BlockSpec((1, t, 2 * LANES), lambda bi, pi, qi: (bi, 0, pi)),
                  pl.BlockSpec((1, LANES, t), lambda bi, pi, qi: (bi, pi, 0))],
        out_specs=pl.BlockSpec((1, FOX_TQ, LANES), lambda bi, pi, qi: (bi, qi, pi)),
        out_shape=jax.ShapeDtypeStruct((b, t, FOX_HEADS * HEAD_DIM), BF16),
        compiler_params=_params(3),
        name="fox",
    )(qf, kf, vft)


def _merge_kernel(x_ref, g_ref, uc_ref, us_ref, oa_ref, of_ref, wb_ref, wgate_ref, wout_ref, o_ref):
    x = x_ref[0]
    h = (_rms(x) * g_ref[...]).astype(BF16)
    merged = jnp.zeros(x.shape, F32)
    for i, u_ref in enumerate((uc_ref, us_ref, oa_ref, of_ref)):
        p = jnp.dot(u_ref[0], wb_ref[i], preferred_element_type=F32)
        gate = jnp.dot(h, wgate_ref[:, i * D_MODEL:(i + 1) * D_MODEL], preferred_element_type=F32)
        merged = merged + _sigmoid(gate) * p
    o_ref[0] = x + jnp.dot(merged.astype(BF16), wout_ref[...], preferred_element_type=F32)


def _merge(x, layer, g, uc, us, oa, of, wb, wgate, wout):
    b, t, d = x.shape
    tm = ROW_TILE

    def rows(width):
        return pl.BlockSpec((1, tm, width), lambda bi, ti: (bi, ti, 0))

    return pl.pallas_call(
        _merge_kernel,
        grid=(b, t // tm),
        in_specs=[rows(d), _resident((1, d), layer), rows(CONV_CH), rows(SC_CH),
                  rows(SWA_Q_HEADS * HEAD_DIM), rows(FOX_HEADS * HEAD_DIM),
                  _resident(wb.shape[1:], layer), _resident(wgate.shape[1:], layer),
                  _resident(wout.shape[1:], layer)],
        out_specs=rows(d),
        out_shape=jax.ShapeDtypeStruct((b, t, d), F32),
        compiler_params=_params(2),
        name="merge",
    )(x, g.reshape(g.shape[0], 1, d), uc, us, oa, of, wb, wgate, wout)


def _head_cols(w, col0, order):
    return [w[..., col0 + h * HEAD_DIM: col0 + (h + 1) * HEAD_DIM] for h in order]


def _fgate_cols(ff):
    rep = jnp.repeat(ff, FF_REP, axis=-1)
    pad = [(0, 0)] * (ff.ndim - 1) + [(0, LANES - 2 * FF_REP)]
    return jnp.concatenate([jnp.pad(rep[..., p * 2 * FF_REP:(p + 1) * 2 * FF_REP], pad)
                            for p in range(FOX_PAIRS)], axis=-1)


def _prep_proj(w_in):
    aq0 = COL_AQ
    ff0 = COL_FF
    w_proj = jnp.concatenate(
        [w_in[..., :aq0]] + _head_cols(w_in, aq0, SWA_Q_ORDER)
        + [w_in[..., COL_AK:ff0], _fgate_cols(w_in[..., ff0:ff0 + FOX_HEADS])], axis=-1)
    w_gate = w_in[..., ff0 + FOX_HEADS:]
    return w_proj.astype(BF16), w_gate.astype(BF16)


def _prep_qk_gains(aq, ak, fq, fk):
    scale = HEAD_DIM ** -0.5
    ones = jnp.ones((aq.shape[0], SWA_KV_HEADS * HEAD_DIM), F32)
    row = jnp.concatenate([jnp.tile(aq * scale, (1, SWA_Q_HEADS)), jnp.tile(ak, (1, SWA_KV_HEADS)),
                           ones, jnp.tile(fq * scale, (1, FOX_HEADS)), jnp.tile(fk, (1, FOX_HEADS))],
                          axis=-1)
    return row[:, None, :]


def kernel(x, rel_bias, ffn1_norm, ffn1_w_gate, ffn1_w_up, ffn1_w_down, mix_norm, w_in, b_forget, conf_dw, conf_dw_b, conf_ln_g, conf_ln_b, conf_w_out, sc_conv, sc_w_out, swa_q_norm, swa_k_norm, swa_sink, swa_w_o, fox_q_norm, fox_k_norm, fox_w_o, w_out, ffn2_norm, ffn2_w_gate, ffn2_w_up, ffn2_w_down):
    b, t, d = x.shape
    depth = w_in.shape[0]

    w_proj, w_gate = _prep_proj(w_in)
    qk_gains = _prep_qk_gains(swa_q_norm, swa_k_norm, fox_q_norm, fox_k_norm)
    bf = _fgate_cols(b_forget)[:, None, :]
    swa_rows = jnp.concatenate([swa_w_o[:, h * HEAD_DIM:(h + 1) * HEAD_DIM] for h in SWA_Q_ORDER],
                               axis=1)
    w_branch = jnp.stack([conf_w_out, sc_w_out, swa_rows, fox_w_o], axis=1).astype(BF16)
    w_out_b = w_out.astype(BF16)
    ffn1 = [w.astype(BF16) for w in (ffn1_w_gate, ffn1_w_up, ffn1_w_down)]
    ffn2 = [w.astype(BF16) for w in (ffn2_w_gate, ffn2_w_up, ffn2_w_down)]

    for l in range(depth):
        x = _ffn(x.reshape(b * t, d), l, ffn1_norm, *ffn1).reshape(b, t, d)
        zc, zs, qa, ka, va, qf, kf, vft = _inproj(x, l, mix_norm, w_proj, qk_gains, bf)
        uc, us = _conv(zc, zs, l, conf_dw, conf_dw_b, conf_ln_g, conf_ln_b, sc_conv)
        oa = _swa(rel_bias, swa_sink[l], qa, ka, va)
        of = _fox(qf, kf, vft)
        x = _merge(x, l, mix_norm, uc, us, oa, of, w_branch, w_gate, w_out_b)
        x = _ffn(x.reshape(b * t, d), l, ffn2_norm, *ffn2).reshape(b, t, d)
    return x
```

```python
import functools
import math

import numpy as np
import jax
import jax.numpy as jnp
from jax import lax
from jax.experimental import pallas as pl
from jax.experimental.pallas import tpu as pltpu

F32 = jnp.float32
BF16 = jnp.bfloat16

D_MODEL = 1024
HEAD_DIM = 64
CONV_CH = 256
CONV_K = 31
SC_CH = 256
SC_K = 3
SWA_Q_HEADS = 4
SWA_KV_HEADS = 2
WINDOW = 128
FOX_HEADS = 4
N_BUCKETS = 32
MAX_DISTANCE = 128
D_FF = 2816
N_BRANCH = 4
EPS = 1e-6
NEG_INF = -1e30
LOG2E = math.log2(math.e)

LANES = 128
VMEM_LIMIT = 56 * 1024 * 1024

COL_CONV = 0
COL_SC = COL_CONV + 2 * CONV_CH
COL_AQ = COL_SC + 3 * SC_CH
COL_AK = COL_AQ + SWA_Q_HEADS * HEAD_DIM
COL_AV = COL_AK + SWA_KV_HEADS * HEAD_DIM
COL_FQ = COL_AV + SWA_KV_HEADS * HEAD_DIM
COL_FK = COL_FQ + FOX_HEADS * HEAD_DIM
COL_FV = COL_FK + FOX_HEADS * HEAD_DIM
COL_FF = COL_FV + FOX_HEADS * HEAD_DIM
FOX_PAIRS = FOX_HEADS // 2
FF_REP = 9
PROJ_COLS = COL_FF + FOX_PAIRS * LANES
SWA_Q_ORDER = (0, 2, 1, 3)

ROW_TILE = 512
CONV_HALO = 32
SC_HALO = 8
CONV_CHUNK = 64
SWA_TILE = 512
FOX_TQ = 2048
FOX_TK = 512
FOX_QC = 256
FOX_MAX_SHIFT = 56.0


def _params(n_axes, flags=None):
    return pltpu.CompilerParams(dimension_semantics=("arbitrary",) * n_axes,
                                vmem_limit_bytes=VMEM_LIMIT, flags=flags)


def _resident(shape, layer=None):
    if layer is None:
        return pl.BlockSpec(shape, lambda *_: (0,) * len(shape), pipeline_mode=pl.Buffered(1))
    return pl.BlockSpec((None,) + tuple(shape), lambda *_: (layer,) + (0,) * len(shape),
                        pipeline_mode=pl.Buffered(1))


def _rms(x):
    return x * lax.rsqrt(jnp.mean(x * x, axis=-1, keepdims=True) + EPS)


def _sigmoid(x):
    return 1.0 / (1.0 + jnp.exp(-x))


def _ffn_kernel(x_ref, g_ref, wg_ref, wu_ref, wd_ref, o_ref):
    x = x_ref[...]
    xn = (_rms(x) * g_ref[...]).astype(BF16)
    gate = jnp.dot(xn, wg_ref[...], preferred_element_type=F32)
    up = jnp.dot(xn, wu_ref[...], preferred_element_type=F32)
    act = (gate * _sigmoid(gate) * up).astype(BF16)
    y = jnp.dot(act, wd_ref[...], preferred_element_type=F32)
    o_ref[...] = x + 0.5 * y


def _ffn(x2d, layer, g, wg, wu, wd):
    n, d = x2d.shape
    row = pl.BlockSpec((ROW_TILE, d), lambda i: (i, 0))
    return pl.pallas_call(
        _ffn_kernel,
        grid=(n // ROW_TILE,),
        in_specs=[row, _resident((1, d), layer), _resident(wg.shape[1:], layer),
                  _resident(wu.shape[1:], layer), _resident(wd.shape[1:], layer)],
        out_specs=row,
        out_shape=jax.ShapeDtypeStruct((n, d), F32),
        compiler_params=_params(1),
        name="ffn",
    )(x2d, g.reshape(g.shape[0], 1, d), wg, wu, wd)


def _split3(v):
    hi = v.astype(BF16).astype(F32)
    r = v - hi
    mid = r.astype(BF16).astype(F32)
    return hi, mid, r - mid


def _pair_rms(zb, first):
    sq = zb * zb
    s0 = jnp.sum(jnp.where(first, sq, 0.0), axis=-1, keepdims=True)
    s1 = jnp.sum(jnp.where(first, 0.0, sq), axis=-1, keepdims=True)
    inv = jnp.where(first, lax.rsqrt(s0 * (1.0 / HEAD_DIM) + EPS),
                    lax.rsqrt(s1 * (1.0 / HEAD_DIM) + EPS))
    return zb * inv


def _inproj_kernel(x_ref, g_ref, w_ref, qkg_ref, bf_ref, shift_ref,
                   zc_ref, zs_ref, qa_ref, ka_ref, va_ref, qf_ref, kf_ref, vft_ref, carry_ref):
    tm = x_ref.shape[1]

    @pl.when(pl.program_id(1) == 0)
    def _():
        carry_ref[...] = jnp.zeros_like(carry_ref)

    h = (_rms(x_ref[0]) * g_ref[...]).astype(BF16)
    z = jnp.dot(h, w_ref[...], preferred_element_type=F32)

    zc_ref[0] = z[:, COL_CONV:COL_SC]
    zs_ref[0] = z[:, COL_SC:COL_AQ]
    va_ref[0] = z[:, COL_AV:COL_FQ].astype(BF16)
    vt = z[:, COL_FV:COL_FF].T
    ones_row = (lax.broadcasted_iota(jnp.int32, (HEAD_DIM, tm), 0) == 0).astype(F32)
    for i in range(FOX_HEADS):
        vft_ref[0, i * LANES:(i + 1) * LANES, :] = jnp.concatenate(
            [vt[i * HEAD_DIM:(i + 1) * HEAD_DIM], ones_row], axis=0).astype(BF16)

    first =lax.broadcasted_iota(jnp.int32, (tm, LANES), 1) < HEAD_DIM

    def normed(col):
        g0 = col - COL_AQ
        return _pair_rms(z[:, col:col + LANES], first) * qkg_ref[:, g0:g0 + LANES]

    for i in range(SWA_Q_HEADS // 2):
        qa_ref[0, :, i * LANES:(i + 1) * LANES] = normed(COL_AQ + i * LANES).astype(BF16)
    ka_ref[0] = normed(COL_AK).astype(BF16)

    ff = z[:, COL_FF:PROJ_COLS] + bf_ref[...]
    logf = jnp.minimum(ff, 0.0) - jnp.log1p(jnp.exp(-jnp.abs(ff)))
    rows = lax.broadcasted_iota(jnp.int32, logf.shape, 0)
    cum = logf
    step = 1
    while step < tm:
        cum = cum + jnp.where(rows >= step, pltpu.roll(cum, step, axis=0), 0.0)
        step *= 2
    cum = cum + carry_ref[...]
    carry_ref[...] = cum[tm - 1:tm, :]
    cum = cum * LOG2E

    lane = lax.broadcasted_iota(jnp.int32, (tm, LANES), 1)
    slot = lane % FF_REP
    used = lane < 2 * FF_REP
    s_hi, s_mid, s_lo = _split3(shift_ref[...])
    for p in range(FOX_PAIRS):
        hi, mid, lo = _split3(cum[:, p * LANES:(p + 1) * LANES])
        q_extra = jnp.where(slot == 0, hi, jnp.where(slot == 1, mid, jnp.where(slot == 2, lo, jnp.where(
            slot < 6, 1.0, jnp.where(slot == 6, -s_hi, jnp.where(slot == 7, -s_mid, -s_lo))))))
        k_extra = jnp.where(slot == 3, -hi, jnp.where(slot == 4, -mid, jnp.where(slot == 5, -lo, 1.0)))
        base = 2 * p * LANES
        qf_ref[0, :, base:base + LANES] = normed(COL_FQ + p * LANES).astype(BF16)
        qf_ref[0, :, base + LANES:base + 2 * LANES] = jnp.where(used, q_extra, 0.0).astype(BF16)
        kf_ref[0, :, base:base + LANES] = normed(COL_FK + p * LANES).astype(BF16)
        kf_ref[0, :, base + LANES:base + 2 * LANES] = jnp.where(used, k_extra, 0.0).astype(BF16)


def _inproj(x, layer, g, w, qkg, bf, shift):
    b, t, d = x.shape
    tm = ROW_TILE

    def rows(width):
        return pl.BlockSpec((1, tm, width), lambda bi, ti: (bi, ti, 0))

    def oshape(width, dtype=BF16):
        return jax.ShapeDtypeStruct((b, t, width), dtype)

    n_v = FOX_HEADS * LANES
    return pl.pallas_call(
        _inproj_kernel,
        grid=(b, t // tm),
        in_specs=[rows(d), _resident((1, d), layer), _resident(w.shape[1:], layer),
                  _resident(qkg.shape[1:], layer), _resident(bf.shape[1:], layer),
                  _resident((1, LANES))],
        out_specs=[rows(2 * CONV_CH), rows(3 * SC_CH),
                   rows(SWA_Q_HEADS * HEAD_DIM), rows(SWA_KV_HEADS * HEAD_DIM),
                   rows(SWA_KV_HEADS * HEAD_DIM),
                   rows(2 * FOX_PAIRS * LANES), rows(2 * FOX_PAIRS * LANES),
                   pl.BlockSpec((1, n_v, tm), lambda bi, ti: (bi, 0, ti))],
        out_shape=[oshape(2 * CONV_CH, F32), oshape(3 * SC_CH, F32),
                   oshape(SWA_Q_HEADS * HEAD_DIM), oshape(SWA_KV_HEADS * HEAD_DIM),
                   oshape(SWA_KV_HEADS * HEAD_DIM),
                   oshape(2 * FOX_PAIRS * LANES), oshape(2 * FOX_PAIRS * LANES),
                   jax.ShapeDtypeStruct((b, n_v, t), BF16)],
        scratch_shapes=[pltpu.VMEM((1, FOX_PAIRS * LANES), F32)],
        compiler_params=_params(2),
        name="inproj",
    )(x, g.reshape(g.shape[0], 1, d), w, qkg, bf, jnp.full((1, LANES), shift, F32))


def _conv_kernel(zc_ref, zch_ref, zs_ref, zsh_ref, dw_ref, dwb_ref, lng_ref, lnb_ref, scw_ref,
                 uc_ref, us_ref, ext_ref, ext2_ref):
    tt = zc_ref.shape[1]
    keep = (pl.program_id(1) > 0).astype(F32)

    def glu(v):
        return v[:, :CONV_CH] * _sigmoid(v[:, CONV_CH:])

    ext_ref[0:CONV_HALO, :] = glu(zch_ref[0]) * keep
    ext_ref[CONV_HALO:, :] = glu(zc_ref[0])
    zsh = zsh_ref[0]
    ext2_ref[0:SC_HALO, :] = zsh[:, SC_CH:2 * SC_CH] * zsh[:, 2 * SC_CH:] * keep
    zs = zs_ref[0]
    ext2_ref[SC_HALO:, :] = zs[:, SC_CH:2 * SC_CH] * zs[:, 2 * SC_CH:]

    for r0 in range(0, tt, CONV_CHUNK):
        acc = jnp.broadcast_to(dwb_ref[...], (CONV_CHUNK, CONV_CH))
        for k in range(CONV_K):
            start = r0 + CONV_HALO - (CONV_K - 1) + k
            acc = acc + dw_ref[k:k + 1, :] * ext_ref[start:start + CONV_CHUNK, :]
        mu = jnp.mean(acc, axis=-1, keepdims=True)
        cen = acc - mu
        var = jnp.mean(cen * cen, axis=-1, keepdims=True)
        y = cen * lax.rsqrt(var + EPS) * lng_ref[...] + lnb_ref[...]
        uc_ref[0, r0:r0 + CONV_CHUNK, :] = (y * _sigmoid(y)).astype(BF16)

        acc2 = jnp.zeros((CONV_CHUNK, SC_CH), F32)
        for k in range(SC_K):
            start = r0 + SC_HALO - (SC_K - 1) + k
            acc2 = acc2 + scw_ref[k:k + 1, :] * ext2_ref[start:start + CONV_CHUNK, :]
        us_ref[0, r0:r0 + CONV_CHUNK, :] = (zs[r0:r0 + CONV_CHUNK, :SC_CH] * acc2).astype(BF16)


def _conv(zc, zs, layer, dw, dwb, lng, lnb, scw):
    b, t, _ = zc.shape
    tt = ROW_TILE

    def rows(width):
        return pl.BlockSpec((1, tt, width), lambda bi, ti: (bi, ti, 0))

    def halo(rows_, width):
        per = tt // rows_
        return pl.BlockSpec((1, rows_, width),
                            lambda bi, ti: (bi, jnp.maximum(ti * per - 1, 0), 0))

    def vec(a):
        return a.reshape(a.shape[0], 1, a.shape[1])

    return pl.pallas_call(
        _conv_kernel,
        grid=(b, t // tt),
        in_specs=[rows(2 * CONV_CH), halo(CONV_HALO, 2 * CONV_CH),
                  rows(3 * SC_CH), halo(SC_HALO, 3 * SC_CH),
                  _resident(dw.shape[1:], layer), _resident((1, CONV_CH), layer),
                  _resident((1, CONV_CH), layer), _resident((1, CONV_CH), layer),
                  _resident(scw.shape[1:], layer)],
        out_specs=[rows(CONV_CH), rows(SC_CH)],
        out_shape=[jax.ShapeDtypeStruct((b, t, CONV_CH), BF16),
                   jax.ShapeDtypeStruct((b, t, SC_CH), BF16)],
        scratch_shapes=[pltpu.VMEM((CONV_HALO + tt, CONV_CH), F32),
                        pltpu.VMEM((SC_HALO + tt, SC_CH), F32)],
        compiler_params=_params(2),
        name="conv",
    )(zc, zc, zs, zs, dw, vec(dwb), vec(lng), vec(lnb), scw)


def _t5_bucket_table():
    max_exact = N_BUCKETS // 2
    dist = np.maximum(np.arange(WINDOW)[:, None] + WINDOW - np.arange(2 * WINDOW)[None, :], 0)
    d = np.maximum(dist, 1).astype(np.float32)
    large = max_exact + (np.log(d / np.float32(max_exact)) / np.float32(
        math.log(MAX_DISTANCE / max_exact)) * np.float32(N_BUCKETS - max_exact)).astype(np.int32)
    large = np.minimum(large, N_BUCKETS - 1)
    return np.where(dist < max_exact, dist, large).astype(np.int32)


def _swa_kernel(rb_ref, sink_ref, bucket_ref, q_ref, k_ref, kh_ref, v_ref, vh_ref, o_ref,
                bias_ref, kk_ref, vv_ref):
    tq = q_ref.shape[1]
    first_step = (pl.program_id(0) == 0) & (pl.program_id(1) == 0)

    @pl.when(first_step)
    def _():
        bucket = bucket_ref[...]
        qi = lax.broadcasted_iota(jnp.int32, bucket.shape, 0) + WINDOW
        ki = lax.broadcasted_iota(jnp.int32, bucket.shape, 1)
        dist = qi - ki
        ok = (dist >= 0) & (dist < WINDOW)
        for h in range(SWA_Q_HEADS):
            bias = jnp.zeros(bucket.shape, F32)
            for bk in range(N_BUCKETS):
                bias = jnp.where(bucket == bk, rb_ref[bk, h], bias)
            bias_ref[h] = jnp.where(ok, bias, NEG_INF)

    kk_ref[0:WINDOW, :] = kh_ref[0]
    kk_ref[WINDOW:, :] = k_ref[0]
    vv_ref[0:WINDOW, :] = vh_ref[0]
    vv_ref[WINDOW:, :] = v_ref[0]
    seq_start = pl.program_id(1) == 0
    prev_half = lax.broadcasted_iota(jnp.int32, (WINDOW, 2 * WINDOW), 1) < WINDOW
    first = lax.broadcasted_iota(jnp.int32, (WINDOW, LANES), 1) < HEAD_DIM

    for sb in range(tq // WINDOW):
        r0 = sb * WINDOW
        kk = kk_ref[r0:r0 + 2 * WINDOW, :]
        vv = vv_ref[r0:r0 + 2 * WINDOW, :]
        for blk in range(SWA_Q_HEADS // 2):
            qb = q_ref[0, r0:r0 + WINDOW, blk * LANES:(blk + 1) * LANES]
            outs = []
            for half in range(2):
                h = SWA_Q_ORDER[2 * blk + half]
                keep = first if half == 0 else jnp.logical_not(first)
                q = jnp.where(keep, qb, jnp.zeros_like(qb))
                s = lax.dot_general(q, kk, (((1,), (1,)), ((), ())), preferred_element_type=F32)
                s = s + bias_ref[h]
                if sb == 0:
                    s = jnp.where(seq_start & prev_half, NEG_INF, s)
                sink = sink_ref[h]
                m = jnp.maximum(jnp.max(s, axis=-1, keepdims=True), sink)
                p = jnp.exp(s - m)
                denom = jnp.sum(p, axis=-1, keepdims=True) + jnp.exp(sink - m)
                outs.append(jnp.dot(p.astype(BF16), vv, preferred_element_type=F32) / denom)
            o_ref[0, r0:r0 + WINDOW, blk * LANES:(blk + 1) * LANES] = jnp.where(
                first, outs[0], outs[1]).astype(BF16)


def _swa(rel_bias, sink, qa, ka, va):
    b, t, _ = qa.shape
    tq = SWA_TILE
    per = tq // WINDOW
    kvw = SWA_KV_HEADS * HEAD_DIM
    smem = pl.BlockSpec(memory_space=pltpu.SMEM)
    cur_q = pl.BlockSpec((1, tq, SWA_Q_HEADS * HEAD_DIM), lambda bi, ti: (bi, ti, 0))
    cur_kv = pl.BlockSpec((1, tq, kvw), lambda bi, ti: (bi, ti, 0))
    halo = pl.BlockSpec((1, WINDOW, kvw), lambda bi, ti: (bi, jnp.maximum(ti * per - 1, 0), 0))
    return pl.pallas_call(
        _swa_kernel,
        grid=(b, t // tq),
        in_specs=[smem, smem, _resident((WINDOW, 2 * WINDOW)),
                  cur_q, cur_kv, halo, cur_kv, halo],
        out_specs=cur_q,
        out_shape=jax.ShapeDtypeStruct(qa.shape, BF16),
        scratch_shapes=[pltpu.VMEM((SWA_Q_HEADS, WINDOW, 2 * WINDOW), F32),
                        pltpu.VMEM((WINDOW + tq, kvw), BF16),
                        pltpu.VMEM((WINDOW + tq, kvw), BF16)],
        compiler_params=_params(2),
        name="swa",
    )(rel_bias, sink, jnp.asarray(_t5_bucket_table()), qa, ka, ka, va, va)


def _fox_kernel(q_ref, k_ref, vt_ref, o_ref, *, online):
    tq = q_ref.shape[1]
    qi = pl.program_id(2)
    qp = q_ref[0]
    lane = lax.broadcasted_iota(jnp.int32, qp.shape, 1)
    zero = jnp.zeros_like(qp)
    q_heads = [
        jnp.where((lane < HEAD_DIM) | ((lane >= LANES) & (lane < LANES + FF_REP)), qp, zero),
        jnp.where(((lane >= HEAD_DIM) & (lane < LANES)) |
                  ((lane >= LANES + FF_REP) & (lane < LANES + 2 * FF_REP)), qp, zero)]

    chains = [(hd, c) for c in range(tq // FOX_QC) for hd in range(2)]

    def step(j, carry, diag):
        k0 = pl.multiple_of(j * FOX_TK, FOX_TK)
        ks = k_ref[0, pl.ds(k0, FOX_TK), :]
        vts = [vt_ref[0, hd * LANES:(hd + 1) * LANES, pl.ds(k0, FOX_TK)] for hd in range(2)]
        key_lo = 0 if diag is None else diag * FOX_TK
        active = [ci for ci, (_, c) in enumerate(chains)
                  if diag is None or key_lo <= (c + 1) * FOX_QC - 1]

        def scores(ci):
            hd, c = chains[ci]
            st = lax.dot_general(ks, q_heads[hd][c * FOX_QC:(c + 1) * FOX_QC],
                                 (((1,), (1,)), ((), ())), preferred_element_type=F32)
            if diag is not None and key_lo + FOX_TK - 1 > c * FOX_QC:
                key = lax.broadcasted_iota(jnp.int32, st.shape, 0) + key_lo
                qry = lax.broadcasted_iota(jnp.int32, st.shape, 1) + c * FOX_QC
                st = jnp.where(key <= qry, st, NEG_INF)
            return st

        def softmax(ci, st):
            m, acc = carry[ci]
            if not online:
                return m, acc, jnp.exp2(st).astype(BF16)
            m_new = jnp.maximum(m, jnp.max(st, axis=0, keepdims=True))
            return m_new, jnp.exp2(m - m_new) * acc, jnp.exp2(st - m_new).astype(BF16)

        def weighted(ci, part):
            m_new, acc, p = part
            return m_new, acc + jnp.dot(vts[chains[ci][0]], p, preferred_element_type=F32)

        out = dict(enumerate(carry))
        st = {active[0]: scores(active[0])}
        part = {}
        for pos, ci in enumerate(active):
            if pos + 1 < len(active):
                st[active[pos + 1]] = scores(active[pos + 1])
            part[ci] = softmax(ci, st.pop(ci))
            if pos >= 1:
                prev = active[pos - 1]
                out[prev] = weighted(prev, part.pop(prev))
        out[active[-1]] = weighted(active[-1], part.pop(active[-1]))
        return tuple(out[ci] for ci in range(len(chains)))

    init = tuple((jnp.full((1, FOX_QC), NEG_INF, F32), jnp.zeros((LANES, FOX_QC), F32))
                 for _ in chains)
    per = tq // FOX_TK
    res = lax.fori_loop(0, qi * per, lambda j, c: step(j, c, None), init)
    for d in range(per):
        res = step(qi * per + d, res, d)
    for c in range(tq // FOX_QC):
        heads = [acc[:HEAD_DIM] / acc[HEAD_DIM:HEAD_DIM + 1] for _, acc in res[2 * c:2 * c + 2]]
        out_t = jnp.concatenate(heads, axis=0)
        o_ref[0, c * FOX_QC:(c + 1) * FOX_QC, :] = out_t.T.astype(BF16)


def _fox(qf, kf, vft, *, online):
    b, t, _ = qf.shape
    assert FOX_TQ % FOX_TK == 0 and FOX_TQ % FOX_QC == 0
    return pl.pallas_call(
        functools.partial(_fox_kernel, online=online),
        grid=(b, FOX_PAIRS, t // FOX_TQ),
        in_specs=[pl.BlockSpec((1, FOX_TQ, 2 * LANES), lambda bi, pi, qi: (bi, qi, pi)),
                  pl.BlockSpec((1, t, 2 * LANES), lambda bi, pi, qi: (bi, 0, pi)),
                  pl.BlockSpec((1, 2 * LANES, t), lambda bi, pi, qi: (bi, pi, 0))],
        out_specs=pl.BlockSpec((1, FOX_TQ, LANES), lambda bi, pi, qi: (bi, qi, pi)),
        out_shape=jax.ShapeDtypeStruct((b, t, FOX_HEADS * HEAD_DIM), BF16),
        compiler_params=_params(3),
        name="fox_online" if online else "fox",
    )(qf, kf, vft)


def _fox_shift(q_gain, k_gain):
    bound = (LOG2E * HEAD_DIM ** 0.5) * jnp.max(jnp.abs(q_gain)) * jnp.max(jnp.abs(k_gain))
    usable = bound < FOX_MAX_SHIFT
    return jnp.where(usable, bound, 0.0), usable


def _merge_kernel(x_ref, g_ref, uc_ref, us_ref, oa_ref, of_ref, wb_ref, wgate_ref, wout_ref, o_ref):
    x = x_ref[0]
    h = (_rms(x) * g_ref[...]).astype(BF16)
    merged = jnp.zeros(x.shape, F32)
    for i, u_ref in enumerate((uc_ref, us_ref, oa_ref, of_ref)):
        p = jnp.dot(u_ref[0], wb_ref[i], preferred_element_type=F32)
        gate = jnp.dot(h, wgate_ref[:, i * D_MODEL:(i + 1) * D_MODEL], preferred_element_type=F32)
        merged = merged + _sigmoid(gate) * p
    o_ref[0] = x + jnp.dot(merged.astype(BF16), wout_ref[...], preferred_element_type=F32)


def _merge(x, layer, g, uc, us, oa, of, wb, wgate, wout):
    b, t, d = x.shape
    tm = ROW_TILE

    def rows(width):
        return pl.BlockSpec((1, tm, width), lambda bi, ti: (bi, ti, 0))

    return pl.pallas_call(
        _merge_kernel,
        grid=(b, t // tm),
        in_specs=[rows(d), _resident((1, d), layer), rows(CONV_CH), rows(SC_CH),
                  rows(SWA_Q_HEADS * HEAD_DIM), rows(FOX_HEADS * HEAD_DIM),
                  _resident(wb.shape[1:], layer), _resident(wgate.shape[1:], layer),
                  _resident(wout.shape[1:], layer)],
        out_specs=rows(d),
        out_shape=jax.ShapeDtypeStruct((b, t, d), F32),
        compiler_params=_params(2),
        name="merge",
    )(x, g.reshape(g.shape[0], 1, d), uc, us, oa, of, wb, wgate, wout)


def _head_cols(w, col0, order):
    return [w[..., col0 + h * HEAD_DIM: col0 + (h + 1) * HEAD_DIM] for h in order]


def _fgate_cols(ff):
    rep = jnp.repeat(ff, FF_REP, axis=-1)
    pad = [(0, 0)] * (ff.ndim - 1) + [(0, LANES - 2 * FF_REP)]
    return jnp.concatenate([jnp.pad(rep[..., p * 2 * FF_REP:(p + 1) * 2 * FF_REP], pad)
                            for p in range(FOX_PAIRS)], axis=-1)


def _prep_proj(w_in):
    aq0 = COL_AQ
    ff0 = COL_FF
    w_proj = jnp.concatenate(
        [w_in[..., :aq0]] + _head_cols(w_in, aq0, SWA_Q_ORDER)
        + [w_in[..., COL_AK:ff0], _fgate_cols(w_in[..., ff0:ff0 + FOX_HEADS])], axis=-1)
    w_gate = w_in[..., ff0 + FOX_HEADS:]
    return w_proj.astype(BF16), w_gate.astype(BF16)


def _prep_qk_gains(aq, ak, fq, fk):
    scale = HEAD_DIM ** -0.5
    ones = jnp.ones((aq.shape[0], SWA_KV_HEADS * HEAD_DIM), F32)
    row = jnp.concatenate([jnp.tile(aq * scale, (1, SWA_Q_HEADS)), jnp.tile(ak, (1, SWA_KV_HEADS)),
                           ones, jnp.tile(fq * (scale * LOG2E), (1, FOX_HEADS)),
                           jnp.tile(fk, (1, FOX_HEADS))],
                          axis=-1)
    return row[:, None, :]


def kernel(x, rel_bias, ffn1_norm, ffn1_w_gate, ffn1_w_up, ffn1_w_down, mix_norm, w_in, b_forget, conf_dw, conf_dw_b, conf_ln_g, conf_ln_b, conf_w_out, sc_conv, sc_w_out, swa_q_norm, swa_k_norm, swa_sink, swa_w_o, fox_q_norm, fox_k_norm, fox_w_o, w_out, ffn2_norm, ffn2_w_gate, ffn2_w_up, ffn2_w_down):
    b, t, d = x.shape
    depth = w_in.shape[0]

    w_proj, w_gate = _prep_proj(w_in)
    qk_gains = _prep_qk_gains(swa_q_norm, swa_k_norm, fox_q_norm, fox_k_norm)
    bf = _fgate_cols(b_forget)[:, None, :]
    swa_rows = jnp.concatenate([swa_w_o[:, h * HEAD_DIM:(h + 1) * HEAD_DIM] for h in SWA_Q_ORDER],
                               axis=1)
    w_branch = jnp.stack([conf_w_out, sc_w_out, swa_rows, fox_w_o], axis=1).astype(BF16)
    w_out_b = w_out.astype(BF16)
    ffn1 = [w.astype(BF16) for w in (ffn1_w_gate, ffn1_w_up, ffn1_w_down)]
    ffn2 = [w.astype(BF16) for w in (ffn2_w_gate, ffn2_w_up, ffn2_w_down)]

    for l in range(depth):
        x = _ffn(x.reshape(b * t, d), l, ffn1_norm, *ffn1).reshape(b, t, d)
        shift, shifted = _fox_shift(fox_q_norm[l], fox_k_norm[l])
        zc, zs, qa, ka, va, qf, kf, vft = _inproj(x, l, mix_norm, w_proj, qk_gains, bf, shift)
        uc, us = _conv(zc, zs, l, conf_dw, conf_dw_b, conf_ln_g, conf_ln_b, sc_conv)
        oa = _swa(rel_bias, swa_sink[l], qa, ka, va)
        of = lax.cond(shifted, functools.partial(_fox, online=False),
                      functools.partial(_fox, online=True), qf, kf, vft)
        x = _merge(x, l, mix_norm, uc, us, oa, of, w_branch, w_gate, w_out_b)
        x = _ffn(x.reshape(b * t, d), l, ffn2_norm, *ffn2).reshape(b, t, d)
    return x
```

```python
import functools
import math

import numpy as np
import jax
import jax.numpy as jnp
from jax import lax
from jax.experimental import pallas as pl
from jax.experimental.pallas import tpu as pltpu

F32 = jnp.float32
BF16 = jnp.bfloat16

D_MODEL = 1024
HEAD_DIM = 64
CONV_CH = 256
CONV_K = 31
SC_CH = 256
SC_K = 3
SWA_Q_HEADS = 4
SWA_KV_HEADS = 2
WINDOW = 128
FOX_HEADS = 4
N_BUCKETS = 32
MAX_DISTANCE = 128
D_FF = 2816
N_BRANCH = 4
EPS = 1e-6
NEG_INF = -1e30
LOG2E = math.log2(math.e)

LANES = 128
SUBLANES = 8
VMEM_LIMIT = 56 * 1024 * 1024

COL_CONV = 0
COL_SC = COL_CONV + 2 * CONV_CH
COL_AQ = COL_SC + 3 * SC_CH
COL_AK = COL_AQ + SWA_Q_HEADS * HEAD_DIM
COL_AV = COL_AK + SWA_KV_HEADS * HEAD_DIM
COL_FQ = COL_AV + SWA_KV_HEADS * HEAD_DIM
COL_FK = COL_FQ + FOX_HEADS * HEAD_DIM
COL_FV = COL_FK + FOX_HEADS * HEAD_DIM
COL_FF = COL_FV + FOX_HEADS * HEAD_DIM
FOX_PAIRS = FOX_HEADS // 2
FF_REP = 9
PROJ_COLS = COL_FF + FOX_PAIRS * LANES
SWA_Q_ORDER = (0, 2, 1, 3)

ROW_TILE = 512
CONV_HALO = 32
SC_HALO = 8
CONV_CHUNK = 64
SWA_TILE = 512
FOX_TQ = 2048
FOX_TK = 512
FOX_QC = 256
FOX_VROWS = 80
FOX_UNROLL = 2
FOX_MAX_SHIFT = 56.0


def _params(n_axes, flags=None):
    return pltpu.CompilerParams(dimension_semantics=("arbitrary",) * n_axes,
                                vmem_limit_bytes=VMEM_LIMIT, flags=flags)


def _resident(shape, layer=None):
    if layer is None:
        return pl.BlockSpec(shape, lambda *_: (0,) * len(shape), pipeline_mode=pl.Buffered(1))
    return pl.BlockSpec((None,) + tuple(shape), lambda *_: (layer,) + (0,) * len(shape),
                        pipeline_mode=pl.Buffered(1))


def _rms(x):
    return x * lax.rsqrt(jnp.mean(x * x, axis=-1, keepdims=True) + EPS)


def _sigmoid(x):
    return 1.0 / (1.0 + jnp.exp(-x))


def _ffn_kernel(x_ref, g_ref, wg_ref, wu_ref, wd_ref, o_ref):
    x = x_ref[...]
    xn = (_rms(x) * g_ref[...]).astype(BF16)
    gate = jnp.dot(xn, wg_ref[...], preferred_element_type=F32)
    up = jnp.dot(xn, wu_ref[...], preferred_element_type=F32)
    act = (gate * _sigmoid(gate) * up).astype(BF16)
    y = jnp.dot(act, wd_ref[...], preferred_element_type=F32)
    o_ref[...] = x + 0.5 * y


def _ffn(x2d, layer, g, wg, wu, wd):
    n, d = x2d.shape
    row = pl.BlockSpec((ROW_TILE, d), lambda i: (i, 0))
    return pl.pallas_call(
        _ffn_kernel,
        grid=(n // ROW_TILE,),
        in_specs=[row, _resident((1, d), layer), _resident(wg.shape[1:], layer),
                  _resident(wu.shape[1:], layer), _resident(wd.shape[1:], layer)],
        out_specs=row,
        out_shape=jax.ShapeDtypeStruct((n, d), F32),
        compiler_params=_params(1),
        name="ffn",
    )(x2d, g.reshape(g.shape[0], 1, d), wg, wu, wd)


def _split3(v):
    hi = v.astype(BF16).astype(F32)
    r = v - hi
    mid = r.astype(BF16).astype(F32)
    return hi, mid, r - mid


def _pair_rms(zb, first):
    sq = zb * zb
    s0 = jnp.sum(jnp.where(first, sq, 0.0), axis=-1, keepdims=True)
    s1 = jnp.sum(jnp.where(first, 0.0, sq), axis=-1, keepdims=True)
    inv = jnp.where(first, lax.rsqrt(s0 * (1.0 / HEAD_DIM) + EPS),
                    lax.rsqrt(s1 * (1.0 / HEAD_DIM) + EPS))
    return zb * inv


def _inproj_kernel(x_ref, g_ref, w_ref, qkg_ref, bf_ref, shift_ref,
                   zc_ref, zs_ref, qa_ref, ka_ref, va_ref, qf_ref, kf_ref, vft_ref, carry_ref):
    tm = x_ref.shape[1]

    @pl.when(pl.program_id(1) == 0)
    def _():
        carry_ref[...] = jnp.zeros_like(carry_ref)

    h = (_rms(x_ref[0]) * g_ref[...]).astype(BF16)
    lane = lax.broadcasted_iota(jnp.int32, (tm, LANES), 1)
    first = lane < HEAD_DIM
    slot = lane % FF_REP
    used = lane < 2 * FF_REP
    ext = {}

    def normed(z, col0, i):
        g0 = col0 - COL_AQ + i * LANES
        return _pair_rms(z[:, i * LANES:(i + 1) * LANES], first) * qkg_ref[:, g0:g0 + LANES]

    def forget_terms(z):
        ff = z + bf_ref[...]
        logf = jnp.minimum(ff, 0.0) - jnp.log1p(jnp.exp(-jnp.abs(ff)))
        rows = lax.broadcasted_iota(jnp.int32, logf.shape, 0)
        cum = logf
        step = 1
        while step < tm:
            cum = cum + jnp.where(rows >= step, pltpu.roll(cum, step, axis=0), 0.0)
            step *= 2
        cum = cum + carry_ref[...]
        carry_ref[...] = cum[tm - 1:tm, :]
        cum = cum * LOG2E
        s_hi, s_mid, s_lo = _split3(shift_ref[...])
        for p in range(FOX_PAIRS):
            hi, mid, lo = _split3(cum[:, p * LANES:(p + 1) * LANES])
            q_extra = jnp.where(slot == 0, hi, jnp.where(slot == 1, mid, jnp.where(
                slot == 2, lo, jnp.where(slot < 6, 1.0, jnp.where(
                    slot == 6, -s_hi, jnp.where(slot == 7, -s_mid, -s_lo))))))
            k_extra = jnp.where(slot == 3, -hi, jnp.where(slot == 4, -mid, jnp.where(
                slot == 5, -lo, 1.0)))
            ext[p] = (jnp.where(used, q_extra, 0.0).astype(BF16),
                      jnp.where(used, k_extra, 0.0).astype(BF16))

    def fox_qk(z):
        for p in range(FOX_PAIRS):
            base = 2 * p * LANES
            qf_ref[0, :, base:base + LANES] = normed(z, COL_FQ, p).astype(BF16)
            qf_ref[0, :, base + LANES:base + 2 * LANES] = ext[p][0]
            kf_ref[0, :, base:base + LANES] = normed(z, COL_FQ, FOX_PAIRS + p).astype(BF16)
            kf_ref[0, :, base + LANES:base + 2 * LANES] = ext[p][1]

    def swa_qk(z):
        n_q = SWA_Q_HEADS // 2
        for i in range(n_q):
            qa_ref[0, :, i * LANES:(i + 1) * LANES] = normed(z, COL_AQ, i).astype(BF16)
        ka_ref[0] = normed(z, COL_AQ, n_q).astype(BF16)

    def fox_v(z):
        vt = z.T
        ones_row = (lax.broadcasted_iota(jnp.int32, (FOX_VROWS - HEAD_DIM, tm), 0) == 0).astype(F32)
        for i in range(FOX_HEADS):
            vft_ref[0, i * FOX_VROWS:(i + 1) * FOX_VROWS, :] = jnp.concatenate(
                [vt[i * HEAD_DIM:(i + 1) * HEAD_DIM], ones_row], axis=0).astype(BF16)

    def swa_v(z):
        va_ref[0] = z.astype(BF16)

    def conv_in(z):
        zc_ref[0] = z

    def sc_in(z):
        zs_ref[0] = z

    z = jnp.dot(h, w_ref[...], preferred_element_type=F32)
    for c0, c1, epilogue in ((COL_FF, PROJ_COLS, forget_terms), (COL_FQ, COL_FV, fox_qk),
                             (COL_AQ, COL_AV, swa_qk), (COL_FV, COL_FF, fox_v),
                             (COL_AV, COL_FQ, swa_v), (COL_CONV, COL_SC, conv_in),
                             (COL_SC, COL_AQ, sc_in)):
        epilogue(z[:, c0:c1])


def _inproj(x, layer, g, w, qkg, bf, shift):
    b, t, d = x.shape
    tm = ROW_TILE

    def rows(width):
        return pl.BlockSpec((1, tm, width), lambda bi, ti: (bi, ti, 0))

    def oshape(width, dtype=BF16):
        return jax.ShapeDtypeStruct((b, t, width), dtype)

    n_v = FOX_HEADS * FOX_VROWS
    return pl.pallas_call(
        _inproj_kernel,
        grid=(b, t // tm),
        in_specs=[rows(d), _resident((1, d), layer), _resident(w.shape[1:], layer),
                  _resident(qkg.shape[1:], layer), _resident(bf.shape[1:], layer),
                  _resident((1, LANES))],
        out_specs=[rows(2 * CONV_CH), rows(3 * SC_CH),
                   rows(SWA_Q_HEADS * HEAD_DIM), rows(SWA_KV_HEADS * HEAD_DIM),
                   rows(SWA_KV_HEADS * HEAD_DIM),
                   rows(2 * FOX_PAIRS * LANES), rows(2 * FOX_PAIRS * LANES),
                   pl.BlockSpec((1, n_v, tm), lambda bi, ti: (bi, 0, ti))],
        out_shape=[oshape(2 * CONV_CH, F32), oshape(3 * SC_CH, F32),
                   oshape(SWA_Q_HEADS * HEAD_DIM), oshape(SWA_KV_HEADS * HEAD_DIM),
                   oshape(SWA_KV_HEADS * HEAD_DIM),
                   oshape(2 * FOX_PAIRS * LANES), oshape(2 * FOX_PAIRS * LANES),
                   jax.ShapeDtypeStruct((b, n_v, t), BF16)],
        scratch_shapes=[pltpu.VMEM((1, FOX_PAIRS * LANES), F32)],
        compiler_params=_params(2),
        name="inproj",
    )(x, g.reshape(g.shape[0], 1, d), w, qkg, bf, jnp.full((1, LANES), shift, F32))


def _fill_shifted(ext_ref, base, n_taps):
    n = ext_ref.shape[1] - SUBLANES
    for r in sorted({(base + k) % SUBLANES for k in range(n_taps)} - {0}):
        ext_ref[r, 0:n, :] = ext_ref[0, r:r + n, :]


def _taps(ext_ref, w_ref, r0, base, n_taps, acc):
    for k in range(n_taps):
        r = (base + k) % SUBLANES
        start = r0 + base + k - r
        acc = acc + w_ref[k:k + 1, :] * ext_ref[r, start:start + CONV_CHUNK, :]
    return acc


def _conv_kernel(zc_ref, zch_ref, zs_ref, zsh_ref, dw_ref, dwb_ref, lng_ref, lnb_ref, scw_ref,
                 uc_ref, us_ref, ext_ref, ext2_ref):
    tt = zc_ref.shape[1]
    keep = (pl.program_id(1) > 0).astype(F32)

    def glu(v):
        return v[:, :CONV_CH] * _sigmoid(v[:, CONV_CH:])

    ext_ref[0, 0:CONV_HALO, :] = glu(zch_ref[0]) * keep
    ext_ref[0, CONV_HALO:, :] = glu(zc_ref[0])
    zsh = zsh_ref[0]
    ext2_ref[0, 0:SC_HALO, :] = zsh[:, SC_CH:2 * SC_CH] * zsh[:, 2 * SC_CH:] * keep
    zs = zs_ref[0]
    ext2_ref[0, SC_HALO:, :] = zs[:, SC_CH:2 * SC_CH] * zs[:, 2 * SC_CH:]
    conv_base = CONV_HALO - (CONV_K - 1)
    sc_base = SC_HALO - (SC_K - 1)
    _fill_shifted(ext_ref, conv_base, CONV_K)
    _fill_shifted(ext2_ref, sc_base, SC_K)

    for r0 in range(0, tt, CONV_CHUNK):
        acc = _taps(ext_ref, dw_ref, r0, conv_base, CONV_K,
                    jnp.broadcast_to(dwb_ref[...], (CONV_CHUNK, CONV_CH)))
        mu = jnp.mean(acc, axis=-1, keepdims=True)
        cen = acc - mu
        var = jnp.mean(cen * cen, axis=-1, keepdims=True)
        y = cen * lax.rsqrt(var + EPS) * lng_ref[...] + lnb_ref[...]
        uc_ref[0, r0:r0 + CONV_CHUNK, :] = (y * _sigmoid(y)).astype(BF16)

        acc2 = _taps(ext2_ref, scw_ref, r0, sc_base, SC_K, jnp.zeros((CONV_CHUNK, SC_CH), F32))
        us_ref[0, r0:r0 + CONV_CHUNK, :] = (zs[r0:r0 + CONV_CHUNK, :SC_CH] * acc2).astype(BF16)


def _conv(zc, zs, layer, dw, dwb, lng, lnb, scw):
    b, t, _ = zc.shape
    tt = ROW_TILE

    def rows(width):
        return pl.BlockSpec((1, tt, width), lambda bi, ti: (bi, ti, 0))

    def halo(rows_, width):
        per = tt // rows_
        return pl.BlockSpec((1, rows_, width),
                            lambda bi, ti: (bi, jnp.maximum(ti * per - 1, 0), 0))

    def vec(a):
        return a.reshape(a.shape[0], 1, a.shape[1])

    return pl.pallas_call(
        _conv_kernel,
        grid=(b, t // tt),
        in_specs=[rows(2 * CONV_CH), halo(CONV_HALO, 2 * CONV_CH),
                  rows(3 * SC_CH), halo(SC_HALO, 3 * SC_CH),
                  _resident(dw.shape[1:], layer), _resident((1, CONV_CH), layer),
                  _resident((1, CONV_CH), layer), _resident((1, CONV_CH), layer),
                  _resident(scw.shape[1:], layer)],
        out_specs=[rows(CONV_CH), rows(SC_CH)],
        out_shape=[jax.ShapeDtypeStruct((b, t, CONV_CH), BF16),
                   jax.ShapeDtypeStruct((b, t, SC_CH), BF16)],
        scratch_shapes=[pltpu.VMEM((SUBLANES, CONV_HALO + tt, CONV_CH), F32),
                        pltpu.VMEM((SUBLANES, SC_HALO + tt, SC_CH), F32)],
        compiler_params=_params(2),
        name="conv",
    )(zc, zc, zs, zs, dw, vec(dwb), vec(lng), vec(lnb), scw)


def _t5_bucket_table():
    max_exact = N_BUCKETS // 2
    dist = np.maximum(np.arange(WINDOW)[:, None] + WINDOW - np.arange(2 * WINDOW)[None, :], 0)
    d = np.maximum(dist, 1).astype(np.float32)
    large = max_exact + (np.log(d / np.float32(max_exact)) / np.float32(
        math.log(MAX_DISTANCE / max_exact)) * np.float32(N_BUCKETS - max_exact)).astype(np.int32)
    large = np.minimum(large, N_BUCKETS - 1)
    return np.where(dist < max_exact, dist, large).astype(np.int32)


def _swa_kernel(rb_ref, sink_ref, bucket_ref, q_ref, k_ref, kh_ref, v_ref, vh_ref, o_ref,
                bias_ref, kk_ref, vv_ref):
    tq = q_ref.shape[1]
    first_step = (pl.program_id(0) == 0) & (pl.program_id(1) == 0)

    @pl.when(first_step)
    def _():
        bucket = bucket_ref[...]
        qi = lax.broadcasted_iota(jnp.int32, bucket.shape, 0) + WINDOW
        ki = lax.broadcasted_iota(jnp.int32, bucket.shape, 1)
        dist = qi - ki
        ok = (dist >= 0) & (dist < WINDOW)
        for h in range(SWA_Q_HEADS):
            bias = jnp.zeros(bucket.shape, F32)
            for bk in range(N_BUCKETS):
                bias = jnp.where(bucket == bk, rb_ref[bk, h], bias)
            bias_ref[h] = jnp.where(ok, bias, NEG_INF)

    kk_ref[0:WINDOW, :] = kh_ref[0]
    kk_ref[WINDOW:, :] = k_ref[0]
    vv_ref[0:WINDOW, :] = vh_ref[0]
    vv_ref[WINDOW:, :] = v_ref[0]
    seq_start = pl.program_id(1) == 0
    prev_half = lax.broadcasted_iota(jnp.int32, (WINDOW, 2 * WINDOW), 1) < WINDOW
    first = lax.broadcasted_iota(jnp.int32, (WINDOW, LANES), 1) < HEAD_DIM

    for sb in range(tq // WINDOW):
        r0 = sb * WINDOW
        kk = kk_ref[r0:r0 + 2 * WINDOW, :]
        vv = vv_ref[r0:r0 + 2 * WINDOW, :]
        for blk in range(SWA_Q_HEADS // 2):
            qb = q_ref[0, r0:r0 + WINDOW, blk * LANES:(blk + 1) * LANES]
            outs = []
            for half in range(2):
                h = SWA_Q_ORDER[2 * blk + half]
                keep = first if half == 0 else jnp.logical_not(first)
                q = jnp.where(keep, qb, jnp.zeros_like(qb))
                s = lax.dot_general(q, kk, (((1,), (1,)), ((), ())), preferred_element_type=F32)
                s = s + bias_ref[h]
                if sb == 0:
                    s = jnp.where(seq_start & prev_half, NEG_INF, s)
                sink = sink_ref[h]
                m = jnp.maximum(jnp.max(s, axis=-1, keepdims=True), sink)
                p = jnp.exp(s - m)
                denom = jnp.sum(p, axis=-1, keepdims=True) + jnp.exp(sink - m)
                outs.append(jnp.dot(p.astype(BF16), vv, preferred_element_type=F32) / denom)
            o_ref[0, r0:r0 + WINDOW, blk * LANES:(blk + 1) * LANES] = jnp.where(
                first, outs[0], outs[1]).astype(BF16)


def _swa(rel_bias, sink, qa, ka, va):
    b, t, _ = qa.shape
    tq = SWA_TILE
    per = tq // WINDOW
    kvw = SWA_KV_HEADS * HEAD_DIM
    smem = pl.BlockSpec(memory_space=pltpu.SMEM)
    cur_q = pl.BlockSpec((1, tq, SWA_Q_HEADS * HEAD_DIM), lambda bi, ti: (bi, ti, 0))
    cur_kv = pl.BlockSpec((1, tq, kvw), lambda bi, ti: (bi, ti, 0))
    halo = pl.BlockSpec((1, WINDOW, kvw), lambda bi, ti: (bi, jnp.maximum(ti * per - 1, 0), 0))
    return pl.pallas_call(
        _swa_kernel,
        grid=(b, t // tq),
        in_specs=[smem, smem, _resident((WINDOW, 2 * WINDOW)),
                  cur_q, cur_kv, halo, cur_kv, halo],
        out_specs=cur_q,
        out_shape=jax.ShapeDtypeStruct(qa.shape, BF16),
        scratch_shapes=[pltpu.VMEM((SWA_Q_HEADS, WINDOW, 2 * WINDOW), F32),
                        pltpu.VMEM((WINDOW + tq, kvw), BF16),
                        pltpu.VMEM((WINDOW + tq, kvw), BF16)],
        compiler_params=_params(2),
        name="swa",
    )(rel_bias, sink, jnp.asarray(_t5_bucket_table()), qa, ka, ka, va, va)


def _fox_kernel(q_ref, k_ref, vt_ref, o_ref, *, online):
    tq = q_ref.shape[1]
    qi = pl.program_id(2)
    qp = q_ref[0]
    lane = lax.broadcasted_iota(jnp.int32, qp.shape, 1)
    zero = jnp.zeros_like(qp)
    q_heads = [
        jnp.where((lane < HEAD_DIM) | ((lane >= LANES) & (lane < LANES + FF_REP)), qp, zero),
        jnp.where(((lane >= HEAD_DIM) & (lane < LANES)) |
                  ((lane >= LANES + FF_REP) & (lane < LANES + 2 * FF_REP)), qp, zero)]

    chains = [(hd, c) for c in range(tq // FOX_QC) for hd in range(2)]

    def run(blocks, carry):
        state = dict(enumerate(carry))
        items = []
        for j, diag in blocks:
            k0 = pl.multiple_of(j * FOX_TK, FOX_TK)
            ks = k_ref[0, pl.ds(k0, FOX_TK), :]
            vts = [vt_ref[0, hd * FOX_VROWS:(hd + 1) * FOX_VROWS, pl.ds(k0, FOX_TK)]
                   for hd in range(2)]
            key_lo = 0 if diag is None else diag * FOX_TK
            for ci, (hd, c) in enumerate(chains):
                if diag is None or key_lo <= (c + 1) * FOX_QC - 1:
                    masked = diag is not None and key_lo + FOX_TK - 1 > c * FOX_QC
                    items.append((ci, ks, vts[hd], key_lo if masked else None))

        def scores(item):
            ci, ks, _, mask_lo = item
            hd, c = chains[ci]
            st = lax.dot_general(ks, q_heads[hd][c * FOX_QC:(c + 1) * FOX_QC],
                                 (((1,), (1,)), ((), ())), preferred_element_type=F32)
            if mask_lo is not None:
                key = lax.broadcasted_iota(jnp.int32, st.shape, 0) + mask_lo
                qry = lax.broadcasted_iota(jnp.int32, st.shape, 1) + c * FOX_QC
                st = jnp.where(key <= qry, st, NEG_INF)
            return st

        def softmax(item, st):
            m, acc = state[item[0]]
            if not online:
                return m, acc, jnp.exp2(st).astype(BF16)
            m_new = jnp.maximum(m, jnp.max(st, axis=0, keepdims=True))
            return m_new, jnp.exp2(m - m_new) * acc, jnp.exp2(st - m_new).astype(BF16)

        def weighted(item, part):
            m_new, acc, p = part
            state[item[0]] = (m_new, acc + jnp.dot(item[2], p, preferred_element_type=F32))

        st_next = scores(items[0])
        part_prev = None
        for pos, item in enumerate(items):
            st_cur = st_next
            if pos + 1 < len(items):
                st_next = scores(items[pos + 1])
            part = softmax(item, st_cur)
            if part_prev is not None:
                weighted(items[pos - 1], part_prev)
            part_prev = part
        weighted(items[-1], part_prev)
        return tuple(state[ci] for ci in range(len(chains)))

    init = tuple((jnp.full((1, FOX_QC), NEG_INF, F32), jnp.zeros((FOX_VROWS, FOX_QC), F32))
                 for _ in chains)
    per = tq // FOX_TK
    assert per % FOX_UNROLL == 0
    res = lax.fori_loop(
        0, qi * (per // FOX_UNROLL),
        lambda it, c: run([(it * FOX_UNROLL + u, None) for u in range(FOX_UNROLL)], c), init)
    res = run([(qi * per + d, d) for d in range(per)], res)
    for c in range(tq // FOX_QC):
        heads = [acc[:HEAD_DIM] / acc[HEAD_DIM:HEAD_DIM + 1] for _, acc in res[2 * c:2 * c + 2]]
        out_t = jnp.concatenate(heads, axis=0)
        o_ref[0, c * FOX_QC:(c + 1) * FOX_QC, :] = out_t.T.astype(BF16)


def _fox(qf, kf, vft, *, online):
    b, t, _ = qf.shape
    assert FOX_TQ % FOX_TK == 0 and FOX_TQ % FOX_QC == 0
    return pl.pallas_call(
        functools.partial(_fox_kernel, online=online),
        grid=(b, FOX_PAIRS, t // FOX_TQ),
        in_specs=[pl.BlockSpec((1, FOX_TQ, 2 * LANES), lambda bi, pi, qi: (bi, qi, pi)),
                  pl.BlockSpec((1, t, 2 * LANES), lambda bi, pi, qi: (bi, 0, pi)),
                  pl.BlockSpec((1, 2 * FOX_VROWS, t), lambda bi, pi, qi: (bi, pi, 0))],
        out_specs=pl.BlockSpec((1, FOX_TQ, LANES), lambda bi, pi, qi: (bi, qi, pi)),
        out_shape=jax.ShapeDtypeStruct((b, t, FOX_HEADS * HEAD_DIM), BF16),
        compiler_params=_params(3),
        name="fox_online" if online else "fox",
    )(qf, kf, vft)


def _fox_shift(q_gain, k_gain):
    bound = (LOG2E * HEAD_DIM ** 0.5) * jnp.max(jnp.abs(q_gain)) * jnp.max(jnp.abs(k_gain))
    usable = bound < FOX_MAX_SHIFT
    return jnp.where(usable, bound, 0.0), usable


def _merge_kernel(x_ref, g_ref, uc_ref, us_ref, oa_ref, of_ref, wb_ref, wgate_ref, wout_ref, o_ref):
    x = x_ref[0]
    h = (_rms(x) * g_ref[...]).astype(BF16)
    merged = jnp.zeros(x.shape, F32)
    for i, u_ref in enumerate((uc_ref, us_ref, oa_ref, of_ref)):
        p = jnp.dot(u_ref[0], wb_ref[i], preferred_element_type=F32)
        gate = jnp.dot(h, wgate_ref[:, i * D_MODEL:(i + 1) * D_MODEL], preferred_element_type=F32)
        merged = merged + _sigmoid(gate) * p
    o_ref[0] = x + jnp.dot(merged.astype(BF16), wout_ref[...], preferred_element_type=F32)


def _merge(x, layer, g, uc, us, oa, of, wb, wgate, wout):
    b, t, d = x.shape
    tm = ROW_TILE

    def rows(width):
        return pl.BlockSpec((1, tm, width), lambda bi, ti: (bi, ti, 0))

    return pl.pallas_call(
        _merge_kernel,
        grid=(b, t // tm),
        in_specs=[rows(d), _resident((1, d), layer), rows(CONV_CH), rows(SC_CH),
                  rows(SWA_Q_HEADS * HEAD_DIM), rows(FOX_HEADS * HEAD_DIM),
                  _resident(wb.shape[1:], layer), _resident(wgate.shape[1:], layer),
                  _resident(wout.shape[1:], layer)],
        out_specs=rows(d),
        out_shape=jax.ShapeDtypeStruct((b, t, d), F32),
        compiler_params=_params(2),
        name="merge",
    )(x, g.reshape(g.shape[0], 1, d), uc, us, oa, of, wb, wgate, wout)


def _head_cols(w, col0, order):
    return [w[..., col0 + h * HEAD_DIM: col0 + (h + 1) * HEAD_DIM] for h in order]


def _fgate_cols(ff):
    rep = jnp.repeat(ff, FF_REP, axis=-1)
    pad = [(0, 0)] * (ff.ndim - 1) + [(0, LANES - 2 * FF_REP)]
    return jnp.concatenate([jnp.pad(rep[..., p * 2 * FF_REP:(p + 1) * 2 * FF_REP], pad)
                            for p in range(FOX_PAIRS)], axis=-1)


def _prep_proj(w_in):
    aq0 = COL_AQ
    ff0 = COL_FF
    w_proj = jnp.concatenate(
        [w_in[..., :aq0]] + _head_cols(w_in, aq0, SWA_Q_ORDER)
        + [w_in[..., COL_AK:ff0], _fgate_cols(w_in[..., ff0:ff0 + FOX_HEADS])], axis=-1)
    w_gate = w_in[..., ff0 + FOX_HEADS:]
    return w_proj.astype(BF16), w_gate.astype(BF16)


def _prep_qk_gains(aq, ak, fq, fk):
    scale = HEAD_DIM ** -0.5
    ones = jnp.ones((aq.shape[0], SWA_KV_HEADS * HEAD_DIM), F32)
    row = jnp.concatenate([jnp.tile(aq * scale, (1, SWA_Q_HEADS)), jnp.tile(ak, (1, SWA_KV_HEADS)),
                           ones, jnp.tile(fq * (scale * LOG2E), (1, FOX_HEADS)),
                           jnp.tile(fk, (1, FOX_HEADS))],
                          axis=-1)
    return row[:, None, :]


def kernel(x, rel_bias, ffn1_norm, ffn1_w_gate, ffn1_w_up, ffn1_w_down, mix_norm, w_in, b_forget, conf_dw, conf_dw_b, conf_ln_g, conf_ln_b, conf_w_out, sc_conv, sc_w_out, swa_q_norm, swa_k_norm, swa_sink, swa_w_o, fox_q_norm, fox_k_norm, fox_w_o, w_out, ffn2_norm, ffn2_w_gate, ffn2_w_up, ffn2_w_down):
    b, t, d = x.shape
    depth = w_in.shape[0]

    w_proj, w_gate = _prep_proj(w_in)
    qk_gains = _prep_qk_gains(swa_q_norm, swa_k_norm, fox_q_norm, fox_k_norm)
    bf = _fgate_cols(b_forget)[:, None, :]
    swa_rows = jnp.concatenate([swa_w_o[:, h * HEAD_DIM:(h + 1) * HEAD_DIM] for h in SWA_Q_ORDER],
                               axis=1)
    w_branch = jnp.stack([conf_w_out, sc_w_out, swa_rows, fox_w_o], axis=1).astype(BF16)
    w_out_b = w_out.astype(BF16)
    ffn1 = [w.astype(BF16) for w in (ffn1_w_gate, ffn1_w_up, ffn1_w_down)]
    ffn2 = [w.astype(BF16) for w in (ffn2_w_gate, ffn2_w_up, ffn2_w_down)]

    for l in range(depth):
        x = _ffn(x.reshape(b * t, d), l, ffn1_norm, *ffn1).reshape(b, t, d)
        shift, shifted = _fox_shift(fox_q_norm[l], fox_k_norm[l])
        zc, zs, qa, ka, va, qf, kf, vft = _inproj(x, l, mix_norm, w_proj, qk_gains, bf, shift)
        uc, us = _conv(zc, zs, l, conf_dw, conf_dw_b, conf_ln_g, conf_ln_b, sc_conv)
        oa = _swa(rel_bias, swa_sink[l], qa, ka, va)
        of = lax.cond(shifted, functools.partial(_fox, online=False),
                      functools.partial(_fox, online=True), qf, kf, vft)
        x = _merge(x, l, mix_norm, uc, us, oa, of, w_branch, w_gate, w_out_b)
        x = _ffn(x.reshape(b * t, d), l, ffn2_norm, *ffn2).reshape(b, t, d)
    return x
```

```python
import functools
import math

import numpy as np
import jax
import jax.numpy as jnp
from jax import lax
from jax.experimental import pallas as pl
from jax.experimental.pallas import tpu as pltpu

F32 = jnp.float32
BF16 = jnp.bfloat16

D_MODEL = 1024
HEAD_DIM = 64
CONV_CH = 256
CONV_K = 31
SC_CH = 256
SC_K = 3
SWA_Q_HEADS = 4
SWA_KV_HEADS = 2
WINDOW = 128
FOX_HEADS = 4
N_BUCKETS = 32
MAX_DISTANCE = 128
D_FF = 2816
N_BRANCH = 4
EPS = 1e-6
NEG_INF = -1e30
LOG2E = math.log2(math.e)

LANES = 128
SUBLANES = 8
VMEM_LIMIT = 56 * 1024 * 1024

COL_CONV = 0
COL_SC = COL_CONV + 2 * CONV_CH
COL_AQ = COL_SC + 3 * SC_CH
COL_AK = COL_AQ + SWA_Q_HEADS * HEAD_DIM
COL_AV = COL_AK + SWA_KV_HEADS * HEAD_DIM
COL_FQ = COL_AV + SWA_KV_HEADS * HEAD_DIM
COL_FK = COL_FQ + FOX_HEADS * HEAD_DIM
COL_FV = COL_FK + FOX_HEADS * HEAD_DIM
COL_FF = COL_FV + FOX_HEADS * HEAD_DIM
FOX_PAIRS = FOX_HEADS // 2
FF_REP = 9
PROJ_COLS = COL_FF + FOX_PAIRS * LANES
SWA_Q_ORDER = (0, 2, 1, 3)

ROW_TILE = 512
CONV_HALO = 32
SC_HALO = 8
CONV_CHUNK = 64
SWA_TILE = 512
FOX_TQ = 2048
FOX_TK = 512
FOX_QC = 256
FOX_VROWS = 80
FOX_UNROLL = 4
SWA_MAX_SPREAD = 100.0
FOX_MAX_SHIFT = 56.0


def _params(n_axes, flags=None):
    return pltpu.CompilerParams(dimension_semantics=("arbitrary",) * n_axes,
                                vmem_limit_bytes=VMEM_LIMIT, flags=flags)


def _resident(shape, layer=None):
    if layer is None:
        return pl.BlockSpec(shape, lambda *_: (0,) * len(shape), pipeline_mode=pl.Buffered(1))
    return pl.BlockSpec((None,) + tuple(shape), lambda *_: (layer,) + (0,) * len(shape),
                        pipeline_mode=pl.Buffered(1))


def _rms(x):
    return x * lax.rsqrt(jnp.mean(x * x, axis=-1, keepdims=True) + EPS)


def _sigmoid(x):
    return 1.0 / (1.0 + jnp.exp(-x))


def _ffn_kernel(x_ref, g_ref, wg_ref, wu_ref, wd_ref, o_ref):
    x = x_ref[...]
    xn = (_rms(x) * g_ref[...]).astype(BF16)
    gate = jnp.dot(xn, wg_ref[...], preferred_element_type=F32)
    up = jnp.dot(xn, wu_ref[...], preferred_element_type=F32)
    act = (gate * _sigmoid(gate) * up).astype(BF16)
    y = jnp.dot(act, wd_ref[...], preferred_element_type=F32)
    o_ref[...] = x + 0.5 * y


def _ffn(x2d, layer, g, wg, wu, wd):
    n, d = x2d.shape
    row = pl.BlockSpec((ROW_TILE, d), lambda i: (i, 0))
    return pl.pallas_call(
        _ffn_kernel,
        grid=(n // ROW_TILE,),
        in_specs=[row, _resident((1, d), layer), _resident(wg.shape[1:], layer),
                  _resident(wu.shape[1:], layer), _resident(wd.shape[1:], layer)],
        out_specs=row,
        out_shape=jax.ShapeDtypeStruct((n, d), F32),
        compiler_params=_params(1),
        name="ffn",
    )(x2d, g.reshape(g.shape[0], 1, d), wg, wu, wd)


def _split3(v):
    hi = v.astype(BF16).astype(F32)
    r = v - hi
    mid = r.astype(BF16).astype(F32)
    return hi, mid, r - mid


def _pair_rms(zb, first):
    sq = zb * zb
    s0 = jnp.sum(jnp.where(first, sq, 0.0), axis=-1, keepdims=True)
    s1 = jnp.sum(jnp.where(first, 0.0, sq), axis=-1, keepdims=True)
    inv = jnp.where(first, lax.rsqrt(s0 * (1.0 / HEAD_DIM) + EPS),
                    lax.rsqrt(s1 * (1.0 / HEAD_DIM) + EPS))
    return zb * inv


def _inproj_kernel(x_ref, g_ref, w_ref, qkg_ref, bf_ref, shift_ref,
                   zc_ref, zs_ref, qa_ref, ka_ref, va_ref, qf_ref, kf_ref, vft_ref, carry_ref):
    tm = x_ref.shape[1]

    @pl.when(pl.program_id(1) == 0)
    def _():
        carry_ref[...] = jnp.zeros_like(carry_ref)

    h = (_rms(x_ref[0]) * g_ref[...]).astype(BF16)
    lane = lax.broadcasted_iota(jnp.int32, (tm, LANES), 1)
    first = lane < HEAD_DIM
    slot = lane % FF_REP
    used = lane < 2 * FF_REP
    ext = {}

    def normed(z, col0, i):
        g0 = col0 - COL_AQ + i * LANES
        return _pair_rms(z[:, i * LANES:(i + 1) * LANES], first) * qkg_ref[:, g0:g0 + LANES]

    def forget_terms(z):
        ff = z + bf_ref[...]
        logf = jnp.minimum(ff, 0.0) - jnp.log1p(jnp.exp(-jnp.abs(ff)))
        rows = lax.broadcasted_iota(jnp.int32, logf.shape, 0)
        cum = logf
        step = 1
        while step < tm:
            cum = cum + jnp.where(rows >= step, pltpu.roll(cum, step, axis=0), 0.0)
            step *= 2
        cum = cum + carry_ref[...]
        carry_ref[...] = cum[tm - 1:tm, :]
        cum = cum * LOG2E
        s_hi, s_mid, s_lo = _split3(shift_ref[...])
        for p in range(FOX_PAIRS):
            hi, mid, lo = _split3(cum[:, p * LANES:(p + 1) * LANES])
            q_extra = jnp.where(slot == 0, hi, jnp.where(slot == 1, mid, jnp.where(
                slot == 2, lo, jnp.where(slot < 6, 1.0, jnp.where(
                    slot == 6, -s_hi, jnp.where(slot == 7, -s_mid, -s_lo))))))
            k_extra = jnp.where(slot == 3, -hi, jnp.where(slot == 4, -mid, jnp.where(
                slot == 5, -lo, 1.0)))
            ext[p] = (jnp.where(used, q_extra, 0.0).astype(BF16),
                      jnp.where(used, k_extra, 0.0).astype(BF16))

    def fox_qk(z):
        for p in range(FOX_PAIRS):
            base = 2 * p * LANES
            qf_ref[0, :, base:base + LANES] = normed(z, COL_FQ, p).astype(BF16)
            qf_ref[0, :, base + LANES:base + 2 * LANES] = ext[p][0]
            kf_ref[0, :, base:base + LANES] = normed(z, COL_FQ, FOX_PAIRS + p).astype(BF16)
            kf_ref[0, :, base + LANES:base + 2 * LANES] = ext[p][1]

    def swa_qk(z):
        n_q = SWA_Q_HEADS // 2
        for i in range(n_q):
            qa_ref[0, :, i * LANES:(i + 1) * LANES] = normed(z, COL_AQ, i).astype(BF16)
        ka_ref[0] = normed(z, COL_AQ, n_q).astype(BF16)

    def fox_v(z):
        vt = z.T
        ones_row = (lax.broadcasted_iota(jnp.int32, (FOX_VROWS - HEAD_DIM, tm), 0) == 0).astype(F32)
        for i in range(FOX_HEADS):
            vft_ref[0, i * FOX_VROWS:(i + 1) * FOX_VROWS, :] = jnp.concatenate(
                [vt[i * HEAD_DIM:(i + 1) * HEAD_DIM], ones_row], axis=0).astype(BF16)

    def swa_v(z):
        va_ref[0, :, 0:LANES] = z.astype(BF16)
        va_ref[0, :, LANES:2 * LANES] = (lane == 0).astype(BF16)

    def conv_in(z):
        zc_ref[0] = z

    def sc_in(z):
        zs_ref[0] = z

    z = lax.dot_general(h, w_ref[...], (((1,), (1,)), ((), ())),
                        preferred_element_type=F32)
    for c0, c1, epilogue in ((COL_FF, PROJ_COLS, forget_terms), (COL_FQ, COL_FV, fox_qk),
                             (COL_AQ, COL_AV, swa_qk), (COL_FV, COL_FF, fox_v),
                             (COL_AV, COL_FQ, swa_v), (COL_CONV, COL_SC, conv_in),
                             (COL_SC, COL_AQ, sc_in)):
        epilogue(z[:, c0:c1])


def _inproj(x, layer, g, w, qkg, bf, shift):
    b, t, d = x.shape
    tm = ROW_TILE

    def rows(width):
        return pl.BlockSpec((1, tm, width), lambda bi, ti: (bi, ti, 0))

    def oshape(width, dtype=BF16):
        return jax.ShapeDtypeStruct((b, t, width), dtype)

    n_v = FOX_HEADS * FOX_VROWS
    return pl.pallas_call(
        _inproj_kernel,
        grid=(b, t // tm),
        in_specs=[rows(d), _resident((1, d), layer), _resident(w.shape[1:], layer),
                  _resident(qkg.shape[1:], layer), _resident(bf.shape[1:], layer),
                  _resident((1, LANES))],
        out_specs=[rows(2 * CONV_CH), rows(3 * SC_CH),
                   rows(SWA_Q_HEADS * HEAD_DIM), rows(SWA_KV_HEADS * HEAD_DIM),
                   rows(2 * LANES),
                   rows(2 * FOX_PAIRS * LANES), rows(2 * FOX_PAIRS * LANES),
                   pl.BlockSpec((1, n_v, tm), lambda bi, ti: (bi, 0, ti))],
        out_shape=[oshape(2 * CONV_CH, F32), oshape(3 * SC_CH, F32),
                   oshape(SWA_Q_HEADS * HEAD_DIM), oshape(SWA_KV_HEADS * HEAD_DIM),
                   oshape(2 * LANES),
                   oshape(2 * FOX_PAIRS * LANES), oshape(2 * FOX_PAIRS * LANES),
                   jax.ShapeDtypeStruct((b, n_v, t), BF16)],
        scratch_shapes=[pltpu.VMEM((1, FOX_PAIRS * LANES), F32)],
        compiler_params=_params(2),
        name="inproj",
    )(x, g.reshape(g.shape[0], 1, d), w, qkg, bf, jnp.full((1, LANES), shift, F32))


def _fill_shifted(ext_ref, base, n_taps):
    n = ext_ref.shape[1] - SUBLANES
    for r in sorted({(base + k) % SUBLANES for k in range(n_taps)} - {0}):
        ext_ref[r, 0:n, :] = ext_ref[0, r:r + n, :]


def _taps(ext_ref, w_ref, r0, base, n_taps, acc):
    groups = CONV_CHUNK // SUBLANES
    acc = acc.reshape(groups, SUBLANES, acc.shape[-1])
    for k in range(n_taps):
        r = (base + k) % SUBLANES
        start = r0 + base + k - r
        u = ext_ref[r, start:start + CONV_CHUNK, :]
        acc = acc + w_ref[k][None] * u.reshape(groups, SUBLANES, u.shape[-1])
    return acc.reshape(CONV_CHUNK, acc.shape[-1])


def _conv_kernel(zc_ref, zch_ref, zs_ref, zsh_ref, dw_ref, dwb_ref, lng_ref, lnb_ref, scw_ref,
                 uc_ref, us_ref, ext_ref, ext2_ref):
    tt = zc_ref.shape[1]
    keep = (pl.program_id(1) > 0).astype(F32)

    def glu(v):
        return v[:, :CONV_CH] * _sigmoid(v[:, CONV_CH:])

    ext_ref[0, 0:CONV_HALO, :] = glu(zch_ref[0]) * keep
    ext_ref[0, CONV_HALO:, :] = glu(zc_ref[0])
    zsh = zsh_ref[0]
    ext2_ref[0, 0:SC_HALO, :] = zsh[:, SC_CH:2 * SC_CH] * zsh[:, 2 * SC_CH:] * keep
    zs = zs_ref[0]
    ext2_ref[0, SC_HALO:, :] = zs[:, SC_CH:2 * SC_CH] * zs[:, 2 * SC_CH:]
    conv_base = CONV_HALO - (CONV_K - 1)
    sc_base = SC_HALO - (SC_K - 1)
    _fill_shifted(ext_ref, conv_base, CONV_K)
    _fill_shifted(ext2_ref, sc_base, SC_K)

    for r0 in range(0, tt, CONV_CHUNK):
        acc = _taps(ext_ref, dw_ref, r0, conv_base, CONV_K,
                    jnp.broadcast_to(dwb_ref[...], (CONV_CHUNK, CONV_CH)))
        mu = jnp.mean(acc, axis=-1, keepdims=True)
        cen = acc - mu
        var = jnp.mean(cen * cen, axis=-1, keepdims=True)
        y = cen * lax.rsqrt(var + EPS) * lng_ref[...] + lnb_ref[...]
        uc_ref[0, r0:r0 + CONV_CHUNK, :] = (y * _sigmoid(y)).astype(BF16)

        acc2 = _taps(ext2_ref, scw_ref, r0, sc_base, SC_K, jnp.zeros((CONV_CHUNK, SC_CH), F32))
        us_ref[0, r0:r0 + CONV_CHUNK, :] = (zs[r0:r0 + CONV_CHUNK, :SC_CH] * acc2).astype(BF16)


def _conv(zc, zs, layer, dw, dwb, lng, lnb, scw):
    b, t, _ = zc.shape
    tt = ROW_TILE

    def rows(width):
        return pl.BlockSpec((1, tt, width), lambda bi, ti: (bi, ti, 0))

    def halo(rows_, width):
        per = tt // rows_
        return pl.BlockSpec((1, rows_, width),
                            lambda bi, ti: (bi, jnp.maximum(ti * per - 1, 0), 0))

    def vec(a):
        return a.reshape(a.shape[0], 1, a.shape[1])

    def taps(w):
        return jnp.broadcast_to(w[:, :, None, :], w.shape[:2] + (SUBLANES, w.shape[2]))

    return pl.pallas_call(
        _conv_kernel,
        grid=(b, t // tt),
        in_specs=[rows(2 * CONV_CH), halo(CONV_HALO, 2 * CONV_CH),
                  rows(3 * SC_CH), halo(SC_HALO, 3 * SC_CH),
                  _resident((CONV_K, SUBLANES, CONV_CH), layer), _resident((1, CONV_CH), layer),
                  _resident((1, CONV_CH), layer), _resident((1, CONV_CH), layer),
                  _resident((SC_K, SUBLANES, SC_CH), layer)],
        out_specs=[rows(CONV_CH), rows(SC_CH)],
        out_shape=[jax.ShapeDtypeStruct((b, t, CONV_CH), BF16),
                   jax.ShapeDtypeStruct((b, t, SC_CH), BF16)],
        scratch_shapes=[pltpu.VMEM((SUBLANES, CONV_HALO + tt, CONV_CH), F32),
                        pltpu.VMEM((SUBLANES, SC_HALO + tt, SC_CH), F32)],
        compiler_params=_params(2),
        name="conv",
    )(zc, zc, zs, zs, taps(dw), vec(dwb), vec(lng), vec(lnb), taps(scw))


def _t5_bucket_table():
    max_exact = N_BUCKETS // 2
    dist = np.maximum(np.arange(WINDOW)[:, None] + WINDOW - np.arange(2 * WINDOW)[None, :], 0)
    d = np.maximum(dist, 1).astype(np.float32)
    large = max_exact + (np.log(d / np.float32(max_exact)) / np.float32(
        math.log(MAX_DISTANCE / max_exact)) * np.float32(N_BUCKETS - max_exact)).astype(np.int32)
    large = np.minimum(large, N_BUCKETS - 1)
    return np.where(dist < max_exact, dist, large).astype(np.int32)


def _swa_kernel(rb_ref, sink_ref, shift_ref, bucket_ref, q_ref, k_ref, kh_ref, v_ref, vh_ref, o_ref,
                bias_ref, kk_ref, vv_ref, *, online):
    tq = q_ref.shape[1]
    first_step = (pl.program_id(0) == 0) & (pl.program_id(1) == 0)

    @pl.when(first_step)
    def _():
        bucket = bucket_ref[...]
        qi = lax.broadcasted_iota(jnp.int32, bucket.shape, 0) + WINDOW
        ki = lax.broadcasted_iota(jnp.int32, bucket.shape, 1)
        dist = qi - ki
        ok = (dist >= 0) & (dist < WINDOW)
        for h in range(SWA_Q_HEADS):
            bias = jnp.zeros(bucket.shape, F32)
            for bk in range(N_BUCKETS):
                bias = jnp.where(bucket == bk, rb_ref[bk, h], bias)
            bias_ref[h] = jnp.where(ok, bias - shift_ref[h], NEG_INF)

    kk_ref[0:WINDOW, :] = kh_ref[0]
    kk_ref[WINDOW:, :] = k_ref[0]
    vv_ref[0:WINDOW, :] = vh_ref[0]
    vv_ref[WINDOW:, :] = v_ref[0]
    seq_start = pl.program_id(1) == 0
    prev_half = lax.broadcasted_iota(jnp.int32, (WINDOW, 2 * WINDOW), 1) < WINDOW
    first = lax.broadcasted_iota(jnp.int32, (WINDOW, LANES), 1) < HEAD_DIM

    for sb in range(tq // WINDOW):
        r0 = sb * WINDOW
        kk = kk_ref[r0:r0 + 2 * WINDOW, :]
        vv = vv_ref[r0:r0 + 2 * WINDOW, :]
        for blk in range(SWA_Q_HEADS // 2):
            qb = q_ref[0, r0:r0 + WINDOW, blk * LANES:(blk + 1) * LANES]
            outs = []
            for half in range(2):
                h = SWA_Q_ORDER[2 * blk + half]
                keep = first if half == 0 else jnp.logical_not(first)
                q = jnp.where(keep, qb, jnp.zeros_like(qb))
                s = lax.dot_general(q, kk, (((1,), (1,)), ((), ())), preferred_element_type=F32)
                s = s + bias_ref[h]
                if sb == 0:
                    s = jnp.where(seq_start & prev_half, NEG_INF, s)
                sink = jnp.full((1, 1), sink_ref[h] - shift_ref[h], F32)
                if online:
                    m = jnp.maximum(jnp.max(s, axis=-1, keepdims=True), sink)
                    s, sink = s - m, sink - m
                pv = jnp.dot(jnp.exp2(s).astype(BF16), vv, preferred_element_type=F32)
                outs.append(pv[:, :LANES] / (pv[:, LANES:LANES + 1] + jnp.exp2(sink)))
            o_ref[0, r0:r0 + WINDOW, blk * LANES:(blk + 1) * LANES] = jnp.where(
                first, outs[0], outs[1]).astype(BF16)


def _swa(rel_bias2, sink2, shift, qa, ka, va, *, online):
    b, t, _ = qa.shape
    tq = SWA_TILE
    per = tq // WINDOW
    kw, vw = ka.shape[-1], va.shape[-1]
    smem = pl.BlockSpec(memory_space=pltpu.SMEM)

    def cur(width):
        return pl.BlockSpec((1, tq, width), lambda bi, ti: (bi, ti, 0))

    def halo(width):
        return pl.BlockSpec((1, WINDOW, width),
                            lambda bi, ti: (bi, jnp.maximum(ti * per - 1, 0), 0))

    return pl.pallas_call(
        functools.partial(_swa_kernel, online=online),
        grid=(b, t // tq),
        in_specs=[smem, smem, smem, _resident((WINDOW, 2 * WINDOW)),
                  cur(qa.shape[-1]), cur(kw), halo(kw), cur(vw), halo(vw)],
        out_specs=cur(qa.shape[-1]),
        out_shape=jax.ShapeDtypeStruct(qa.shape, BF16),
        scratch_shapes=[pltpu.VMEM((SWA_Q_HEADS, WINDOW, 2 * WINDOW), F32),
                        pltpu.VMEM((WINDOW + tq, kw), BF16),
                        pltpu.VMEM((WINDOW + tq, vw), BF16)],
        compiler_params=_params(2),
        name="swa_online" if online else "swa",
    )(rel_bias2, sink2, shift, jnp.asarray(_t5_bucket_table()), qa, ka, ka, va, va)


def _swa_shift(q_gain, k_gain, rel_bias, sink):
    qk = HEAD_DIM ** 0.5 * jnp.max(jnp.abs(q_gain)) * jnp.max(jnp.abs(k_gain))
    hi = jnp.maximum(qk + jnp.max(rel_bias, axis=0), sink)
    lo = jnp.maximum(rel_bias[0] - qk, sink)
    usable = jnp.all((hi - lo) * LOG2E < SWA_MAX_SPREAD)
    return jnp.where(usable, hi, 0.0), usable


def _fox_kernel(q_ref, k_ref, vt_ref, o_ref, *, online):
    tq = q_ref.shape[1]
    qi = pl.program_id(2)
    qp = q_ref[0]
    lane = lax.broadcasted_iota(jnp.int32, qp.shape, 1)
    zero = jnp.zeros_like(qp)
    q_heads = [
        jnp.where((lane < HEAD_DIM) | ((lane >= LANES) & (lane < LANES + FF_REP)), qp, zero),
        jnp.where(((lane >= HEAD_DIM) & (lane < LANES)) |
                  ((lane >= LANES + FF_REP) & (lane < LANES + 2 * FF_REP)), qp, zero)]

    chains = [(hd, c) for c in range(tq // FOX_QC) for hd in range(2)]

    def run(blocks, carry):
        state = dict(enumerate(carry))
        items = []
        for j, diag in blocks:
            k0 = pl.multiple_of(j * FOX_TK, FOX_TK)
            ks = k_ref[0, pl.ds(k0, FOX_TK), :]
            vts = [vt_ref[0, hd * FOX_VROWS:(hd + 1) * FOX_VROWS, pl.ds(k0, FOX_TK)]
                   for hd in range(2)]
            key_lo = 0 if diag is None else diag * FOX_TK
            for ci, (hd, c) in enumerate(chains):
                if diag is None or key_lo <= (c + 1) * FOX_QC - 1:
                    masked = diag is not None and key_lo + FOX_TK - 1 > c * FOX_QC
                    items.append((ci, ks, vts[hd], key_lo if masked else None))

        def scores(item):
            ci, ks, _, mask_lo = item
            hd, c = chains[ci]
            st = lax.dot_general(ks, q_heads[hd][c * FOX_QC:(c + 1) * FOX_QC],
                                 (((1,), (1,)), ((), ())), preferred_element_type=F32)
            if mask_lo is not None:
                key = lax.broadcasted_iota(jnp.int32, st.shape, 0) + mask_lo
                qry = lax.broadcasted_iota(jnp.int32, st.shape, 1) + c * FOX_QC
                st = jnp.where(key <= qry, st, NEG_INF)
            return st

        def softmax(item, st):
            m, acc = state[item[0]]
            if not online:
                return m, acc, jnp.exp2(st).astype(BF16)
            m_new = jnp.maximum(m, jnp.max(st, axis=0, keepdims=True))
            return m_new, jnp.exp2(m - m_new) * acc, jnp.exp2(st - m_new).astype(BF16)

        def weighted(item, part):
            m_new, acc, p = part
            state[item[0]] = (m_new, acc + jnp.dot(item[2], p, preferred_element_type=F32))

        st_next = scores(items[0])
        part_prev = None
        for pos, item in enumerate(items):
            st_cur = st_next
            if pos + 1 < len(items):
                st_next = scores(items[pos + 1])
            part = softmax(item, st_cur)
            if part_prev is not None:
                weighted(items[pos - 1], part_prev)
            part_prev = part
        weighted(items[-1], part_prev)
        return tuple(state[ci] for ci in range(len(chains)))

    init = tuple((jnp.full((1, FOX_QC), NEG_INF, F32), jnp.zeros((FOX_VROWS, FOX_QC), F32))
                 for _ in chains)
    per = tq // FOX_TK
    assert per % FOX_UNROLL == 0
    res = lax.fori_loop(
        0, qi * (per // FOX_UNROLL),
        lambda it, c: run([(it * FOX_UNROLL + u, None) for u in range(FOX_UNROLL)], c), init)
    res = run([(qi * per + d, d) for d in range(per)], res)
    for c in range(tq // FOX_QC):
        heads = [acc[:HEAD_DIM] / acc[HEAD_DIM:HEAD_DIM + 1] for _, acc in res[2 * c:2 * c + 2]]
        out_t = jnp.concatenate(heads, axis=0)
        o_ref[0, c * FOX_QC:(c + 1) * FOX_QC, :] = out_t.T.astype(BF16)


def _fox(qf, kf, vft, *, online):
    b, t, _ = qf.shape
    assert FOX_TQ % FOX_TK == 0 and FOX_TQ % FOX_QC == 0
    return pl.pallas_call(
        functools.partial(_fox_kernel, online=online),
        grid=(b, FOX_PAIRS, t // FOX_TQ),
        in_specs=[pl.BlockSpec((1, FOX_TQ, 2 * LANES), lambda bi, pi, qi: (bi, qi, pi)),
                  pl.BlockSpec((1, t, 2 * LANES), lambda bi, pi, qi: (bi, 0, pi)),
                  pl.BlockSpec((1, 2 * FOX_VROWS, t), lambda bi, pi, qi: (bi, pi, 0))],
        out_specs=pl.BlockSpec((1, FOX_TQ, LANES), lambda bi, pi, qi: (bi, qi, pi)),
        out_shape=jax.ShapeDtypeStruct((b, t, FOX_HEADS * HEAD_DIM), BF16),
        compiler_params=_params(3),
        name="fox_online" if online else "fox",
    )(qf, kf, vft)


def _fox_shift(q_gain, k_gain):
    bound = (LOG2E * HEAD_DIM ** 0.5) * jnp.max(jnp.abs(q_gain)) * jnp.max(jnp.abs(k_gain))
    usable = bound < FOX_MAX_SHIFT
    return jnp.where(usable, bound, 0.0), usable


def _merge_kernel(x_ref, g_ref, uc_ref, us_ref, oa_ref, of_ref, wb_ref, wgate_ref, wout_ref, o_ref):
    x = x_ref[0]
    h = (_rms(x) * g_ref[...]).astype(BF16)
    merged = jnp.zeros(x.shape, F32)
    for i, u_ref in enumerate((uc_ref, us_ref, oa_ref, of_ref)):
        p = jnp.dot(u_ref[0], wb_ref[i], preferred_element_type=F32)
        gate = lax.dot_general(h, wgate_ref[i * D_MODEL:(i + 1) * D_MODEL, :],
                               (((1,), (1,)), ((), ())), preferred_element_type=F32)
        merged = merged + _sigmoid(gate) * p
    o_ref[0] = x + jnp.dot(merged.astype(BF16), wout_ref[...], preferred_element_type=F32)


def _merge(x, layer, g, uc, us, oa, of, wb, wgate, wout):
    b, t, d = x.shape
    tm = ROW_TILE

    def rows(width):
        return pl.BlockSpec((1, tm, width), lambda bi, ti: (bi, ti, 0))

    return pl.pallas_call(
        _merge_kernel,
        grid=(b, t // tm),
        in_specs=[rows(d), _resident((1, d), layer), rows(CONV_CH), rows(SC_CH),
                  rows(SWA_Q_HEADS * HEAD_DIM), rows(FOX_HEADS * HEAD_DIM),
                  _resident(wb.shape[1:], layer), _resident(wgate.shape[1:], layer),
                  _resident(wout.shape[1:], layer)],
        out_specs=rows(d),
        out_shape=jax.ShapeDtypeStruct((b, t, d), F32),
        compiler_params=_params(2),
        name="merge",
    )(x, g.reshape(g.shape[0], 1, d), uc, us, oa, of, wb, wgate, wout)


def _fgate_cols(ff, axis=-1):
    rep = jnp.repeat(ff, FF_REP, axis=axis)
    pad = [(0, 0)] * ff.ndim
    pad[axis] = (0, LANES - 2 * FF_REP)
    pairs = [lax.slice_in_dim(rep, p * 2 * FF_REP, (p + 1) * 2 * FF_REP, axis=axis)
             for p in range(FOX_PAIRS)]
    return jnp.concatenate([jnp.pad(x, pad) for x in pairs], axis=axis)


def _prep_proj(w_in):
    wt = jnp.transpose(w_in, (2, 0, 1))
    aq0 = COL_AQ
    ff0 = COL_FF
    heads = [wt[aq0 + h * HEAD_DIM: aq0 + (h + 1) * HEAD_DIM] for h in SWA_Q_ORDER]
    w_proj = jnp.concatenate([wt[:aq0]] + heads + [wt[COL_AK:ff0],
                                                   _fgate_cols(wt[ff0:ff0 + FOX_HEADS], axis=0)], axis=0)
    w_gate = wt[ff0 + FOX_HEADS:]
    return (jnp.transpose(w_proj, (1, 0, 2)).astype(BF16),
            jnp.transpose(w_gate, (1, 0, 2)).astype(BF16))


def _prep_qk_gains(aq, ak, fq, fk):
    scale = HEAD_DIM ** -0.5 * LOG2E
    ones = jnp.ones((aq.shape[0], SWA_KV_HEADS * HEAD_DIM), F32)
    row = jnp.concatenate([jnp.tile(aq * scale, (1, SWA_Q_HEADS)), jnp.tile(ak, (1, SWA_KV_HEADS)),
                           ones, jnp.tile(fq * scale, (1, FOX_HEADS)),
                           jnp.tile(fk, (1, FOX_HEADS))],
                          axis=-1)
    return row[:, None, :]


def kernel(x, rel_bias, ffn1_norm, ffn1_w_gate, ffn1_w_up, ffn1_w_down, mix_norm, w_in, b_forget, conf_dw, conf_dw_b, conf_ln_g, conf_ln_b, conf_w_out, sc_conv, sc_w_out, swa_q_norm, swa_k_norm, swa_sink, swa_w_o, fox_q_norm, fox_k_norm, fox_w_o, w_out, ffn2_norm, ffn2_w_gate, ffn2_w_up, ffn2_w_down):
    b, t, d = x.shape
    depth = w_in.shape[0]

    w_proj, w_gate = _prep_proj(w_in)
    qk_gains = _prep_qk_gains(swa_q_norm, swa_k_norm, fox_q_norm, fox_k_norm)
    bf = _fgate_cols(b_forget)[:, None, :]
    swa_rows = jnp.concatenate([swa_w_o[:, h * HEAD_DIM:(h + 1) * HEAD_DIM] for h in SWA_Q_ORDER],
                               axis=1)
    w_branch = jnp.stack([conf_w_out, sc_w_out, swa_rows, fox_w_o], axis=1).astype(BF16)
    w_out_b = w_out.astype(BF16)
    ffn1 = [w.astype(BF16) for w in (ffn1_w_gate, ffn1_w_up, ffn1_w_down)]
    ffn2 = [w.astype(BF16) for w in (ffn2_w_gate, ffn2_w_up, ffn2_w_down)]

    for l in range(depth):
        x = _ffn(x.reshape(b * t, d), l, ffn1_norm, *ffn1).reshape(b, t, d)
        shift, shifted = _fox_shift(fox_q_norm[l], fox_k_norm[l])
        zc, zs, qa, ka, va, qf, kf, vft = _inproj(x, l, mix_norm, w_proj, qk_gains, bf, shift)
        uc, us = _conv(zc, zs, l, conf_dw, conf_dw_b, conf_ln_g, conf_ln_b, sc_conv)
        a_shift, a_shifted = _swa_shift(swa_q_norm[l], swa_k_norm[l], rel_bias, swa_sink[l])
        oa = lax.cond(a_shifted, functools.partial(_swa, online=False),
                      functools.partial(_swa, online=True),
                      rel_bias * LOG2E, swa_sink[l] * LOG2E, a_shift * LOG2E, qa, ka, va)
        of = lax.cond(shifted, functools.partial(_fox, online=False),
                      functools.partial(_fox, online=True), qf, kf, vft)
        x = _merge(x, l, mix_norm, uc, us, oa, of, w_branch, w_gate, w_out_b)
        x = _ffn(x.reshape(b * t, d), l, ffn2_norm, *ffn2).reshape(b, t, d)
    return x
```

```python
import functools
import math

import numpy as np
import jax
import jax.numpy as jnp
from jax import lax
from jax.experimental import pallas as pl
from jax.experimental.pallas import tpu as pltpu

F32 = jnp.float32
BF16 = jnp.bfloat16

D_MODEL = 1024
HEAD_DIM = 64
CONV_CH = 256
CONV_K = 31
SC_CH = 256
SC_K = 3
SWA_Q_HEADS = 4
SWA_KV_HEADS = 2
WINDOW = 128
FOX_HEADS = 4
N_BUCKETS = 32
MAX_DISTANCE = 128
D_FF = 2816
N_BRANCH = 4
EPS = 1e-6
NEG_INF = -1e30
LOG2E = math.log2(math.e)

LANES = 128
SUBLANES = 8
VMEM_LIMIT = 56 * 1024 * 1024

COL_CONV = 0
COL_SC = COL_CONV + 2 * CONV_CH
COL_AQ = COL_SC + 3 * SC_CH
COL_AK = COL_AQ + SWA_Q_HEADS * HEAD_DIM
COL_AV = COL_AK + SWA_KV_HEADS * HEAD_DIM
COL_FQ = COL_AV + SWA_KV_HEADS * HEAD_DIM
COL_FK = COL_FQ + FOX_HEADS * HEAD_DIM
COL_FV = COL_FK + FOX_HEADS * HEAD_DIM
COL_FF = COL_FV + FOX_HEADS * HEAD_DIM
FOX_PAIRS = FOX_HEADS // 2
FF_REP = 9
PROJ_COLS = COL_FF + FOX_PAIRS * LANES
SWA_Q_ORDER = (0, 2, 1, 3)

ROW_TILE = 512
CONV_HALO = 32
SC_HALO = 8
CONV_CHUNK = 64
SWA_TILE = 512
FOX_TQ = 2048
FOX_TK = 512
FOX_QC = 256
FOX_VROWS = 80
FOX_UNROLL = 4
SWA_MAX_SPREAD = 100.0
FOX_MAX_SHIFT = 56.0


def _params(n_axes, flags=None):
    return pltpu.CompilerParams(dimension_semantics=("arbitrary",) * n_axes,
                                vmem_limit_bytes=VMEM_LIMIT, flags=flags)


def _resident(shape, layer=None):
    if layer is None:
        return pl.BlockSpec(shape, lambda *_: (0,) * len(shape), pipeline_mode=pl.Buffered(1))
    return pl.BlockSpec((None,) + tuple(shape), lambda *_: (layer,) + (0,) * len(shape),
                        pipeline_mode=pl.Buffered(1))


def _sigmoid(x):
    return 1.0 / (1.0 + jnp.exp(-x))


def _ffn_kernel(x_ref, wg_ref, wu_ref, wd_ref, o_ref):
    x = x_ref[...]
    inv = lax.rsqrt(jnp.mean(x * x, axis=-1, keepdims=True) + EPS)
    xb = x.astype(BF16)
    half_gate = jnp.dot(xb, wg_ref[...], preferred_element_type=F32) * (0.5 * inv)
    up = jnp.dot(xb, wu_ref[...], preferred_element_type=F32)
    act = ((half_gate + half_gate * jnp.tanh(half_gate)) * up).astype(BF16)
    y = jnp.dot(act, wd_ref[...], preferred_element_type=F32)
    o_ref[...] = x + (0.5 * inv) * y


def _ffn(x2d, layer, wg, wu, wd):
    n, d = x2d.shape
    row = pl.BlockSpec((ROW_TILE, d), lambda i: (i, 0))
    return pl.pallas_call(
        _ffn_kernel,
        grid=(n // ROW_TILE,),
        in_specs=[row, _resident(wg.shape[1:], layer), _resident(wu.shape[1:], layer),
                  _resident(wd.shape[1:], layer)],
        out_specs=row,
        out_shape=jax.ShapeDtypeStruct((n, d), F32),
        compiler_params=_params(1),
        name="ffn",
    )(x2d, wg, wu, wd)


def _split3(v):
    hi = v.astype(BF16).astype(F32)
    r = v - hi
    mid = r.astype(BF16).astype(F32)
    return hi, mid, r - mid


def _pair_rms(zb, first, eps):
    sq = zb * zb
    s0 = jnp.sum(jnp.where(first, sq, 0.0), axis=-1, keepdims=True)
    s1 = jnp.sum(jnp.where(first, 0.0, sq), axis=-1, keepdims=True)
    inv = jnp.where(first, lax.rsqrt(s0 * (1.0 / HEAD_DIM) + eps),
                    lax.rsqrt(s1 * (1.0 / HEAD_DIM) + eps))
    return zb * inv


def _inproj_kernel(x_ref, w_ref, qkg_ref, bf_ref, shift_ref,
                   zc_ref, zs_ref, qa_ref, ka_ref, va_ref, qf_ref, kf_ref, vft_ref, carry_ref):
    tm = x_ref.shape[1]

    @pl.when(pl.program_id(1) == 0)
    def _():
        carry_ref[...] = jnp.zeros_like(carry_ref)

    x = x_ref[0]
    ms = jnp.mean(x * x, axis=-1, keepdims=True) + EPS
    inv = lax.rsqrt(ms)
    qk_eps = EPS * ms
    h = x.astype(BF16)
    lane = lax.broadcasted_iota(jnp.int32, (tm, LANES), 1)
    first = lane < HEAD_DIM
    slot = lane % FF_REP
    used = lane < 2 * FF_REP
    ext = {}

    def normed(z, col0, i):
        g0 = col0 - COL_AQ + i * LANES
        return _pair_rms(z[:, i * LANES:(i + 1) * LANES], first, qk_eps) * qkg_ref[:, g0:g0 + LANES]

    def forget_terms(z):
        ff = z * inv + bf_ref[...]
        logf = jnp.minimum(ff, 0.0) - jnp.log1p(jnp.exp(-jnp.abs(ff)))
        rows = lax.broadcasted_iota(jnp.int32, logf.shape, 0)
        cum = logf
        step = 1
        while step < tm:
            cum = cum + jnp.where(rows >= step, pltpu.roll(cum, step, axis=0), 0.0)
            step *= 2
        cum = cum + carry_ref[...]
        carry_ref[...] = cum[tm - 1:tm, :]
        cum = cum * LOG2E
        s_hi, s_mid, s_lo = _split3(shift_ref[...])
        for p in range(FOX_PAIRS):
            hi, mid, lo = _split3(cum[:, p * LANES:(p + 1) * LANES])
            q_extra = jnp.where(slot == 0, hi, jnp.where(slot == 1, mid, jnp.where(
                slot == 2, lo, jnp.where(slot < 6, 1.0, jnp.where(
                    slot == 6, -s_hi, jnp.where(slot == 7, -s_mid, -s_lo))))))
            k_extra = jnp.where(slot == 3, -hi, jnp.where(slot == 4, -mid, jnp.where(
                slot == 5, -lo, 1.0)))
            ext[p] = (jnp.where(used, q_extra, 0.0).astype(BF16),
                      jnp.where(used, k_extra, 0.0).astype(BF16))

    def fox_qk(z):
        for p in range(FOX_PAIRS):
            base = 2 * p * LANES
            qf_ref[0, :, base:base + LANES] = normed(z, COL_FQ, p).astype(BF16)
            qf_ref[0, :, base + LANES:base + 2 * LANES] = ext[p][0]
            kf_ref[0, :, base:base + LANES] = normed(z, COL_FQ, FOX_PAIRS + p).astype(BF16)
            kf_ref[0, :, base + LANES:base + 2 * LANES] = ext[p][1]

    def swa_qk(z):
        n_q = SWA_Q_HEADS // 2
        for i in range(n_q):
            qa_ref[0, :, i * LANES:(i + 1) * LANES] = normed(z, COL_AQ, i).astype(BF16)
        ka_ref[0] = normed(z, COL_AQ, n_q).astype(BF16)

    def fox_v(z):
        vt = (z * inv).T
        ones_row = (lax.broadcasted_iota(jnp.int32, (FOX_VROWS - HEAD_DIM, tm), 0) == 0).astype(F32)
        for i in range(FOX_HEADS):
            vft_ref[0, i * FOX_VROWS:(i + 1) * FOX_VROWS, :] = jnp.concatenate(
                [vt[i * HEAD_DIM:(i + 1) * HEAD_DIM], ones_row], axis=0).astype(BF16)

    def swa_v(z):
        va_ref[0, :, 0:LANES] = (z * inv).astype(BF16)
        va_ref[0, :, LANES:2 * LANES] = (lane == 0).astype(BF16)

    def conv_in(z):
        zc_ref[0] = z * inv

    def sc_in(z):
        zs_ref[0] = z * inv

    z = lax.dot_general(h, w_ref[...], (((1,), (1,)), ((), ())),
                        preferred_element_type=F32)
    for c0, c1, epilogue in ((COL_FF, PROJ_COLS, forget_terms), (COL_FQ, COL_FV, fox_qk),
                             (COL_AQ, COL_AV, swa_qk), (COL_FV, COL_FF, fox_v),
                             (COL_AV, COL_FQ, swa_v), (COL_CONV, COL_SC, conv_in),
                             (COL_SC, COL_AQ, sc_in)):
        epilogue(z[:, c0:c1])


def _inproj(x, layer, w, qkg, bf, shift):
    b, t, d = x.shape
    tm = ROW_TILE

    def rows(width):
        return pl.BlockSpec((1, tm, width), lambda bi, ti: (bi, ti, 0))

    def oshape(width, dtype=BF16):
        return jax.ShapeDtypeStruct((b, t, width), dtype)

    n_v = FOX_HEADS * FOX_VROWS
    return pl.pallas_call(
        _inproj_kernel,
        grid=(b, t // tm),
        in_specs=[rows(d), _resident(w.shape[1:], layer),
                  _resident(qkg.shape[1:], layer), _resident(bf.shape[1:], layer),
                  _resident((1, LANES))],
        out_specs=[rows(2 * CONV_CH), rows(3 * SC_CH),
                   rows(SWA_Q_HEADS * HEAD_DIM), rows(SWA_KV_HEADS * HEAD_DIM),
                   rows(2 * LANES),
                   rows(2 * FOX_PAIRS * LANES), rows(2 * FOX_PAIRS * LANES),
                   pl.BlockSpec((1, n_v, tm), lambda bi, ti: (bi, 0, ti))],
        out_shape=[oshape(2 * CONV_CH, F32), oshape(3 * SC_CH, F32),
                   oshape(SWA_Q_HEADS * HEAD_DIM), oshape(SWA_KV_HEADS * HEAD_DIM),
                   oshape(2 * LANES),
                   oshape(2 * FOX_PAIRS * LANES), oshape(2 * FOX_PAIRS * LANES),
                   jax.ShapeDtypeStruct((b, n_v, t), BF16)],
        scratch_shapes=[pltpu.VMEM((1, FOX_PAIRS * LANES), F32)],
        compiler_params=_params(2),
        name="inproj",
    )(x, w, qkg, bf, jnp.full((1, LANES), shift, F32))


def _fill_shifted(ext_ref, base, n_taps):
    n = ext_ref.shape[1] - SUBLANES
    for r in sorted({(base + k) % SUBLANES for k in range(n_taps)} - {0}):
        ext_ref[r, 0:n, :] = ext_ref[0, r:r + n, :]


def _taps(ext_ref, w_ref, r0, base, n_taps, acc):
    groups = CONV_CHUNK // SUBLANES
    acc = acc.reshape(groups, SUBLANES, acc.shape[-1])
    for k in range(n_taps):
        r = (base + k) % SUBLANES
        start = r0 + base + k - r
        u = ext_ref[r, start:start + CONV_CHUNK, :]
        acc = acc + w_ref[k][None] * u.reshape(groups, SUBLANES, u.shape[-1])
    return acc.reshape(CONV_CHUNK, acc.shape[-1])


def _conv_kernel(zc_ref, zch_ref, zs_ref, zsh_ref, dw_ref, dwb_ref, lng_ref, lnb_ref, scw_ref,
                 uc_ref, us_ref, ext_ref, ext2_ref):
    tt = zc_ref.shape[1]
    keep = (pl.program_id(1) > 0).astype(F32)

    def glu(v):
        return v[:, :CONV_CH] * _sigmoid(v[:, CONV_CH:])

    ext_ref[0, 0:CONV_HALO, :] = glu(zch_ref[0]) * keep
    ext_ref[0, CONV_HALO:, :] = glu(zc_ref[0])
    zsh = zsh_ref[0]
    ext2_ref[0, 0:SC_HALO, :] = zsh[:, SC_CH:2 * SC_CH] * zsh[:, 2 * SC_CH:] * keep
    zs = zs_ref[0]
    ext2_ref[0, SC_HALO:, :] = zs[:, SC_CH:2 * SC_CH] * zs[:, 2 * SC_CH:]
    conv_base = CONV_HALO - (CONV_K - 1)
    sc_base = SC_HALO - (SC_K - 1)
    _fill_shifted(ext_ref, conv_base, CONV_K)
    _fill_shifted(ext2_ref, sc_base, SC_K)

    for r0 in range(0, tt, CONV_CHUNK):
        acc = _taps(ext_ref, dw_ref, r0, conv_base, CONV_K,
                    jnp.broadcast_to(dwb_ref[...], (CONV_CHUNK, CONV_CH)))
        mu = jnp.mean(acc, axis=-1, keepdims=True)
        cen = acc - mu
        var = jnp.mean(cen * cen, axis=-1, keepdims=True)
        y = cen * lax.rsqrt(var + EPS) * lng_ref[...] + lnb_ref[...]
        uc_ref[0, r0:r0 + CONV_CHUNK, :] = (y * _sigmoid(y)).astype(BF16)

        acc2 = _taps(ext2_ref, scw_ref, r0, sc_base, SC_K, jnp.zeros((CONV_CHUNK, SC_CH), F32))
        us_ref[0, r0:r0 + CONV_CHUNK, :] = (zs[r0:r0 + CONV_CHUNK, :SC_CH] * acc2).astype(BF16)


def _conv(zc, zs, layer, dw, dwb, lng, lnb, scw):
    b, t, _ = zc.shape
    tt = ROW_TILE

    def rows(width):
        return pl.BlockSpec((1, tt, width), lambda bi, ti: (bi, ti, 0))

    def halo(rows_, width):
        per = tt // rows_
        return pl.BlockSpec((1, rows_, width),
                            lambda bi, ti: (bi, jnp.maximum(ti * per - 1, 0), 0))

    def vec(a):
        return a.reshape(a.shape[0], 1, a.shape[1])

    def taps(w):
        return jnp.broadcast_to(w[:, :, None, :], w.shape[:2] + (SUBLANES, w.shape[2]))

    return pl.pallas_call(
        _conv_kernel,
        grid=(b, t // tt),
        in_specs=[rows(2 * CONV_CH), halo(CONV_HALO, 2 * CONV_CH),
                  rows(3 * SC_CH), halo(SC_HALO, 3 * SC_CH),
                  _resident((CONV_K, SUBLANES, CONV_CH), layer), _resident((1, CONV_CH), layer),
                  _resident((1, CONV_CH), layer), _resident((1, CONV_CH), layer),
                  _resident((SC_K, SUBLANES, SC_CH), layer)],
        out_specs=[rows(CONV_CH), rows(SC_CH)],
        out_shape=[jax.ShapeDtypeStruct((b, t, CONV_CH), BF16),
                   jax.ShapeDtypeStruct((b, t, SC_CH), BF16)],
        scratch_shapes=[pltpu.VMEM((SUBLANES, CONV_HALO + tt, CONV_CH), F32),
                        pltpu.VMEM((SUBLANES, SC_HALO + tt, SC_CH), F32)],
        compiler_params=_params(2),
        name="conv",
    )(zc, zc, zs, zs, taps(dw), vec(dwb), vec(lng), vec(lnb), taps(scw))


def _t5_bucket_table():
    max_exact = N_BUCKETS // 2
    dist = np.maximum(np.arange(WINDOW)[:, None] + WINDOW - np.arange(2 * WINDOW)[None, :], 0)
    d = np.maximum(dist, 1).astype(np.float32)
    large = max_exact + (np.log(d / np.float32(max_exact)) / np.float32(
        math.log(MAX_DISTANCE / max_exact)) * np.float32(N_BUCKETS - max_exact)).astype(np.int32)
    large = np.minimum(large, N_BUCKETS - 1)
    return np.where(dist < max_exact, dist, large).astype(np.int32)


def _swa_kernel(rb_ref, sink_ref, shift_ref, bucket_ref, q_ref, k_ref, kh_ref, v_ref, vh_ref, o_ref,
                bias_ref, kk_ref, vv_ref, *, online):
    tq = q_ref.shape[1]
    first_step = (pl.program_id(0) == 0) & (pl.program_id(1) == 0)

    @pl.when(first_step)
    def _():
        bucket = bucket_ref[...]
        qi = lax.broadcasted_iota(jnp.int32, bucket.shape, 0) + WINDOW
        ki = lax.broadcasted_iota(jnp.int32, bucket.shape, 1)
        dist = qi - ki
        ok = (dist >= 0) & (dist < WINDOW)
        for h in range(SWA_Q_HEADS):
            bias = jnp.zeros(bucket.shape, F32)
            for bk in range(N_BUCKETS):
                bias = jnp.where(bucket == bk, rb_ref[bk, h], bias)
            bias_ref[h] = jnp.where(ok, bias - shift_ref[h], NEG_INF)

    kk_ref[0:WINDOW, :] = kh_ref[0]
    kk_ref[WINDOW:, :] = k_ref[0]
    vv_ref[0:WINDOW, :] = vh_ref[0]
    vv_ref[WINDOW:, :] = v_ref[0]
    seq_start = pl.program_id(1) == 0
    prev_half = lax.broadcasted_iota(jnp.int32, (WINDOW, 2 * WINDOW), 1) < WINDOW
    first = lax.broadcasted_iota(jnp.int32, (WINDOW, LANES), 1) < HEAD_DIM

    for sb in range(tq // WINDOW):
        r0 = sb * WINDOW
        kk = kk_ref[r0:r0 + 2 * WINDOW, :]
        vv = vv_ref[r0:r0 + 2 * WINDOW, :]
        for blk in range(SWA_Q_HEADS // 2):
            qb = q_ref[0, r0:r0 + WINDOW, blk * LANES:(blk + 1) * LANES]
            outs = []
            for half in range(2):
                h = SWA_Q_ORDER[2 * blk + half]
                keep = first if half == 0 else jnp.logical_not(first)
                q = jnp.where(keep, qb, jnp.zeros_like(qb))
                s = lax.dot_general(q, kk, (((1,), (1,)), ((), ())), preferred_element_type=F32)
                s = s + bias_ref[h]
                if sb == 0:
                    s = jnp.where(seq_start & prev_half, NEG_INF, s)
                sink = jnp.full((1, 1), sink_ref[h] - shift_ref[h], F32)
                if online:
                    m = jnp.maximum(jnp.max(s, axis=-1, keepdims=True), sink)
                    s, sink = s - m, sink - m
                pv = jnp.dot(jnp.exp2(s).astype(BF16), vv, preferred_element_type=F32)
                outs.append(pv[:, :LANES] / (pv[:, LANES:LANES + 1] + jnp.exp2(sink)))
            o_ref[0, r0:r0 + WINDOW, blk * LANES:(blk + 1) * LANES] = jnp.where(
                first, outs[0], outs[1]).astype(BF16)


def _swa(rel_bias2, sink2, shift, qa, ka, va, *, online):
    b, t, _ = qa.shape
    tq = SWA_TILE
    per = tq // WINDOW
    kw, vw = ka.shape[-1], va.shape[-1]
    smem = pl.BlockSpec(memory_space=pltpu.SMEM)

    def cur(width):
        return pl.BlockSpec((1, tq, width), lambda bi, ti: (bi, ti, 0))

    def halo(width):
        return pl.BlockSpec((1, WINDOW, width),
                            lambda bi, ti: (bi, jnp.maximum(ti * per - 1, 0), 0))

    return pl.pallas_call(
        functools.partial(_swa_kernel, online=online),
        grid=(b, t // tq),
        in_specs=[smem, smem, smem, _resident((WINDOW, 2 * WINDOW)),
                  cur(qa.shape[-1]), cur(kw), halo(kw), cur(vw), halo(vw)],
        out_specs=cur(qa.shape[-1]),
        out_shape=jax.ShapeDtypeStruct(qa.shape, BF16),
        scratch_shapes=[pltpu.VMEM((SWA_Q_HEADS, WINDOW, 2 * WINDOW), F32),
                        pltpu.VMEM((WINDOW + tq, kw), BF16),
                        pltpu.VMEM((WINDOW + tq, vw), BF16)],
        compiler_params=_params(2),
        name="swa_online" if online else "swa",
    )(rel_bias2, sink2, shift, jnp.asarray(_t5_bucket_table()), qa, ka, ka, va, va)


def _swa_shift(q_gain, k_gain, rel_bias, sink):
    qk = HEAD_DIM ** 0.5 * jnp.max(jnp.abs(q_gain)) * jnp.max(jnp.abs(k_gain))
    hi = jnp.maximum(qk + jnp.max(rel_bias, axis=0), sink)
    lo = jnp.maximum(rel_bias[0] - qk, sink)
    usable = jnp.all((hi - lo) * LOG2E < SWA_MAX_SPREAD)
    return jnp.where(usable, hi, 0.0), usable


def _fox_kernel(q_ref, k_ref, vt_ref, o_ref, *, online):
    tq = q_ref.shape[1]
    qi = pl.program_id(2)
    qp = q_ref[0]
    lane = lax.broadcasted_iota(jnp.int32, qp.shape, 1)
    zero = jnp.zeros_like(qp)
    q_heads = [
        jnp.where((lane < HEAD_DIM) | ((lane >= LANES) & (lane < LANES + FF_REP)), qp, zero),
        jnp.where(((lane >= HEAD_DIM) & (lane < LANES)) |
                  ((lane >= LANES + FF_REP) & (lane < LANES + 2 * FF_REP)), qp, zero)]

    chains = [(hd, c) for c in range(tq // FOX_QC) for hd in range(2)]

    def run(blocks, carry):
        state = dict(enumerate(carry))
        items = []
        for j, diag in blocks:
            k0 = pl.multiple_of(j * FOX_TK, FOX_TK)
            ks = k_ref[0, pl.ds(k0, FOX_TK), :]
            vts = [vt_ref[0, hd * FOX_VROWS:(hd + 1) * FOX_VROWS, pl.ds(k0, FOX_TK)]
                   for hd in range(2)]
            key_lo = 0 if diag is None else diag * FOX_TK
            for ci, (hd, c) in enumerate(chains):
                if diag is None or key_lo <= (c + 1) * FOX_QC - 1:
                    masked = diag is not None and key_lo + FOX_TK - 1 > c * FOX_QC
                    items.append((ci, ks, vts[hd], key_lo if masked else None))

        def scores(item):
            ci, ks, _, mask_lo = item
            hd, c = chains[ci]
            st = lax.dot_general(ks, q_heads[hd][c * FOX_QC:(c + 1) * FOX_QC],
                                 (((1,), (1,)), ((), ())), preferred_element_type=F32)
            if mask_lo is not None:
                key = lax.broadcasted_iota(jnp.int32, st.shape, 0) + mask_lo
                qry = lax.broadcasted_iota(jnp.int32, st.shape, 1) + c * FOX_QC
                st = jnp.where(key <= qry, st, NEG_INF)
            return st

        def softmax(item, st):
            m, acc = state[item[0]]
            if not online:
                return m, acc, jnp.exp2(st).astype(BF16)
            m_new = jnp.maximum(m, jnp.max(st, axis=0, keepdims=True))
            return m_new, jnp.exp2(m - m_new) * acc, jnp.exp2(st - m_new).astype(BF16)

        def weighted(item, part):
            m_new, acc, p = part
            state[item[0]] = (m_new, acc + jnp.dot(item[2], p, preferred_element_type=F32))

        st_next = scores(items[0])
        part_prev = None
        for pos, item in enumerate(items):
            st_cur = st_next
            if pos + 1 < len(items):
                st_next = scores(items[pos + 1])
            part = softmax(item, st_cur)
            if part_prev is not None:
                weighted(items[pos - 1], part_prev)
            part_prev = part
        weighted(items[-1], part_prev)
        return tuple(state[ci] for ci in range(len(chains)))

    init = tuple((jnp.full((1, FOX_QC), NEG_INF, F32), jnp.zeros((FOX_VROWS, FOX_QC), F32))
                 for _ in chains)
    per = tq // FOX_TK
    assert per % FOX_UNROLL == 0
    res = lax.fori_loop(
        0, qi * (per // FOX_UNROLL),
        lambda it, c: run([(it * FOX_UNROLL + u, None) for u in range(FOX_UNROLL)], c), init)
    res = run([(qi * per + d, d) for d in range(per)], res)
    for c in range(tq // FOX_QC):
        heads = [acc[:HEAD_DIM] / acc[HEAD_DIM:HEAD_DIM + 1] for _, acc in res[2 * c:2 * c + 2]]
        out_t = jnp.concatenate(heads, axis=0)
        o_ref[0, c * FOX_QC:(c + 1) * FOX_QC, :] = out_t.T.astype(BF16)


def _fox(qf, kf, vft, *, online):
    b, t, _ = qf.shape
    assert FOX_TQ % FOX_TK == 0 and FOX_TQ % FOX_QC == 0
    return pl.pallas_call(
        functools.partial(_fox_kernel, online=online),
        grid=(b, FOX_PAIRS, t // FOX_TQ),
        in_specs=[pl.BlockSpec((1, FOX_TQ, 2 * LANES), lambda bi, pi, qi: (bi, qi, pi)),
                  pl.BlockSpec((1, t, 2 * LANES), lambda bi, pi, qi: (bi, 0, pi)),
                  pl.BlockSpec((1, 2 * FOX_VROWS, t), lambda bi, pi, qi: (bi, pi, 0))],
        out_specs=pl.BlockSpec((1, FOX_TQ, LANES), lambda bi, pi, qi: (bi, qi, pi)),
        out_shape=jax.ShapeDtypeStruct((b, t, FOX_HEADS * HEAD_DIM), BF16),
        compiler_params=_params(3),
        name="fox_online" if online else "fox",
    )(qf, kf, vft)


def _fox_shift(q_gain, k_gain):
    bound = (LOG2E * HEAD_DIM ** 0.5) * jnp.max(jnp.abs(q_gain)) * jnp.max(jnp.abs(k_gain))
    usable = bound < FOX_MAX_SHIFT
    return jnp.where(usable, bound, 0.0), usable


def _merge_kernel(x_ref, uc_ref, us_ref, oa_ref, of_ref, wb_ref, wgate_ref, wout_ref, o_ref):
    x = x_ref[0]
    half_inv = 0.5 * lax.rsqrt(jnp.mean(x * x, axis=-1, keepdims=True) + EPS)
    xb = x.astype(BF16)
    twice_merged = jnp.zeros(x.shape, F32)
    for i, u_ref in enumerate((uc_ref, us_ref, oa_ref, of_ref)):
        p = jnp.dot(u_ref[0], wb_ref[i], preferred_element_type=F32)
        half_gate = lax.dot_general(xb, wgate_ref[i * D_MODEL:(i + 1) * D_MODEL, :],
                                    (((1,), (1,)), ((), ())), preferred_element_type=F32) * half_inv
        twice_merged = twice_merged + (p + jnp.tanh(half_gate) * p)
    merged = (0.5 * twice_merged).astype(BF16)
    o_ref[0] = x + jnp.dot(merged, wout_ref[...], preferred_element_type=F32)


def _merge(x, layer, uc, us, oa, of, wb, wgate, wout):
    b, t, d = x.shape
    tm = ROW_TILE

    def rows(width):
        return pl.BlockSpec((1, tm, width), lambda bi, ti: (bi, ti, 0))

    return pl.pallas_call(
        _merge_kernel,
        grid=(b, t // tm),
        in_specs=[rows(d), rows(CONV_CH), rows(SC_CH),
                  rows(SWA_Q_HEADS * HEAD_DIM), rows(FOX_HEADS * HEAD_DIM),
                  _resident(wb.shape[1:], layer), _resident(wgate.shape[1:], layer),
                  _resident(wout.shape[1:], layer)],
        out_specs=rows(d),
        out_shape=jax.ShapeDtypeStruct((b, t, d), F32),
        compiler_params=_params(2),
        name="merge",
    )(x, uc, us, oa, of, wb, wgate, wout)


def _fgate_cols(ff, axis=-1):
    rep = jnp.repeat(ff, FF_REP, axis=axis)
    pad = [(0, 0)] * ff.ndim
    pad[axis] = (0, LANES - 2 * FF_REP)
    pairs = [lax.slice_in_dim(rep, p * 2 * FF_REP, (p + 1) * 2 * FF_REP, axis=axis)
             for p in range(FOX_PAIRS)]
    return jnp.concatenate([jnp.pad(x, pad) for x in pairs], axis=axis)


def _prep_proj(w_in, mix_gain):
    wt = jnp.transpose(w_in, (2, 0, 1))
    aq0 = COL_AQ
    ff0 = COL_FF
    heads = [wt[aq0 + h * HEAD_DIM: aq0 + (h + 1) * HEAD_DIM] for h in SWA_Q_ORDER]
    w_proj = jnp.concatenate([wt[:aq0]] + heads + [wt[COL_AK:ff0],
                                                   _fgate_cols(wt[ff0:ff0 + FOX_HEADS], axis=0)], axis=0)
    w_gate = wt[ff0 + FOX_HEADS:]
    return ((jnp.transpose(w_proj, (1, 0, 2)) * mix_gain[:, None, :]).astype(BF16),
            (jnp.transpose(w_gate, (1, 0, 2)) * mix_gain[:, None, :]).astype(BF16))


def _prep_ffn(norm_gain, w_gate, w_up, w_down):
    g = norm_gain[:, :, None]
    return (g * w_gate).astype(BF16), (g * w_up).astype(BF16), w_down.astype(BF16)


def _prep_qk_gains(aq, ak, fq, fk):
    scale = HEAD_DIM ** -0.5 * LOG2E
    ones = jnp.ones((aq.shape[0], SWA_KV_HEADS * HEAD_DIM), F32)
    row = jnp.concatenate([jnp.tile(aq * scale, (1, SWA_Q_HEADS)), jnp.tile(ak, (1, SWA_KV_HEADS)),
                           ones, jnp.tile(fq * scale, (1, FOX_HEADS)),
                           jnp.tile(fk, (1, FOX_HEADS))],
                          axis=-1)
    return row[:, None, :]


def kernel(x, rel_bias, ffn1_norm, ffn1_w_gate, ffn1_w_up, ffn1_w_down, mix_norm, w_in, b_forget, conf_dw, conf_dw_b, conf_ln_g, conf_ln_b, conf_w_out, sc_conv, sc_w_out, swa_q_norm, swa_k_norm, swa_sink, swa_w_o, fox_q_norm, fox_k_norm, fox_w_o, w_out, ffn2_norm, ffn2_w_gate, ffn2_w_up, ffn2_w_down):
    b, t, d = x.shape
    depth = w_in.shape[0]

    w_proj, w_gate = _prep_proj(w_in, mix_norm)
    qk_gains = _prep_qk_gains(swa_q_norm, swa_k_norm, fox_q_norm, fox_k_norm)
    bf = _fgate_cols(b_forget)[:, None, :]
    swa_rows = jnp.concatenate([swa_w_o[:, h * HEAD_DIM:(h + 1) * HEAD_DIM] for h in SWA_Q_ORDER],
                               axis=1)
    w_branch = jnp.stack([conf_w_out, sc_w_out, swa_rows, fox_w_o], axis=1).astype(BF16)
    w_out_b = w_out.astype(BF16)
    ffn1 = _prep_ffn(ffn1_norm, ffn1_w_gate, ffn1_w_up, ffn1_w_down)
    ffn2 = _prep_ffn(ffn2_norm, ffn2_w_gate, ffn2_w_up, ffn2_w_down)

    for l in range(depth):
        x = _ffn(x.reshape(b * t, d), l, *ffn1).reshape(b, t, d)
        shift, shifted = _fox_shift(fox_q_norm[l], fox_k_norm[l])
        zc, zs, qa, ka, va, qf, kf, vft = _inproj(x, l, w_proj, qk_gains, bf, shift)
        uc, us = _conv(zc, zs, l, conf_dw, conf_dw_b, conf_ln_g, conf_ln_b, sc_conv)
        a_shift, a_shifted = _swa_shift(swa_q_norm[l], swa_k_norm[l], rel_bias, swa_sink[l])
        oa = lax.cond(a_shifted, functools.partial(_swa, online=False),
                      functools.partial(_swa, online=True),
                      rel_bias * LOG2E, swa_sink[l] * LOG2E, a_shift * LOG2E, qa, ka, va)
        of = lax.cond(shifted, functools.partial(_fox, online=False),
                      functools.partial(_fox, online=True), qf, kf, vft)
        x = _merge(x, l, uc, us, oa, of, w_branch, w_gate, w_out_b)
        x = _ffn(x.reshape(b * t, d), l, *ffn2).reshape(b, t, d)
    return x
```

```python
import functools
import math

import numpy as np
import jax
import jax.numpy as jnp
from jax import lax
from jax.experimental import pallas as pl
from jax.experimental.pallas import tpu as pltpu

F32 = jnp.float32
BF16 = jnp.bfloat16

D_MODEL = 1024
HEAD_DIM = 64
CONV_CH = 256
CONV_K = 31
SC_CH = 256
SC_K = 3
SWA_Q_HEADS = 4
SWA_KV_HEADS = 2
WINDOW = 128
FOX_HEADS = 4
N_BUCKETS = 32
MAX_DISTANCE = 128
D_FF = 2816
N_BRANCH = 4
EPS = 1e-6
NEG_INF = -1e30
LOG2E = math.log2(math.e)

LANES = 128
SUBLANES = 8
VMEM_LIMIT = 56 * 1024 * 1024

COL_CONV = 0
COL_SC = COL_CONV + 2 * CONV_CH
COL_AQ = COL_SC + 3 * SC_CH
COL_AK = COL_AQ + SWA_Q_HEADS * HEAD_DIM
COL_AV = COL_AK + SWA_KV_HEADS * HEAD_DIM
COL_FQ = COL_AV + SWA_KV_HEADS * HEAD_DIM
COL_FK = COL_FQ + FOX_HEADS * HEAD_DIM
COL_FV = COL_FK + FOX_HEADS * HEAD_DIM
COL_FF = COL_FV + FOX_HEADS * HEAD_DIM
FOX_PAIRS = FOX_HEADS // 2
FF_REP = 9
PROJ_COLS = COL_FF + LANES
SWA_Q_ORDER = (0, 2, 1, 3)

ROW_TILE = 512
CONV_HALO = 32
SC_HALO = 8
CONV_CHUNK = 64
SWA_TILE = 512
FOX_TQ = 2048
FOX_TK = 512
FOX_QC = 256
FOX_VROWS = 80
FOX_UNROLL = 4
SWA_MAX_SPREAD = 100.0
FOX_MAX_SHIFT = 56.0


def _params(n_axes, flags=None):
    return pltpu.CompilerParams(dimension_semantics=("arbitrary",) * n_axes,
                                vmem_limit_bytes=VMEM_LIMIT, flags=flags)


def _resident(shape, layer=None):
    if layer is None:
        return pl.BlockSpec(shape, lambda *_: (0,) * len(shape), pipeline_mode=pl.Buffered(1))
    return pl.BlockSpec((None,) + tuple(shape), lambda *_: (layer,) + (0,) * len(shape),
                        pipeline_mode=pl.Buffered(1))


def _sigmoid(x):
    return 1.0 / (1.0 + jnp.exp2(x * -LOG2E))


def _ffn_kernel(x_ref, wg_ref, wu_ref, wd_ref, o_ref):
    x = x_ref[...]
    inv = lax.rsqrt(jnp.mean(x * x, axis=-1, keepdims=True) + EPS)
    xb = x.astype(BF16)
    half_gate = jnp.dot(xb, wg_ref[...], preferred_element_type=F32) * (0.5 * inv)
    up = jnp.dot(xb, wu_ref[...], preferred_element_type=F32)
    act = ((half_gate + half_gate * jnp.tanh(half_gate)) * up).astype(BF16)
    y = jnp.dot(act, wd_ref[...], preferred_element_type=F32)
    o_ref[...] = x + (0.5 * inv) * y


def _ffn(x2d, layer, wg, wu, wd):
    n, d = x2d.shape
    row = pl.BlockSpec((ROW_TILE, d), lambda i: (i, 0))
    return pl.pallas_call(
        _ffn_kernel,
        grid=(n // ROW_TILE,),
        in_specs=[row, _resident(wg.shape[1:], layer), _resident(wu.shape[1:], layer),
                  _resident(wd.shape[1:], layer)],
        out_specs=row,
        out_shape=jax.ShapeDtypeStruct((n, d), F32),
        compiler_params=_params(1),
        name="ffn",
    )(x2d, wg, wu, wd)


def _split3(v):
    hi = v.astype(BF16).astype(F32)
    r = v - hi
    mid = r.astype(BF16).astype(F32)
    return hi, mid, r - mid


def _pair_rms(zb, first, eps):
    sq = zb * zb
    s0 = jnp.sum(jnp.where(first, sq, 0.0), axis=-1, keepdims=True)
    s1 = jnp.sum(jnp.where(first, 0.0, sq), axis=-1, keepdims=True)
    inv = jnp.where(first, lax.rsqrt(s0 * (1.0 / HEAD_DIM) + eps),
                    lax.rsqrt(s1 * (1.0 / HEAD_DIM) + eps))
    return zb * inv


def _inproj_kernel(x_ref, w_ref, qkg_ref, bf_ref, shift_ref,
                   zc_ref, zs_ref, qa_ref, ka_ref, va_ref, qf_ref, kf_ref, vft_ref, carry_ref):
    tm = x_ref.shape[1]

    @pl.when(pl.program_id(1) == 0)
    def _():
        carry_ref[...] = jnp.zeros_like(carry_ref)

    x = x_ref[0]
    ms = jnp.mean(x * x, axis=-1, keepdims=True) + EPS
    inv = lax.rsqrt(ms)
    qk_eps = EPS * ms
    h = x.astype(BF16)
    lane = lax.broadcasted_iota(jnp.int32, (tm, LANES), 1)
    first = lane < HEAD_DIM
    slot = lane % FF_REP
    used = lane < FOX_HEADS * FF_REP
    ext = {}

    def normed(z, col0, i):
        g0 = col0 - COL_AQ + i * LANES
        return _pair_rms(z[:, i * LANES:(i + 1) * LANES], first, qk_eps) * qkg_ref[:, g0:g0 + LANES]

    def forget_terms(z):
        ff = z * inv + bf_ref[...]
        logf = jnp.minimum(ff, 0.0) - jnp.log1p(jnp.exp(-jnp.abs(ff)))
        rows = lax.broadcasted_iota(jnp.int32, logf.shape, 0)
        cum = logf
        step = 1
        while step < tm:
            cum = cum + jnp.where(rows >= step, pltpu.roll(cum, step, axis=0), 0.0)
            step *= 2
        cum = cum + carry_ref[...]
        carry_ref[...] = cum[tm - 1:tm, :]
        cum = cum * LOG2E
        s_hi, s_mid, s_lo = _split3(shift_ref[...])
        hi, mid, lo = _split3(cum)
        q_extra = jnp.where(slot == 0, hi, jnp.where(slot == 1, mid, jnp.where(
            slot == 2, lo, jnp.where(slot < 6, 1.0, jnp.where(
                slot == 6, -s_hi, jnp.where(slot == 7, -s_mid, -s_lo))))))
        k_extra = jnp.where(slot == 3, -hi, jnp.where(slot == 4, -mid, jnp.where(
            slot == 5, -lo, 1.0)))
        ext["q"] = jnp.where(used, q_extra, 0.0).astype(BF16)
        ext["k"] = jnp.where(used, k_extra, 0.0).astype(BF16)

    def fox_qk(z):
        for p in range(FOX_PAIRS):
            base = 2 * p * LANES
            qf_ref[0, :, base:base + LANES] = normed(z, COL_FQ, p).astype(BF16)
            qf_ref[0, :, base + LANES:base + 2 * LANES] = ext["q"]
            kf_ref[0, :, base:base + LANES] = normed(z, COL_FQ, FOX_PAIRS + p).astype(BF16)
            kf_ref[0, :, base + LANES:base + 2 * LANES] = ext["k"]

    def swa_qk(z):
        n_q = SWA_Q_HEADS // 2
        for i in range(n_q):
            qa_ref[0, :, i * LANES:(i + 1) * LANES] = normed(z, COL_AQ, i).astype(BF16)
        ka_ref[0] = normed(z, COL_AQ, n_q).astype(BF16)

    def fox_v(z):
        vt = (z * inv).T
        ones_row = (lax.broadcasted_iota(jnp.int32, (FOX_VROWS - HEAD_DIM, tm), 0) == 0).astype(F32)
        for i in range(FOX_HEADS):
            vft_ref[0, i * FOX_VROWS:(i + 1) * FOX_VROWS, :] = jnp.concatenate(
                [vt[i * HEAD_DIM:(i + 1) * HEAD_DIM], ones_row], axis=0).astype(BF16)

    def swa_v(z):
        va_ref[0, :, 0:LANES] = (z * inv).astype(BF16)
        va_ref[0, :, LANES:2 * LANES] = (lane == 0).astype(BF16)

    def conv_in(z):
        zc_ref[0] = z * inv

    def sc_in(z):
        zs_ref[0] = z * inv

    z = lax.dot_general(h, w_ref[...], (((1,), (1,)), ((), ())),
                        preferred_element_type=F32)
    for c0, c1, epilogue in ((COL_FF, PROJ_COLS, forget_terms), (COL_FQ, COL_FV, fox_qk),
                             (COL_AQ, COL_AV, swa_qk), (COL_FV, COL_FF, fox_v),
                             (COL_AV, COL_FQ, swa_v), (COL_CONV, COL_SC, conv_in),
                             (COL_SC, COL_AQ, sc_in)):
        epilogue(z[:, c0:c1])


def _inproj(x, layer, w, qkg, bf, shift):
    b, t, d = x.shape
    tm = ROW_TILE

    def rows(width):
        return pl.BlockSpec((1, tm, width), lambda bi, ti: (bi, ti, 0))

    def oshape(width, dtype=BF16):
        return jax.ShapeDtypeStruct((b, t, width), dtype)

    n_v = FOX_HEADS * FOX_VROWS
    return pl.pallas_call(
        _inproj_kernel,
        grid=(b, t // tm),
        in_specs=[rows(d), _resident(w.shape[1:], layer),
                  _resident(qkg.shape[1:], layer), _resident(bf.shape[1:], layer),
                  _resident((1, LANES))],
        out_specs=[rows(2 * CONV_CH), rows(3 * SC_CH),
                   rows(SWA_Q_HEADS * HEAD_DIM), rows(SWA_KV_HEADS * HEAD_DIM),
                   rows(2 * LANES),
                   rows(2 * FOX_PAIRS * LANES), rows(2 * FOX_PAIRS * LANES),
                   pl.BlockSpec((1, n_v, tm), lambda bi, ti: (bi, 0, ti))],
        out_shape=[oshape(2 * CONV_CH, F32), oshape(3 * SC_CH, F32),
                   oshape(SWA_Q_HEADS * HEAD_DIM), oshape(SWA_KV_HEADS * HEAD_DIM),
                   oshape(2 * LANES),
                   oshape(2 * FOX_PAIRS * LANES), oshape(2 * FOX_PAIRS * LANES),
                   jax.ShapeDtypeStruct((b, n_v, t), BF16)],
        scratch_shapes=[pltpu.VMEM((1, LANES), F32)],
        compiler_params=_params(2),
        name="inproj",
    )(x, w, qkg, bf, jnp.full((1, LANES), shift, F32))


def _fill_shifted(ext_ref, base, n_taps):
    n = ext_ref.shape[1] - SUBLANES
    for r in sorted({(base + k) % SUBLANES for k in range(n_taps)} - {0}):
        ext_ref[r, 0:n, :] = ext_ref[0, r:r + n, :]


def _taps(ext_ref, w_ref, r0, base, n_taps, acc):
    groups = CONV_CHUNK // SUBLANES
    acc = acc.reshape(groups, SUBLANES, acc.shape[-1])
    for k in range(n_taps):
        r = (base + k) % SUBLANES
        start = r0 + base + k - r
        u = ext_ref[r, start:start + CONV_CHUNK, :]
        acc = acc + w_ref[k][None] * u.reshape(groups, SUBLANES, u.shape[-1])
    return acc.reshape(CONV_CHUNK, acc.shape[-1])


def _conv_kernel(zc_ref, zch_ref, zs_ref, zsh_ref, dw_ref, dwb_ref, lng_ref, lnb_ref, scw_ref,
                 uc_ref, us_ref, ext_ref, ext2_ref):
    tt = zc_ref.shape[1]
    keep = (pl.program_id(1) > 0).astype(F32)

    def glu(v):
        return v[:, :CONV_CH] * _sigmoid(v[:, CONV_CH:])

    ext_ref[0, 0:CONV_HALO, :] = glu(zch_ref[0]) * keep
    ext_ref[0, CONV_HALO:, :] = glu(zc_ref[0])
    zsh = zsh_ref[0]
    ext2_ref[0, 0:SC_HALO, :] = zsh[:, SC_CH:2 * SC_CH] * zsh[:, 2 * SC_CH:] * keep
    zs = zs_ref[0]
    ext2_ref[0, SC_HALO:, :] = zs[:, SC_CH:2 * SC_CH] * zs[:, 2 * SC_CH:]
    conv_base = CONV_HALO - (CONV_K - 1)
    sc_base = SC_HALO - (SC_K - 1)
    _fill_shifted(ext_ref, conv_base, CONV_K)
    _fill_shifted(ext2_ref, sc_base, SC_K)

    for r0 in range(0, tt, CONV_CHUNK):
        acc = _taps(ext_ref, dw_ref, r0, conv_base, CONV_K,
                    jnp.broadcast_to(dwb_ref[...], (CONV_CHUNK, CONV_CH)))
        mu = jnp.mean(acc, axis=-1, keepdims=True)
        cen = acc - mu
        var = jnp.mean(cen * cen, axis=-1, keepdims=True)
        y = cen * lax.rsqrt(var + EPS) * lng_ref[...] + lnb_ref[...]
        uc_ref[0, r0:r0 + CONV_CHUNK, :] = (y * _sigmoid(y)).astype(BF16)

        acc2 = _taps(ext2_ref, scw_ref, r0, sc_base, SC_K, jnp.zeros((CONV_CHUNK, SC_CH), F32))
        us_ref[0, r0:r0 + CONV_CHUNK, :] = (zs[r0:r0 + CONV_CHUNK, :SC_CH] * acc2).astype(BF16)


def _conv(zc, zs, layer, dw, dwb, lng, lnb, scw):
    b, t, _ = zc.shape
    tt = ROW_TILE

    def rows(width):
        return pl.BlockSpec((1, tt, width), lambda bi, ti: (bi, ti, 0))

    def halo(rows_, width):
        per = tt // rows_
        return pl.BlockSpec((1, rows_, width),
                            lambda bi, ti: (bi, jnp.maximum(ti * per - 1, 0), 0))

    def vec(a):
        return a.reshape(a.shape[0], 1, a.shape[1])

    def taps(w):
        return jnp.broadcast_to(w[:, :, None, :], w.shape[:2] + (SUBLANES, w.shape[2]))

    return pl.pallas_call(
        _conv_kernel,
        grid=(b, t // tt),
        in_specs=[rows(2 * CONV_CH), halo(CONV_HALO, 2 * CONV_CH),
                  rows(3 * SC_CH), halo(SC_HALO, 3 * SC_CH),
                  _resident((CONV_K, SUBLANES, CONV_CH), layer), _resident((1, CONV_CH), layer),
                  _resident((1, CONV_CH), layer), _resident((1, CONV_CH), layer),
                  _resident((SC_K, SUBLANES, SC_CH), layer)],
        out_specs=[rows(CONV_CH), rows(SC_CH)],
        out_shape=[jax.ShapeDtypeStruct((b, t, CONV_CH), BF16),
                   jax.ShapeDtypeStruct((b, t, SC_CH), BF16)],
        scratch_shapes=[pltpu.VMEM((SUBLANES, CONV_HALO + tt, CONV_CH), F32),
                        pltpu.VMEM((SUBLANES, SC_HALO + tt, SC_CH), F32)],
        compiler_params=_params(2),
        name="conv",
    )(zc, zc, zs, zs, taps(dw), vec(dwb), vec(lng), vec(lnb), taps(scw))


def _t5_bucket_table():
    max_exact = N_BUCKETS // 2
    dist = np.maximum(np.arange(WINDOW)[:, None] + WINDOW - np.arange(2 * WINDOW)[None, :], 0)
    d = np.maximum(dist, 1).astype(np.float32)
    large = max_exact + (np.log(d / np.float32(max_exact)) / np.float32(
        math.log(MAX_DISTANCE / max_exact)) * np.float32(N_BUCKETS - max_exact)).astype(np.int32)
    large = np.minimum(large, N_BUCKETS - 1)
    return np.where(dist < max_exact, dist, large).astype(np.int32)


def _swa_kernel(rb_ref, sink_ref, shift_ref, bucket_ref, q_ref, k_ref, kh_ref, v_ref, vh_ref, o_ref,
                bias_ref, kk_ref, vv_ref, *, online):
    tq = q_ref.shape[1]
    first_step = (pl.program_id(0) == 0) & (pl.program_id(1) == 0)

    @pl.when(first_step)
    def _():
        bucket = bucket_ref[...]
        qi = lax.broadcasted_iota(jnp.int32, bucket.shape, 0) + WINDOW
        ki = lax.broadcasted_iota(jnp.int32, bucket.shape, 1)
        dist = qi - ki
        ok = (dist >= 0) & (dist < WINDOW)
        for h in range(SWA_Q_HEADS):
            bias = jnp.zeros(bucket.shape, F32)
            for bk in range(N_BUCKETS):
                bias = jnp.where(bucket == bk, rb_ref[bk, h], bias)
            bias_ref[h] = jnp.where(ok, bias - shift_ref[h], NEG_INF)

    kk_ref[0:WINDOW, :] = kh_ref[0]
    kk_ref[WINDOW:, :] = k_ref[0]
    vv_ref[0:WINDOW, :] = vh_ref[0]
    vv_ref[WINDOW:, :] = v_ref[0]
    seq_start = pl.program_id(1) == 0
    prev_half = lax.broadcasted_iota(jnp.int32, (WINDOW, 2 * WINDOW), 1) < WINDOW
    first = lax.broadcasted_iota(jnp.int32, (WINDOW, LANES), 1) < HEAD_DIM

    for sb in range(tq // WINDOW):
        r0 = sb * WINDOW
        kk = kk_ref[r0:r0 + 2 * WINDOW, :]
        vv = vv_ref[r0:r0 + 2 * WINDOW, :]
        for blk in range(SWA_Q_HEADS // 2):
            qb = q_ref[0, r0:r0 + WINDOW, blk * LANES:(blk + 1) * LANES]
            outs = []
            for half in range(2):
                h = SWA_Q_ORDER[2 * blk + half]
                keep = first if half == 0 else jnp.logical_not(first)
                q = jnp.where(keep, qb, jnp.zeros_like(qb))
                s = lax.dot_general(q, kk, (((1,), (1,)), ((), ())), preferred_element_type=F32)
                s = s + bias_ref[h]
                if sb == 0:
                    s = jnp.where(seq_start & prev_half, NEG_INF, s)
                sink = jnp.full((1, 1), sink_ref[h] - shift_ref[h], F32)
                if online:
                    m = jnp.maximum(jnp.max(s, axis=-1, keepdims=True), sink)
                    s, sink = s - m, sink - m
                pv = jnp.dot(jnp.exp2(s).astype(BF16), vv, preferred_element_type=F32)
                outs.append(pv[:, :LANES] / (pv[:, LANES:LANES + 1] + jnp.exp2(sink)))
            o_ref[0, r0:r0 + WINDOW, blk * LANES:(blk + 1) * LANES] = jnp.where(
                first, outs[0], outs[1]).astype(BF16)


def _swa(rel_bias2, sink2, shift, qa, ka, va, *, online):
    b, t, _ = qa.shape
    tq = SWA_TILE
    per = tq // WINDOW
    kw, vw = ka.shape[-1], va.shape[-1]
    smem = pl.BlockSpec(memory_space=pltpu.SMEM)

    def cur(width):
        return pl.BlockSpec((1, tq, width), lambda bi, ti: (bi, ti, 0))

    def halo(width):
        return pl.BlockSpec((1, WINDOW, width),
                            lambda bi, ti: (bi, jnp.maximum(ti * per - 1, 0), 0))

    return pl.pallas_call(
        functools.partial(_swa_kernel, online=online),
        grid=(b, t // tq),
        in_specs=[smem, smem, smem, _resident((WINDOW, 2 * WINDOW)),
                  cur(qa.shape[-1]), cur(kw), halo(kw), cur(vw), halo(vw)],
        out_specs=cur(qa.shape[-1]),
        out_shape=jax.ShapeDtypeStruct(qa.shape, BF16),
        scratch_shapes=[pltpu.VMEM((SWA_Q_HEADS, WINDOW, 2 * WINDOW), F32),
                        pltpu.VMEM((WINDOW + tq, kw), BF16),
                        pltpu.VMEM((WINDOW + tq, vw), BF16)],
        compiler_params=_params(2),
        name="swa_online" if online else "swa",
    )(rel_bias2, sink2, shift, jnp.asarray(_t5_bucket_table()), qa, ka, ka, va, va)


def _swa_shift(q_gain, k_gain, rel_bias, sink):
    qk = HEAD_DIM ** 0.5 * jnp.max(jnp.abs(q_gain)) * jnp.max(jnp.abs(k_gain))
    hi = jnp.maximum(qk + jnp.max(rel_bias, axis=0), sink)
    lo = jnp.maximum(rel_bias[0] - qk, sink)
    usable = jnp.all((hi - lo) * LOG2E < SWA_MAX_SPREAD)
    return jnp.where(usable, hi, 0.0), usable


def _fox_kernel(q_ref, k_ref, vt_ref, o_ref, *, online):
    tq = q_ref.shape[1]
    qi = pl.program_id(2)
    qp = q_ref[0]
    lane = lax.broadcasted_iota(jnp.int32, qp.shape, 1)
    zero = jnp.zeros_like(qp)
    x0 = LANES + pl.program_id(1) * (2 * FF_REP)
    q_heads = [
        jnp.where((lane < HEAD_DIM) | ((lane >= x0) & (lane < x0 + FF_REP)), qp, zero),
        jnp.where(((lane >= HEAD_DIM) & (lane < LANES)) |
                  ((lane >= x0 + FF_REP) & (lane < x0 + 2 * FF_REP)), qp, zero)]

    chains = [(hd, c) for c in range(tq // FOX_QC) for hd in range(2)]

    def run(blocks, carry):
        state = dict(enumerate(carry))
        items = []
        for j, diag in blocks:
            k0 = pl.multiple_of(j * FOX_TK, FOX_TK)
            ks = k_ref[0, pl.ds(k0, FOX_TK), :]
            vts = [vt_ref[0, hd * FOX_VROWS:(hd + 1) * FOX_VROWS, pl.ds(k0, FOX_TK)]
                   for hd in range(2)]
            key_lo = 0 if diag is None else diag * FOX_TK
            for ci, (hd, c) in enumerate(chains):
                if diag is None or key_lo <= (c + 1) * FOX_QC - 1:
                    masked = diag is not None and key_lo + FOX_TK - 1 > c * FOX_QC
                    items.append((ci, ks, vts[hd], key_lo if masked else None))

        def scores(item):
            ci, ks, _, mask_lo = item
            hd, c = chains[ci]
            st = lax.dot_general(ks, q_heads[hd][c * FOX_QC:(c + 1) * FOX_QC],
                                 (((1,), (1,)), ((), ())), preferred_element_type=F32)
            if mask_lo is not None:
                key = lax.broadcasted_iota(jnp.int32, st.shape, 0) + mask_lo
                qry = lax.broadcasted_iota(jnp.int32, st.shape, 1) + c * FOX_QC
                st = jnp.where(key <= qry, st, NEG_INF)
            return st

        def softmax(item, st):
            m, acc = state[item[0]]
            if not online:
                return m, acc, jnp.exp2(st).astype(BF16)
            m_new = jnp.maximum(m, jnp.max(st, axis=0, keepdims=True))
            return m_new, jnp.exp2(m - m_new) * acc, jnp.exp2(st - m_new).astype(BF16)

        def weighted(item, part):
            m_new, acc, p = part
            state[item[0]] = (m_new, acc + jnp.dot(item[2], p, preferred_element_type=F32))

        st_next = scores(items[0])
        part_prev = None
        for pos, item in enumerate(items):
            st_cur = st_next
            if pos + 1 < len(items):
                st_next = scores(items[pos + 1])
            part = softmax(item, st_cur)
            if part_prev is not None:
                weighted(items[pos - 1], part_prev)
            part_prev = part
        weighted(items[-1], part_prev)
        return tuple(state[ci] for ci in range(len(chains)))

    init = tuple((jnp.full((1, FOX_QC), NEG_INF, F32), jnp.zeros((FOX_VROWS, FOX_QC), F32))
                 for _ in chains)
    per = tq // FOX_TK
    assert per % FOX_UNROLL == 0
    res = lax.fori_loop(
        0, qi * (per // FOX_UNROLL),
        lambda it, c: run([(it * FOX_UNROLL + u, None) for u in range(FOX_UNROLL)], c), init)
    res = run([(qi * per + d, d) for d in range(per)], res)
    for c in range(tq // FOX_QC):
        heads = [acc[:HEAD_DIM] / acc[HEAD_DIM:HEAD_DIM + 1] for _, acc in res[2 * c:2 * c + 2]]
        out_t = jnp.concatenate(heads, axis=0)
        o_ref[0, c * FOX_QC:(c + 1) * FOX_QC, :] = out_t.T.astype(BF16)


def _fox(qf, kf, vft, *, online):
    b, t, _ = qf.shape
    assert FOX_TQ % FOX_TK == 0 and FOX_TQ % FOX_QC == 0
    return pl.pallas_call(
        functools.partial(_fox_kernel, online=online),
        grid=(b, FOX_PAIRS, t // FOX_TQ),
        in_specs=[pl.BlockSpec((1, FOX_TQ, 2 * LANES), lambda bi, pi, qi: (bi, qi, pi)),
                  pl.BlockSpec((1, t, 2 * LANES), lambda bi, pi, qi: (bi, 0, pi)),
                  pl.BlockSpec((1, 2 * FOX_VROWS, t), lambda bi, pi, qi: (bi, pi, 0))],
        out_specs=pl.BlockSpec((1, FOX_TQ, LANES), lambda bi, pi, qi: (bi, qi, pi)),
        out_shape=jax.ShapeDtypeStruct((b, t, FOX_HEADS * HEAD_DIM), BF16),
        compiler_params=_params(3),
        name="fox_online" if online else "fox",
    )(qf, kf, vft)


def _fox_shift(q_gain, k_gain):
    bound = (LOG2E * HEAD_DIM ** 0.5) * jnp.max(jnp.abs(q_gain)) * jnp.max(jnp.abs(k_gain))
    usable = bound < FOX_MAX_SHIFT
    return jnp.where(usable, bound, 0.0), usable


def _merge_kernel(x_ref, uc_ref, us_ref, oa_ref, of_ref, wb_ref, wgate_ref, wout_ref, o_ref):
    x = x_ref[0]
    half_inv = 0.5 * lax.rsqrt(jnp.mean(x * x, axis=-1, keepdims=True) + EPS)
    xb = x.astype(BF16)
    twice_merged = jnp.zeros(x.shape, F32)
    for i, u_ref in enumerate((uc_ref, us_ref, oa_ref, of_ref)):
        p = jnp.dot(u_ref[0], wb_ref[i], preferred_element_type=F32)
        half_gate = lax.dot_general(xb, wgate_ref[i * D_MODEL:(i + 1) * D_MODEL, :],
                                    (((1,), (1,)), ((), ())), preferred_element_type=F32) * half_inv
        twice_merged = twice_merged + (p + jnp.tanh(half_gate) * p)
    merged = (0.5 * twice_merged).astype(BF16)
    o_ref[0] = x + jnp.dot(merged, wout_ref[...], preferred_element_type=F32)


def _merge(x, layer, uc, us, oa, of, wb, wgate, wout):
    b, t, d = x.shape
    tm = ROW_TILE

    def rows(width):
        return pl.BlockSpec((1, tm, width), lambda bi, ti: (bi, ti, 0))

    return pl.pallas_call(
        _merge_kernel,
        grid=(b, t // tm),
        in_specs=[rows(d), rows(CONV_CH), rows(SC_CH),
                  rows(SWA_Q_HEADS * HEAD_DIM), rows(FOX_HEADS * HEAD_DIM),
                  _resident(wb.shape[1:], layer), _resident(wgate.shape[1:], layer),
                  _resident(wout.shape[1:], layer)],
        out_specs=rows(d),
        out_shape=jax.ShapeDtypeStruct((b, t, d), F32),
        compiler_params=_params(2),
        name="merge",
    )(x, uc, us, oa, of, wb, wgate, wout)


def _fgate_cols(ff, axis=-1):
    rep = jnp.repeat(ff, FF_REP, axis=axis)
    pad = [(0, 0)] * ff.ndim
    pad[axis] = (0, LANES - FOX_HEADS * FF_REP)
    return jnp.pad(rep, pad)


def _prep_proj(w_in, mix_gain):
    wt = jnp.transpose(w_in, (2, 0, 1))
    aq0 = COL_AQ
    ff0 = COL_FF
    heads = [wt[aq0 + h * HEAD_DIM: aq0 + (h + 1) * HEAD_DIM] for h in SWA_Q_ORDER]
    w_proj = jnp.concatenate([wt[:aq0]] + heads + [wt[COL_AK:ff0],
                                                   _fgate_cols(wt[ff0:ff0 + FOX_HEADS], axis=0)], axis=0)
    w_gate = wt[ff0 + FOX_HEADS:]
    return ((jnp.transpose(w_proj, (1, 0, 2)) * mix_gain[:, None, :]).astype(BF16),
            (jnp.transpose(w_gate, (1, 0, 2)) * mix_gain[:, None, :]).astype(BF16))


def _prep_ffn(norm_gain, w_gate, w_up, w_down):
    g = norm_gain[:, :, None]
    return (g * w_gate).astype(BF16), (g * w_up).astype(BF16), w_down.astype(BF16)


def _prep_qk_gains(aq, ak, fq, fk):
    scale = HEAD_DIM ** -0.5 * LOG2E
    ones = jnp.ones((aq.shape[0], SWA_KV_HEADS * HEAD_DIM), F32)
    row = jnp.concatenate([jnp.tile(aq * scale, (1, SWA_Q_HEADS)), jnp.tile(ak, (1, SWA_KV_HEADS)),
                           ones, jnp.tile(fq * scale, (1, FOX_HEADS)),
                           jnp.tile(fk, (1, FOX_HEADS))],
                          axis=-1)
    return row[:, None, :]


def kernel(x, rel_bias, ffn1_norm, ffn1_w_gate, ffn1_w_up, ffn1_w_down, mix_norm, w_in, b_forget, conf_dw, conf_dw_b, conf_ln_g, conf_ln_b, conf_w_out, sc_conv, sc_w_out, swa_q_norm, swa_k_norm, swa_sink, swa_w_o, fox_q_norm, fox_k_norm, fox_w_o, w_out, ffn2_norm, ffn2_w_gate, ffn2_w_up, ffn2_w_down):
    b, t, d = x.shape
    depth = w_in.shape[0]

    w_proj, w_gate = _prep_proj(w_in, mix_norm)
    qk_gains = _prep_qk_gains(swa_q_norm, swa_k_norm, fox_q_norm, fox_k_norm)
    bf = _fgate_cols(b_forget)[:, None, :]
    swa_rows = jnp.concatenate([swa_w_o[:, h * HEAD_DIM:(h + 1) * HEAD_DIM] for h in SWA_Q_ORDER],
                               axis=1)
    w_branch = jnp.stack([conf_w_out, sc_w_out, swa_rows, fox_w_o], axis=1).astype(BF16)
    w_out_b = w_out.astype(BF16)
    ffn1 = _prep_ffn(ffn1_norm, ffn1_w_gate, ffn1_w_up, ffn1_w_down)
    ffn2 = _prep_ffn(ffn2_norm, ffn2_w_gate, ffn2_w_up, ffn2_w_down)

    for l in range(depth):
        x = _ffn(x.reshape(b * t, d), l, *ffn1).reshape(b, t, d)
        shift, shifted = _fox_shift(fox_q_norm[l], fox_k_norm[l])
        zc, zs, qa, ka, va, qf, kf, vft = _inproj(x, l, w_proj, qk_gains, bf, shift)
        uc, us = _conv(zc, zs, l, conf_dw, conf_dw_b, conf_ln_g, conf_ln_b, sc_conv)
        a_shift, a_shifted = _swa_shift(swa_q_norm[l], swa_k_norm[l], rel_bias, swa_sink[l])
        oa = lax.cond(a_shifted, functools.partial(_swa, online=False),
                      functools.partial(_swa, online=True),
                      rel_bias * LOG2E, swa_sink[l] * LOG2E, a_shift * LOG2E, qa, ka, va)
        of = lax.cond(shifted, functools.partial(_fox, online=False),
                      functools.partial(_fox, online=True), qf, kf, vft)
        x = _merge(x, l, uc, us, oa, of, w_branch, w_gate, w_out_b)
        x = _ffn(x.reshape(b * t, d), l, *ffn2).reshape(b, t, d)
    return x
```

```python
import functools
import math

import numpy as np
import jax
import jax.numpy as jnp
from jax import lax
from jax.experimental import pallas as pl
from jax.experimental.pallas import tpu as pltpu

F32 = jnp.float32
BF16 = jnp.bfloat16

D_MODEL = 1024
HEAD_DIM = 64
CONV_CH = 256
CONV_K = 31
SC_CH = 256
SC_K = 3
SWA_Q_HEADS = 4
SWA_KV_HEADS = 2
WINDOW = 128
FOX_HEADS = 4
N_BUCKETS = 32
MAX_DISTANCE = 128
D_FF = 2816
N_BRANCH = 4
EPS = 1e-6
NEG_INF = -1e30
LOG2E = math.log2(math.e)

LANES = 128
SUBLANES = 8
VMEM_LIMIT = 56 * 1024 * 1024

COL_CONV = 0
COL_SC = COL_CONV + 2 * CONV_CH
COL_AQ = COL_SC + 3 * SC_CH
COL_AK = COL_AQ + SWA_Q_HEADS * HEAD_DIM
COL_AV = COL_AK + SWA_KV_HEADS * HEAD_DIM
COL_FQ = COL_AV + SWA_KV_HEADS * HEAD_DIM
COL_FK = COL_FQ + FOX_HEADS * HEAD_DIM
COL_FV = COL_FK + FOX_HEADS * HEAD_DIM
COL_FF = COL_FV + FOX_HEADS * HEAD_DIM
FOX_PAIRS = FOX_HEADS // 2
FF_REP = 9
PROJ_COLS = COL_FF + LANES
SWA_Q_ORDER = (0, 2, 1, 3)

ROW_TILE = 512
CONV_HALO = 32
SC_HALO = 8
CONV_CHUNK = 64
SWA_TILE = 512
FOX_TQ = 2048
FOX_TK = 512
FOX_QC = 256
FOX_VROWS = 80
FOX_UNROLL = 4
SWA_MAX_SPREAD = 100.0
FOX_MAX_SHIFT = 56.0


def _params(n_axes, flags=None):
    return pltpu.CompilerParams(dimension_semantics=("arbitrary",) * n_axes,
                                vmem_limit_bytes=VMEM_LIMIT, flags=flags)


def _resident(shape, layer=None):
    if layer is None:
        return pl.BlockSpec(shape, lambda *_: (0,) * len(shape), pipeline_mode=pl.Buffered(1))
    return pl.BlockSpec((None,) + tuple(shape), lambda *_: (layer,) + (0,) * len(shape),
                        pipeline_mode=pl.Buffered(1))


def _sigmoid(x):
    return 1.0 / (1.0 + jnp.exp2(x * -LOG2E))


def _ffn_kernel(x_ref, wg_ref, wu_ref, wd_ref, o_ref):
    x = x_ref[...]
    inv = lax.rsqrt(jnp.mean(x * x, axis=-1, keepdims=True) + EPS)
    xb = x.astype(BF16)
    half_gate = jnp.dot(xb, wg_ref[...], preferred_element_type=F32) * (0.5 * inv)
    up = jnp.dot(xb, wu_ref[...], preferred_element_type=F32)
    act = ((half_gate + half_gate * jnp.tanh(half_gate)) * up).astype(BF16)
    y = jnp.dot(act, wd_ref[...], preferred_element_type=F32)
    o_ref[...] = x + (0.5 * inv) * y


def _ffn(x2d, layer, wg, wu, wd):
    n, d = x2d.shape
    row = pl.BlockSpec((ROW_TILE, d), lambda i: (i, 0))
    return pl.pallas_call(
        _ffn_kernel,
        grid=(n // ROW_TILE,),
        in_specs=[row, _resident(wg.shape[1:], layer), _resident(wu.shape[1:], layer),
                  _resident(wd.shape[1:], layer)],
        out_specs=row,
        out_shape=jax.ShapeDtypeStruct((n, d), F32),
        compiler_params=_params(1),
        name="ffn",
    )(x2d, wg, wu, wd)


def _split3(v):
    hi = v.astype(BF16).astype(F32)
    r = v - hi
    mid = r.astype(BF16).astype(F32)
    return hi, mid, r - mid


def _pair_rms(zb, first, eps):
    sq = zb * zb
    s0 = jnp.sum(jnp.where(first, sq, 0.0), axis=-1, keepdims=True)
    s1 = jnp.sum(jnp.where(first, 0.0, sq), axis=-1, keepdims=True)
    inv = jnp.where(first, lax.rsqrt(s0 * (1.0 / HEAD_DIM) + eps),
                    lax.rsqrt(s1 * (1.0 / HEAD_DIM) + eps))
    return zb * inv


def _inproj_kernel(x_ref, w_ref, qkg_ref, bf_ref, shift_ref,
                   xb_ref, zc_ref, zs_ref, qa_ref, ka_ref, va_ref, qf_ref, kf_ref, vft_ref,
                   carry_ref):
    tm = x_ref.shape[1]

    @pl.when(pl.program_id(1) == 0)
    def _():
        carry_ref[...] = jnp.zeros_like(carry_ref)

    x = x_ref[0]
    ms = jnp.mean(x * x, axis=-1, keepdims=True) + EPS
    inv = lax.rsqrt(ms)
    qk_eps = EPS * ms
    h = x.astype(BF16)
    xb_ref[0] = h
    lane = lax.broadcasted_iota(jnp.int32, (tm, LANES), 1)
    first = lane < HEAD_DIM
    slot = lane % FF_REP
    used = lane < FOX_HEADS * FF_REP
    ext = {}

    def normed(z, col0, i):
        g0 = col0 - COL_AQ + i * LANES
        return _pair_rms(z[:, i * LANES:(i + 1) * LANES], first, qk_eps) * qkg_ref[:, g0:g0 + LANES]

    def forget_terms(z):
        ff = z * inv + bf_ref[...]
        logf = jnp.minimum(ff, 0.0) - jnp.log1p(jnp.exp(-jnp.abs(ff)))
        rows = lax.broadcasted_iota(jnp.int32, logf.shape, 0)
        cum = logf
        step = 1
        while step < tm:
            cum = cum + jnp.where(rows >= step, pltpu.roll(cum, step, axis=0), 0.0)
            step *= 2
        cum = cum + carry_ref[...]
        carry_ref[...] = cum[tm - 1:tm, :]
        cum = cum * LOG2E
        s_hi, s_mid, s_lo = _split3(shift_ref[...])
        hi, mid, lo = _split3(cum)
        q_extra = jnp.where(slot == 0, hi, jnp.where(slot == 1, mid, jnp.where(
            slot == 2, lo, jnp.where(slot < 6, 1.0, jnp.where(
                slot == 6, -s_hi, jnp.where(slot == 7, -s_mid, -s_lo))))))
        k_extra = jnp.where(slot == 3, -hi, jnp.where(slot == 4, -mid, jnp.where(
            slot == 5, -lo, 1.0)))
        ext["q"] = jnp.where(used, q_extra, 0.0).astype(BF16)
        ext["k"] = jnp.where(used, k_extra, 0.0).astype(BF16)

    def fox_qk(z):
        for p in range(FOX_PAIRS):
            base = 2 * p * LANES
            qf_ref[0, :, base:base + LANES] = normed(z, COL_FQ, p).astype(BF16)
            qf_ref[0, :, base + LANES:base + 2 * LANES] = ext["q"]
            kf_ref[0, :, base:base + LANES] = normed(z, COL_FQ, FOX_PAIRS + p).astype(BF16)
            kf_ref[0, :, base + LANES:base + 2 * LANES] = ext["k"]

    def swa_qk(z):
        n_q = SWA_Q_HEADS // 2
        for i in range(n_q):
            qa_ref[0, :, i * LANES:(i + 1) * LANES] = normed(z, COL_AQ, i).astype(BF16)
        ka_ref[0] = normed(z, COL_AQ, n_q).astype(BF16)

    def fox_v(z):
        vt = (z * inv).T
        ones_row = (lax.broadcasted_iota(jnp.int32, (FOX_VROWS - HEAD_DIM, tm), 0) == 0).astype(F32)
        for i in range(FOX_HEADS):
            vft_ref[0, i * FOX_VROWS:(i + 1) * FOX_VROWS, :] = jnp.concatenate(
                [vt[i * HEAD_DIM:(i + 1) * HEAD_DIM], ones_row], axis=0).astype(BF16)

    def swa_v(z):
        va_ref[0, :, 0:LANES] = (z * inv).astype(BF16)
        va_ref[0, :, LANES:2 * LANES] = (lane == 0).astype(BF16)

    def conv_in(z):
        zc_ref[0] = z * inv

    def sc_in(z):
        zs_ref[0] = z * inv

    z = lax.dot_general(h, w_ref[...], (((1,), (1,)), ((), ())),
                        preferred_element_type=F32)
    for c0, c1, epilogue in ((COL_FF, PROJ_COLS, forget_terms), (COL_FQ, COL_FV, fox_qk),
                             (COL_AQ, COL_AV, swa_qk), (COL_FV, COL_FF, fox_v),
                             (COL_AV, COL_FQ, swa_v), (COL_CONV, COL_SC, conv_in),
                             (COL_SC, COL_AQ, sc_in)):
        epilogue(z[:, c0:c1])


def _inproj(x, layer, w, qkg, bf, shift):
    b, t, d = x.shape
    tm = ROW_TILE

    def rows(width):
        return pl.BlockSpec((1, tm, width), lambda bi, ti: (bi, ti, 0))

    def oshape(width, dtype=BF16):
        return jax.ShapeDtypeStruct((b, t, width), dtype)

    n_v = FOX_HEADS * FOX_VROWS
    return pl.pallas_call(
        _inproj_kernel,
        grid=(b, t // tm),
        in_specs=[rows(d), _resident(w.shape[1:], layer),
                  _resident(qkg.shape[1:], layer), _resident(bf.shape[1:], layer),
                  _resident((1, LANES))],
        out_specs=[rows(d), rows(2 * CONV_CH), rows(3 * SC_CH),
                   rows(SWA_Q_HEADS * HEAD_DIM), rows(SWA_KV_HEADS * HEAD_DIM),
                   rows(2 * LANES),
                   rows(2 * FOX_PAIRS * LANES), rows(2 * FOX_PAIRS * LANES),
                   pl.BlockSpec((1, n_v, tm), lambda bi, ti: (bi, 0, ti))],
        out_shape=[oshape(d), oshape(2 * CONV_CH, F32), oshape(3 * SC_CH, F32),
                   oshape(SWA_Q_HEADS * HEAD_DIM), oshape(SWA_KV_HEADS * HEAD_DIM),
                   oshape(2 * LANES),
                   oshape(2 * FOX_PAIRS * LANES), oshape(2 * FOX_PAIRS * LANES),
                   jax.ShapeDtypeStruct((b, n_v, t), BF16)],
        scratch_shapes=[pltpu.VMEM((1, LANES), F32)],
        compiler_params=_params(2),
        name="inproj",
    )(x, w, qkg, bf, jnp.full((1, LANES), shift, F32))


def _fill_shifted(ext_ref, base, n_taps):
    n = ext_ref.shape[1] - SUBLANES
    for r in sorted({(base + k) % SUBLANES for k in range(n_taps)} - {0}):
        ext_ref[r, 0:n, :] = ext_ref[0, r:r + n, :]


def _taps(ext_ref, w_ref, r0, base, n_taps, acc):
    groups = CONV_CHUNK // SUBLANES
    acc = acc.reshape(groups, SUBLANES, acc.shape[-1])
    for k in range(n_taps):
        r = (base + k) % SUBLANES
        start = r0 + base + k - r
        u = ext_ref[r, start:start + CONV_CHUNK, :]
        acc = acc + w_ref[k][None] * u.reshape(groups, SUBLANES, u.shape[-1])
    return acc.reshape(CONV_CHUNK, acc.shape[-1])


def _ordered_zero(after):
    bits = lax.bitcast_convert_type(after[0:SUBLANES, 0:LANES], jnp.uint32)
    return ((bits >> 16) >> 16)[0:1, 0:1].astype(F32)


def _conv_stages(zc_ref, zch_ref, zs_ref, zsh_ref, dw_ref, dwb_ref, lng_ref, lnb_ref, scw_ref,
                 uc_ref, us_ref, ext_ref, ext2_ref):
    tt = zc_ref.shape[1]
    conv_base = CONV_HALO - (CONV_K - 1)
    sc_base = SC_HALO - (SC_K - 1)

    def prepare(after):
        keep = (pl.program_id(1) > 0).astype(F32) + _ordered_zero(after)

        def glu(v):
            return v[:, :CONV_CH] * _sigmoid(v[:, CONV_CH:])

        ext_ref[0, 0:CONV_HALO, :] = glu(zch_ref[0]) * keep
        ext_ref[0, CONV_HALO:, :] = glu(zc_ref[0])
        zsh = zsh_ref[0]
        ext2_ref[0, 0:SC_HALO, :] = zsh[:, SC_CH:2 * SC_CH] * zsh[:, 2 * SC_CH:] * keep
        ext2_ref[0, SC_HALO:, :] = zs_ref[0, :, SC_CH:2 * SC_CH] * zs_ref[0, :, 2 * SC_CH:]
        _fill_shifted(ext_ref, conv_base, CONV_K)
        _fill_shifted(ext2_ref, sc_base, SC_K)

    def chunk(r0, after):
        acc = _taps(ext_ref, dw_ref, r0, conv_base, CONV_K,
                    jnp.broadcast_to(dwb_ref[...] + _ordered_zero(after), (CONV_CHUNK, CONV_CH)))
        mu = jnp.mean(acc, axis=-1, keepdims=True)
        cen = acc - mu
        var = jnp.mean(cen * cen, axis=-1, keepdims=True)
        y = cen * lax.rsqrt(var + EPS) * lng_ref[...] + lnb_ref[...]
        uc_ref[r0:r0 + CONV_CHUNK, :] = (y * _sigmoid(y)).astype(BF16)
        acc2 = _taps(ext2_ref, scw_ref, r0, sc_base, SC_K, jnp.zeros((CONV_CHUNK, SC_CH), F32))
        us_ref[r0:r0 + CONV_CHUNK, :] = (zs_ref[0, r0:r0 + CONV_CHUNK, :SC_CH] * acc2).astype(BF16)

    return [prepare] + [functools.partial(chunk, r0) for r0 in range(0, tt, CONV_CHUNK)]


def _t5_bucket_table():
    max_exact = N_BUCKETS // 2
    dist = np.maximum(np.arange(WINDOW)[:, None] + WINDOW - np.arange(2 * WINDOW)[None, :], 0)
    d = np.maximum(dist, 1).astype(np.float32)
    large = max_exact + (np.log(d / np.float32(max_exact)) / np.float32(
        math.log(MAX_DISTANCE / max_exact)) * np.float32(N_BUCKETS - max_exact)).astype(np.int32)
    large = np.minimum(large, N_BUCKETS - 1)
    return np.where(dist < max_exact, dist, large).astype(np.int32)


def _swa_kernel(rb_ref, sink_ref, shift_ref, bucket_ref, q_ref, k_ref, kh_ref, v_ref, vh_ref, o_ref,
                bias_ref, kk_ref, vv_ref, *, online):
    tq = q_ref.shape[1]
    first_step = (pl.program_id(0) == 0) & (pl.program_id(1) == 0)

    @pl.when(first_step)
    def _():
        bucket = bucket_ref[...]
        qi = lax.broadcasted_iota(jnp.int32, bucket.shape, 0) + WINDOW
        ki = lax.broadcasted_iota(jnp.int32, bucket.shape, 1)
        dist = qi - ki
        ok = (dist >= 0) & (dist < WINDOW)
        for h in range(SWA_Q_HEADS):
            bias = jnp.zeros(bucket.shape, F32)
            for bk in range(N_BUCKETS):
                bias = jnp.where(bucket == bk, rb_ref[bk, h], bias)
            bias_ref[h] = jnp.where(ok, bias - shift_ref[h], NEG_INF)

    kk_ref[0:WINDOW, :] = kh_ref[0]
    kk_ref[WINDOW:, :] = k_ref[0]
    vv_ref[0:WINDOW, :] = vh_ref[0]
    vv_ref[WINDOW:, :] = v_ref[0]
    seq_start = pl.program_id(1) == 0
    prev_half = lax.broadcasted_iota(jnp.int32, (WINDOW, 2 * WINDOW), 1) < WINDOW
    first = lax.broadcasted_iota(jnp.int32, (WINDOW, LANES), 1) < HEAD_DIM

    for sb in range(tq // WINDOW):
        r0 = sb * WINDOW
        kk = kk_ref[r0:r0 + 2 * WINDOW, :]
        vv = vv_ref[r0:r0 + 2 * WINDOW, :]
        for blk in range(SWA_Q_HEADS // 2):
            qb = q_ref[0, r0:r0 + WINDOW, blk * LANES:(blk + 1) * LANES]
            outs = []
            for half in range(2):
                h = SWA_Q_ORDER[2 * blk + half]
                keep = first if half == 0 else jnp.logical_not(first)
                q = jnp.where(keep, qb, jnp.zeros_like(qb))
                s = lax.dot_general(q, kk, (((1,), (1,)), ((), ())), preferred_element_type=F32)
                s = s + bias_ref[h]
                if sb == 0:
                    s = jnp.where(seq_start & prev_half, NEG_INF, s)
                sink = jnp.full((1, 1), sink_ref[h] - shift_ref[h], F32)
                if online:
                    m = jnp.maximum(jnp.max(s, axis=-1, keepdims=True), sink)
                    s, sink = s - m, sink - m
                pv = jnp.dot(jnp.exp2(s).astype(BF16), vv, preferred_element_type=F32)
                outs.append(pv[:, :LANES] / (pv[:, LANES:LANES + 1] + jnp.exp2(sink)))
            o_ref[0, r0:r0 + WINDOW, blk * LANES:(blk + 1) * LANES] = jnp.where(
                first, outs[0], outs[1]).astype(BF16)


def _swa(rel_bias2, sink2, shift, qa, ka, va, *, online):
    b, t, _ = qa.shape
    tq = SWA_TILE
    per = tq // WINDOW
    kw, vw = ka.shape[-1], va.shape[-1]
    smem = pl.BlockSpec(memory_space=pltpu.SMEM)

    def cur(width):
        return pl.BlockSpec((1, tq, width), lambda bi, ti: (bi, ti, 0))

    def halo(width):
        return pl.BlockSpec((1, WINDOW, width),
                            lambda bi, ti: (bi, jnp.maximum(ti * per - 1, 0), 0))

    return pl.pallas_call(
        functools.partial(_swa_kernel, online=online),
        grid=(b, t // tq),
        in_specs=[smem, smem, smem, _resident((WINDOW, 2 * WINDOW)),
                  cur(qa.shape[-1]), cur(kw), halo(kw), cur(vw), halo(vw)],
        out_specs=cur(qa.shape[-1]),
        out_shape=jax.ShapeDtypeStruct(qa.shape, BF16),
        scratch_shapes=[pltpu.VMEM((SWA_Q_HEADS, WINDOW, 2 * WINDOW), F32),
                        pltpu.VMEM((WINDOW + tq, kw), BF16),
                        pltpu.VMEM((WINDOW + tq, vw), BF16)],
        compiler_params=_params(2),
        name="swa_online" if online else "swa",
    )(rel_bias2, sink2, shift, jnp.asarray(_t5_bucket_table()), qa, ka, ka, va, va)


def _swa_shift(q_gain, k_gain, rel_bias, sink):
    qk = HEAD_DIM ** 0.5 * jnp.max(jnp.abs(q_gain)) * jnp.max(jnp.abs(k_gain))
    hi = jnp.maximum(qk + jnp.max(rel_bias, axis=0), sink)
    lo = jnp.maximum(rel_bias[0] - qk, sink)
    usable = jnp.all((hi - lo) * LOG2E < SWA_MAX_SPREAD)
    return jnp.where(usable, hi, 0.0), usable


def _fox_kernel(q_ref, k_ref, vt_ref, o_ref, *, online):
    tq = q_ref.shape[1]
    qi = pl.program_id(2)
    qp = q_ref[0]
    lane = lax.broadcasted_iota(jnp.int32, qp.shape, 1)
    zero = jnp.zeros_like(qp)
    x0 = LANES + pl.program_id(1) * (2 * FF_REP)
    q_heads = [
        jnp.where((lane < HEAD_DIM) | ((lane >= x0) & (lane < x0 + FF_REP)), qp, zero),
        jnp.where(((lane >= HEAD_DIM) & (lane < LANES)) |
                  ((lane >= x0 + FF_REP) & (lane < x0 + 2 * FF_REP)), qp, zero)]

    chains = [(hd, c) for c in range(tq // FOX_QC) for hd in range(2)]

    def run(blocks, carry):
        state = dict(enumerate(carry))
        items = []
        for j, diag in blocks:
            k0 = pl.multiple_of(j * FOX_TK, FOX_TK)
            ks = k_ref[0, pl.ds(k0, FOX_TK), :]
            vts = [vt_ref[0, hd * FOX_VROWS:(hd + 1) * FOX_VROWS, pl.ds(k0, FOX_TK)]
                   for hd in range(2)]
            key_lo = 0 if diag is None else diag * FOX_TK
            for ci, (hd, c) in enumerate(chains):
                if diag is None or key_lo <= (c + 1) * FOX_QC - 1:
                    masked = diag is not None and key_lo + FOX_TK - 1 > c * FOX_QC
                    items.append((ci, ks, vts[hd], key_lo if masked else None))

        def scores(item):
            ci, ks, _, mask_lo = item
            hd, c = chains[ci]
            st = lax.dot_general(ks, q_heads[hd][c * FOX_QC:(c + 1) * FOX_QC],
                                 (((1,), (1,)), ((), ())), preferred_element_type=F32)
            if mask_lo is not None:
                key = lax.broadcasted_iota(jnp.int32, st.shape, 0) + mask_lo
                qry = lax.broadcasted_iota(jnp.int32, st.shape, 1) + c * FOX_QC
                st = jnp.where(key <= qry, st, NEG_INF)
            return st

        def softmax(item, st):
            m, acc = state[item[0]]
            if not online:
                return m, acc, jnp.exp2(st).astype(BF16)
            m_new = jnp.maximum(m, jnp.max(st, axis=0, keepdims=True))
            return m_new, jnp.exp2(m - m_new) * acc, jnp.exp2(st - m_new).astype(BF16)

        def weighted(item, part):
            m_new, acc, p = part
            state[item[0]] = (m_new, acc + jnp.dot(item[2], p, preferred_element_type=F32))

        st_next = scores(items[0])
        part_prev = None
        for pos, item in enumerate(items):
            st_cur = st_next
            if pos + 1 < len(items):
                st_next = scores(items[pos + 1])
            part = softmax(item, st_cur)
            if part_prev is not None:
                weighted(items[pos - 1], part_prev)
            part_prev = part
        weighted(items[-1], part_prev)
        return tuple(state[ci] for ci in range(len(chains)))

    init = tuple((jnp.full((1, FOX_QC), NEG_INF, F32), jnp.zeros((FOX_VROWS, FOX_QC), F32))
                 for _ in chains)
    per = tq // FOX_TK
    assert per % FOX_UNROLL == 0
    res = lax.fori_loop(
        0, qi * (per // FOX_UNROLL),
        lambda it, c: run([(it * FOX_UNROLL + u, None) for u in range(FOX_UNROLL)], c), init)
    res = run([(qi * per + d, d) for d in range(per)], res)
    for c in range(tq // FOX_QC):
        heads = [acc[:HEAD_DIM] / acc[HEAD_DIM:HEAD_DIM + 1] for _, acc in res[2 * c:2 * c + 2]]
        out_t = jnp.concatenate(heads, axis=0)
        o_ref[0, c * FOX_QC:(c + 1) * FOX_QC, :] = out_t.T.astype(BF16)


def _fox(qf, kf, vft, *, online):
    b, t, _ = qf.shape
    assert FOX_TQ % FOX_TK == 0 and FOX_TQ % FOX_QC == 0
    return pl.pallas_call(
        functools.partial(_fox_kernel, online=online),
        grid=(b, FOX_PAIRS, t // FOX_TQ),
        in_specs=[pl.BlockSpec((1, FOX_TQ, 2 * LANES), lambda bi, pi, qi: (bi, qi, pi)),
                  pl.BlockSpec((1, t, 2 * LANES), lambda bi, pi, qi: (bi, 0, pi)),
                  pl.BlockSpec((1, 2 * FOX_VROWS, t), lambda bi, pi, qi: (bi, pi, 0))],
        out_specs=pl.BlockSpec((1, FOX_TQ, LANES), lambda bi, pi, qi: (bi, qi, pi)),
        out_shape=jax.ShapeDtypeStruct((b, t, FOX_HEADS * HEAD_DIM), BF16),
        compiler_params=_params(3),
        name="fox_online" if online else "fox",
    )(qf, kf, vft)


def _fox_shift(q_gain, k_gain):
    bound = (LOG2E * HEAD_DIM ** 0.5) * jnp.max(jnp.abs(q_gain)) * jnp.max(jnp.abs(k_gain))
    usable = bound < FOX_MAX_SHIFT
    return jnp.where(usable, bound, 0.0), usable


def _merge_kernel(x_ref, xb_ref, zc_ref, zch_ref, zs_ref, zsh_ref, oa_ref, of_ref,
                  dw_ref, dwb_ref, lng_ref, lnb_ref, scw_ref, wb_ref, wgate_ref, wout_ref,
                  o_ref, uc_ref, us_ref, ext_ref, ext2_ref):
    x = x_ref[0]
    half_inv = 0.5 * lax.rsqrt(jnp.mean(x * x, axis=-1, keepdims=True) + EPS)
    conv = _conv_stages(zc_ref, zch_ref, zs_ref, zsh_ref, dw_ref, dwb_ref, lng_ref, lnb_ref,
                        scw_ref, uc_ref, us_ref, ext_ref, ext2_ref)

    def half_gate(i):
        return lax.dot_general(xb_ref[0], wgate_ref[i * D_MODEL:(i + 1) * D_MODEL, :],
                               (((1,), (1,)), ((), ())), preferred_element_type=F32) * half_inv

    def branch(i, u, hg):
        p = jnp.dot(u, wb_ref[i], preferred_element_type=F32)
        return p + jnp.tanh(hg) * p

    def conv_pieces(n, after):
        for _ in range(min(n, len(conv))):
            conv.pop(0)(after)

    per_slot = -(-(len(conv) - 1) // 4)
    hg2 = half_gate(2)
    conv_pieces(1, hg2)
    b2 = branch(2, oa_ref[0], hg2)
    conv_pieces(per_slot, b2)
    hg3 = half_gate(3)
    conv_pieces(per_slot, hg3)
    b3 = branch(3, of_ref[0], hg3)
    conv_pieces(per_slot, b3)
    hg0 = half_gate(0)
    conv_pieces(len(conv), hg0)
    hg1 = half_gate(1)
    twice_merged = b2 + b3
    twice_merged = twice_merged + branch(0, uc_ref[...], hg0)
    twice_merged = twice_merged + branch(1, us_ref[...], hg1)
    merged = (0.5 * twice_merged).astype(BF16)
    o_ref[0] = x + jnp.dot(merged, wout_ref[...], preferred_element_type=F32)


def _merge(x, xb, layer, zc, zs, oa, of, dw, dwb, lng, lnb, scw, wb, wgate, wout):
    b, t, d = x.shape
    tm = ROW_TILE

    def rows(width):
        return pl.BlockSpec((1, tm, width), lambda bi, ti: (bi, ti, 0))

    def halo(rows_, width):
        per = tm // rows_
        return pl.BlockSpec((1, rows_, width),
                            lambda bi, ti: (bi, jnp.maximum(ti * per - 1, 0), 0))

    def vec(a):
        return a.reshape(a.shape[0], 1, a.shape[1])

    def taps(w):
        return jnp.broadcast_to(w[:, :, None, :], w.shape[:2] + (SUBLANES, w.shape[2]))

    return pl.pallas_call(
        _merge_kernel,
        grid=(b, t // tm),
        in_specs=[rows(d), rows(d), rows(2 * CONV_CH), halo(CONV_HALO, 2 * CONV_CH),
                  rows(3 * SC_CH), halo(SC_HALO, 3 * SC_CH),
                  rows(SWA_Q_HEADS * HEAD_DIM), rows(FOX_HEADS * HEAD_DIM),
                  _resident((CONV_K, SUBLANES, CONV_CH), layer), _resident((1, CONV_CH), layer),
                  _resident((1, CONV_CH), layer), _resident((1, CONV_CH), layer),
                  _resident((SC_K, SUBLANES, SC_CH), layer),
                  _resident(wb.shape[1:], layer), _resident(wgate.shape[1:], layer),
                  _resident(wout.shape[1:], layer)],
        out_specs=rows(d),
        out_shape=jax.ShapeDtypeStruct((b, t, d), F32),
        scratch_shapes=[pltpu.VMEM((tm, CONV_CH), BF16), pltpu.VMEM((tm, SC_CH), BF16),
                        pltpu.VMEM((SUBLANES, CONV_HALO + tm, CONV_CH), F32),
                        pltpu.VMEM((SUBLANES, SC_HALO + tm, SC_CH), F32)],
        compiler_params=_params(2),
        name="merge",
    )(x, xb, zc, zc, zs, zs, oa, of, taps(dw), vec(dwb), vec(lng), vec(lnb), taps(scw),
      wb, wgate, wout)


def _fgate_cols(ff, axis=-1):
    rep = jnp.repeat(ff, FF_REP, axis=axis)
    pad = [(0, 0)] * ff.ndim
    pad[axis] = (0, LANES - FOX_HEADS * FF_REP)
    return jnp.pad(rep, pad)


def _prep_proj(w_in, mix_gain):
    wt = jnp.transpose(w_in, (2, 0, 1))
    aq0 = COL_AQ
    ff0 = COL_FF
    heads = [wt[aq0 + h * HEAD_DIM: aq0 + (h + 1) * HEAD_DIM] for h in SWA_Q_ORDER]
    w_proj = jnp.concatenate([wt[:aq0]] + heads + [wt[COL_AK:ff0],
                                                   _fgate_cols(wt[ff0:ff0 + FOX_HEADS], axis=0)], axis=0)
    w_gate = wt[ff0 + FOX_HEADS:]
    return ((jnp.transpose(w_proj, (1, 0, 2)) * mix_gain[:, None, :]).astype(BF16),
            (jnp.transpose(w_gate, (1, 0, 2)) * mix_gain[:, None, :]).astype(BF16))


def _prep_ffn(norm_gain, w_gate, w_up, w_down):
    g = norm_gain[:, :, None]
    return (g * w_gate).astype(BF16), (g * w_up).astype(BF16), w_down.astype(BF16)


def _prep_qk_gains(aq, ak, fq, fk):
    scale = HEAD_DIM ** -0.5 * LOG2E
    ones = jnp.ones((aq.shape[0], SWA_KV_HEADS * HEAD_DIM), F32)
    row = jnp.concatenate([jnp.tile(aq * scale, (1, SWA_Q_HEADS)), jnp.tile(ak, (1, SWA_KV_HEADS)),
                           ones, jnp.tile(fq * scale, (1, FOX_HEADS)),
                           jnp.tile(fk, (1, FOX_HEADS))],
                          axis=-1)
    return row[:, None, :]


def kernel(x, rel_bias, ffn1_norm, ffn1_w_gate, ffn1_w_up, ffn1_w_down, mix_norm, w_in, b_forget, conf_dw, conf_dw_b, conf_ln_g, conf_ln_b, conf_w_out, sc_conv, sc_w_out, swa_q_norm, swa_k_norm, swa_sink, swa_w_o, fox_q_norm, fox_k_norm, fox_w_o, w_out, ffn2_norm, ffn2_w_gate, ffn2_w_up, ffn2_w_down):
    b, t, d = x.shape
    depth = w_in.shape[0]

    w_proj, w_gate = _prep_proj(w_in, mix_norm)
    qk_gains = _prep_qk_gains(swa_q_norm, swa_k_norm, fox_q_norm, fox_k_norm)
    bf = _fgate_cols(b_forget)[:, None, :]
    swa_rows = jnp.concatenate([swa_w_o[:, h * HEAD_DIM:(h + 1) * HEAD_DIM] for h in SWA_Q_ORDER],
                               axis=1)
    w_branch = jnp.stack([conf_w_out, sc_w_out, swa_rows, fox_w_o], axis=1).astype(BF16)
    w_out_b = w_out.astype(BF16)
    ffn1 = _prep_ffn(ffn1_norm, ffn1_w_gate, ffn1_w_up, ffn1_w_down)
    ffn2 = _prep_ffn(ffn2_norm, ffn2_w_gate, ffn2_w_up, ffn2_w_down)

    for l in range(depth):
        x = _ffn(x.reshape(b * t, d), l, *ffn1).reshape(b, t, d)
        shift, shifted = _fox_shift(fox_q_norm[l], fox_k_norm[l])
        xb, zc, zs, qa, ka, va, qf, kf, vft = _inproj(x, l, w_proj, qk_gains, bf, shift)
        a_shift, a_shifted = _swa_shift(swa_q_norm[l], swa_k_norm[l], rel_bias, swa_sink[l])
        oa = lax.cond(a_shifted, functools.partial(_swa, online=False),
                      functools.partial(_swa, online=True),
                      rel_bias * LOG2E, swa_sink[l] * LOG2E, a_shift * LOG2E, qa, ka, va)
        of = lax.cond(shifted, functools.partial(_fox, online=False),
                      functools.partial(_fox, online=True), qf, kf, vft)
        x = _merge(x, xb, l, zc, zs, oa, of, conf_dw, conf_dw_b, conf_ln_g, conf_ln_b, sc_conv,
                   w_branch, w_gate, w_out_b)
        x = _ffn(x.reshape(b * t, d), l, *ffn2).reshape(b, t, d)
    return x
```

```python
import functools
import math

import numpy as np
import jax
import jax.numpy as jnp
from jax import lax
from jax.experimental import pallas as pl
from jax.experimental.pallas import tpu as pltpu

F32 = jnp.float32
BF16 = jnp.bfloat16

D_MODEL = 1024
HEAD_DIM = 64
CONV_CH = 256
CONV_K = 31
SC_CH = 256
SC_K = 3
SWA_Q_HEADS = 4
SWA_KV_HEADS = 2
WINDOW = 128
FOX_HEADS = 4
N_BUCKETS = 32
MAX_DISTANCE = 128
D_FF = 2816
N_BRANCH = 4
EPS = 1e-6
NEG_INF = -1e30
LOG2E = math.log2(math.e)

LANES = 128
SUBLANES = 8
VMEM_LIMIT = 56 * 1024 * 1024

COL_CONV = 0
COL_SC = COL_CONV + 2 * CONV_CH
COL_AQ = COL_SC + 3 * SC_CH
COL_AK = COL_AQ + SWA_Q_HEADS * HEAD_DIM
COL_AV = COL_AK + SWA_KV_HEADS * HEAD_DIM
COL_FQ = COL_AV + SWA_KV_HEADS * HEAD_DIM
COL_FK = COL_FQ + FOX_HEADS * HEAD_DIM
COL_FV = COL_FK + FOX_HEADS * HEAD_DIM
COL_FF = COL_FV + FOX_HEADS * HEAD_DIM
FOX_PAIRS = FOX_HEADS // 2
FF_REP = 9
PROJ_COLS = COL_FF + LANES
SWA_Q_ORDER = (0, 2, 1, 3)

ROW_TILE = 512
FFN_STAGE_CHUNKS = 8
CONV_HALO = 32
SC_HALO = 8
CONV_CHUNK = 64
SWA_TILE = 512
FOX_TQ = 2048
FOX_TK = 512
FOX_QC = 256
FOX_VROWS = 80
FOX_UNROLL = 4
SWA_MAX_SPREAD = 100.0
FOX_MAX_SHIFT = 56.0


def _params(n_axes, flags=None):
    return pltpu.CompilerParams(dimension_semantics=("arbitrary",) * n_axes,
                                vmem_limit_bytes=VMEM_LIMIT, flags=flags)


def _resident(shape, layer=None):
    if layer is None:
        return pl.BlockSpec(shape, lambda *_: (0,) * len(shape), pipeline_mode=pl.Buffered(1))
    return pl.BlockSpec((None,) + tuple(shape), lambda *_: (layer,) + (0,) * len(shape),
                        pipeline_mode=pl.Buffered(1))


def _sigmoid(x):
    return 1.0 / (1.0 + jnp.exp2(x * -LOG2E))


def _cast_weights(src_ref, layer, dst_ref, stage_ref, sem):
    rows = stage_ref.shape[1]
    n = src_ref.shape[1] // rows

    def copy(c):
        return pltpu.make_async_copy(src_ref.at[layer, pl.ds(c * rows, rows), :],
                                     stage_ref.at[c % 2], sem.at[c % 2])

    copy(0).start()
    for c in range(n):
        if c + 1 < n:
            copy(c + 1).start()
        copy(c).wait()
        dst_ref[c * rows:(c + 1) * rows, :] = stage_ref[c % 2].astype(BF16)


def _ffn_kernel(x_ref, g_ref, wg_hbm, wu_hbm, wd_hbm, o_ref,
                wg_ref, wu_ref, wd_ref, stage_in_ref, stage_dn_ref, sem_in, sem_dn, *, layer):
    @pl.when(pl.program_id(0) == 0)
    def _():
        _cast_weights(wg_hbm, layer, wg_ref, stage_in_ref, sem_in)
        _cast_weights(wu_hbm, layer, wu_ref, stage_in_ref, sem_in)
        _cast_weights(wd_hbm, layer, wd_ref, stage_dn_ref, sem_dn)

    x = x_ref[...]
    inv = lax.rsqrt(jnp.mean(x * x, axis=-1, keepdims=True) + EPS)
    xb = (x * g_ref[...]).astype(BF16)
    half_gate = jnp.dot(xb, wg_ref[...], preferred_element_type=F32) * (0.5 * inv)
    up = jnp.dot(xb, wu_ref[...], preferred_element_type=F32)
    act = ((half_gate + half_gate * jnp.tanh(half_gate)) * up).astype(BF16)
    y = jnp.dot(act, wd_ref[...], preferred_element_type=F32)
    o_ref[...] = x + (0.5 * inv) * y


def _ffn(x2d, layer, g, wg, wu, wd):
    n, d = x2d.shape
    d_ff = wg.shape[2]
    row = pl.BlockSpec((ROW_TILE, d), lambda i: (i, 0))
    hbm = pl.BlockSpec(memory_space=pl.ANY)
    return pl.pallas_call(
        functools.partial(_ffn_kernel, layer=layer),
        grid=(n // ROW_TILE,),
        in_specs=[row, _resident((1, d), layer), hbm, hbm, hbm],
        out_specs=row,
        out_shape=jax.ShapeDtypeStruct((n, d), F32),
        scratch_shapes=[pltpu.VMEM((d, d_ff), BF16), pltpu.VMEM((d, d_ff), BF16),
                        pltpu.VMEM((d_ff, d), BF16),
                        pltpu.VMEM((2, d // FFN_STAGE_CHUNKS, d_ff), F32),
                        pltpu.VMEM((2, d_ff // FFN_STAGE_CHUNKS, d), F32),
                        pltpu.SemaphoreType.DMA((2,)), pltpu.SemaphoreType.DMA((2,))],
        compiler_params=_params(1),
        name="ffn",
    )(x2d, g.reshape(g.shape[0], 1, d), wg, wu, wd)


def _split3(v):
    hi = v.astype(BF16).astype(F32)
    r = v - hi
    mid = r.astype(BF16).astype(F32)
    return hi, mid, r - mid


def _pair_rms(zb, first, eps):
    sq = zb * zb
    s0 = jnp.sum(jnp.where(first, sq, 0.0), axis=-1, keepdims=True)
    s1 = jnp.sum(jnp.where(first, 0.0, sq), axis=-1, keepdims=True)
    inv = jnp.where(first, lax.rsqrt(s0 * (1.0 / HEAD_DIM) + eps),
                    lax.rsqrt(s1 * (1.0 / HEAD_DIM) + eps))
    return zb * inv


def _inproj_kernel(x_ref, w_ref, qkg_ref, bf_ref, shift_ref,
                   zc_ref, zs_ref, qa_ref, ka_ref, va_ref, qf_ref, kf_ref, vft_ref, carry_ref):
    tm = x_ref.shape[1]

    @pl.when(pl.program_id(1) == 0)
    def _():
        carry_ref[...] = jnp.zeros_like(carry_ref)

    x = x_ref[0]
    ms = jnp.mean(x * x, axis=-1, keepdims=True) + EPS
    inv = lax.rsqrt(ms)
    qk_eps = EPS * ms
    h = x.astype(BF16)
    lane = lax.broadcasted_iota(jnp.int32, (tm, LANES), 1)
    first = lane < HEAD_DIM
    slot = lane % FF_REP
    used = lane < FOX_HEADS * FF_REP
    ext = {}

    def normed(z, col0, i):
        g0 = col0 - COL_AQ + i * LANES
        return _pair_rms(z[:, i * LANES:(i + 1) * LANES], first, qk_eps) * qkg_ref[:, g0:g0 + LANES]

    def forget_terms(z):
        ff = z * inv + bf_ref[...]
        logf = jnp.minimum(ff, 0.0) - jnp.log1p(jnp.exp(-jnp.abs(ff)))
        rows = lax.broadcasted_iota(jnp.int32, logf.shape, 0)
        cum = logf
        step = 1
        while step < tm:
            cum = cum + jnp.where(rows >= step, pltpu.roll(cum, step, axis=0), 0.0)
            step *= 2
        cum = cum + carry_ref[...]
        carry_ref[...] = cum[tm - 1:tm, :]
        cum = cum * LOG2E
        s_hi, s_mid, s_lo = _split3(shift_ref[...])
        hi, mid, lo = _split3(cum)
        q_extra = jnp.where(slot == 0, hi, jnp.where(slot == 1, mid, jnp.where(
            slot == 2, lo, jnp.where(slot < 6, 1.0, jnp.where(
                slot == 6, -s_hi, jnp.where(slot == 7, -s_mid, -s_lo))))))
        k_extra = jnp.where(slot == 3, -hi, jnp.where(slot == 4, -mid, jnp.where(
            slot == 5, -lo, 1.0)))
        ext["q"] = jnp.where(used, q_extra, 0.0).astype(BF16)
        ext["k"] = jnp.where(used, k_extra, 0.0).astype(BF16)

    def fox_qk(z):
        for p in range(FOX_PAIRS):
            base = 2 * p * LANES
            qf_ref[0, :, base:base + LANES] = normed(z, COL_FQ, p).astype(BF16)
            qf_ref[0, :, base + LANES:base + 2 * LANES] = ext["q"]
            kf_ref[0, :, base:base + LANES] = normed(z, COL_FQ, FOX_PAIRS + p).astype(BF16)
            kf_ref[0, :, base + LANES:base + 2 * LANES] = ext["k"]

    def swa_qk(z):
        n_q = SWA_Q_HEADS // 2
        for i in range(n_q):
            qa_ref[0, :, i * LANES:(i + 1) * LANES] = normed(z, COL_AQ, i).astype(BF16)
        ka_ref[0] = normed(z, COL_AQ, n_q).astype(BF16)

    def fox_v(z):
        vt = (z * inv).T
        ones_row = (lax.broadcasted_iota(jnp.int32, (FOX_VROWS - HEAD_DIM, tm), 0) == 0).astype(F32)
        for i in range(FOX_HEADS):
            vft_ref[0, i * FOX_VROWS:(i + 1) * FOX_VROWS, :] = jnp.concatenate(
                [vt[i * HEAD_DIM:(i + 1) * HEAD_DIM], ones_row], axis=0).astype(BF16)

    def swa_v(z):
        va_ref[0, :, 0:LANES] = (z * inv).astype(BF16)
        va_ref[0, :, LANES:2 * LANES] = (lane == 0).astype(BF16)

    def conv_in(z):
        zc_ref[0] = z * inv

    def sc_in(z):
        zs_ref[0] = z * inv

    z = lax.dot_general(h, w_ref[...], (((1,), (1,)), ((), ())),
                        preferred_element_type=F32)
    for c0, c1, epilogue in ((COL_FF, PROJ_COLS, forget_terms), (COL_FQ, COL_FV, fox_qk),
                             (COL_AQ, COL_AV, swa_qk), (COL_FV, COL_FF, fox_v),
                             (COL_AV, COL_FQ, swa_v), (COL_CONV, COL_SC, conv_in),
                             (COL_SC, COL_AQ, sc_in)):
        epilogue(z[:, c0:c1])


def _inproj(x, layer, w, qkg, bf, shift):
    b, t, d = x.shape
    tm = ROW_TILE

    def rows(width):
        return pl.BlockSpec((1, tm, width), lambda bi, ti: (bi, ti, 0))

    def oshape(width, dtype=BF16):
        return jax.ShapeDtypeStruct((b, t, width), dtype)

    n_v = FOX_HEADS * FOX_VROWS
    return pl.pallas_call(
        _inproj_kernel,
        grid=(b, t // tm),
        in_specs=[rows(d), _resident(w.shape[1:], layer),
                  _resident(qkg.shape[1:], layer), _resident(bf.shape[1:], layer),
                  _resident((1, LANES))],
        out_specs=[rows(2 * CONV_CH), rows(3 * SC_CH),
                   rows(SWA_Q_HEADS * HEAD_DIM), rows(SWA_KV_HEADS * HEAD_DIM),
                   rows(2 * LANES),
                   rows(2 * FOX_PAIRS * LANES), rows(2 * FOX_PAIRS * LANES),
                   pl.BlockSpec((1, n_v, tm), lambda bi, ti: (bi, 0, ti))],
        out_shape=[oshape(2 * CONV_CH, F32), oshape(3 * SC_CH, F32),
                   oshape(SWA_Q_HEADS * HEAD_DIM), oshape(SWA_KV_HEADS * HEAD_DIM),
                   oshape(2 * LANES),
                   oshape(2 * FOX_PAIRS * LANES), oshape(2 * FOX_PAIRS * LANES),
                   jax.ShapeDtypeStruct((b, n_v, t), BF16)],
        scratch_shapes=[pltpu.VMEM((1, LANES), F32)],
        compiler_params=_params(2),
        name="inproj",
    )(x, w, qkg, bf, jnp.full((1, LANES), shift, F32))


def _fill_shifted(ext_ref, base, n_taps):
    n = ext_ref.shape[1] - SUBLANES
    for r in sorted({(base + k) % SUBLANES for k in range(n_taps)} - {0}):
        ext_ref[r, 0:n, :] = ext_ref[0, r:r + n, :]


def _taps(ext_ref, w_ref, r0, base, n_taps, acc):
    groups = CONV_CHUNK // SUBLANES
    acc = acc.reshape(groups, SUBLANES, acc.shape[-1])
    for k in range(n_taps):
        r = (base + k) % SUBLANES
        start = r0 + base + k - r
        u = ext_ref[r, start:start + CONV_CHUNK, :]
        acc = acc + w_ref[k][None] * u.reshape(groups, SUBLANES, u.shape[-1])
    return acc.reshape(CONV_CHUNK, acc.shape[-1])


def _conv_kernel(zc_ref, zch_ref, zs_ref, zsh_ref, dw_ref, dwb_ref, lng_ref, lnb_ref, scw_ref,
                 uc_ref, us_ref, ext_ref, ext2_ref):
    tt = zc_ref.shape[1]
    keep = (pl.program_id(1) > 0).astype(F32)

    def glu(v):
        return v[:, :CONV_CH] * _sigmoid(v[:, CONV_CH:])

    ext_ref[0, 0:CONV_HALO, :] = glu(zch_ref[0]) * keep
    ext_ref[0, CONV_HALO:, :] = glu(zc_ref[0])
    zsh = zsh_ref[0]
    ext2_ref[0, 0:SC_HALO, :] = zsh[:, SC_CH:2 * SC_CH] * zsh[:, 2 * SC_CH:] * keep
    zs = zs_ref[0]
    ext2_ref[0, SC_HALO:, :] = zs[:, SC_CH:2 * SC_CH] * zs[:, 2 * SC_CH:]
    conv_base = CONV_HALO - (CONV_K - 1)
    sc_base = SC_HALO - (SC_K - 1)
    _fill_shifted(ext_ref, conv_base, CONV_K)
    _fill_shifted(ext2_ref, sc_base, SC_K)

    for r0 in range(0, tt, CONV_CHUNK):
        acc = _taps(ext_ref, dw_ref, r0, conv_base, CONV_K,
                    jnp.broadcast_to(dwb_ref[...], (CONV_CHUNK, CONV_CH)))
        mu = jnp.mean(acc, axis=-1, keepdims=True)
        cen = acc - mu
        var = jnp.mean(cen * cen, axis=-1, keepdims=True)
        y = cen * lax.rsqrt(var + EPS) * lng_ref[...] + lnb_ref[...]
        uc_ref[0, r0:r0 + CONV_CHUNK, :] = (y * _sigmoid(y)).astype(BF16)

        acc2 = _taps(ext2_ref, scw_ref, r0, sc_base, SC_K, jnp.zeros((CONV_CHUNK, SC_CH), F32))
        us_ref[0, r0:r0 + CONV_CHUNK, :] = (zs[r0:r0 + CONV_CHUNK, :SC_CH] * acc2).astype(BF16)


def _conv(zc, zs, layer, dw, dwb, lng, lnb, scw):
    b, t, _ = zc.shape
    tt = ROW_TILE

    def rows(width):
        return pl.BlockSpec((1, tt, width), lambda bi, ti: (bi, ti, 0))

    def halo(rows_, width):
        per = tt // rows_
        return pl.BlockSpec((1, rows_, width),
                            lambda bi, ti: (bi, jnp.maximum(ti * per - 1, 0), 0))

    def vec(a):
        return a.reshape(a.shape[0], 1, a.shape[1])

    def taps(w):
        return jnp.broadcast_to(w[:, :, None, :], w.shape[:2] + (SUBLANES, w.shape[2]))

    return pl.pallas_call(
        _conv_kernel,
        grid=(b, t // tt),
        in_specs=[rows(2 * CONV_CH), halo(CONV_HALO, 2 * CONV_CH),
                  rows(3 * SC_CH), halo(SC_HALO, 3 * SC_CH),
                  _resident((CONV_K, SUBLANES, CONV_CH), layer), _resident((1, CONV_CH), layer),
                  _resident((1, CONV_CH), layer), _resident((1, CONV_CH), layer),
                  _resident((SC_K, SUBLANES, SC_CH), layer)],
        out_specs=[rows(CONV_CH), rows(SC_CH)],
        out_shape=[jax.ShapeDtypeStruct((b, t, CONV_CH), BF16),
                   jax.ShapeDtypeStruct((b, t, SC_CH), BF16)],
        scratch_shapes=[pltpu.VMEM((SUBLANES, CONV_HALO + tt, CONV_CH), F32),
                        pltpu.VMEM((SUBLANES, SC_HALO + tt, SC_CH), F32)],
        compiler_params=_params(2),
        name="conv",
    )(zc, zc, zs, zs, taps(dw), vec(dwb), vec(lng), vec(lnb), taps(scw))


def _t5_bucket_table():
    max_exact = N_BUCKETS // 2
    dist = np.maximum(np.arange(WINDOW)[:, None] + WINDOW - np.arange(2 * WINDOW)[None, :], 0)
    d = np.maximum(dist, 1).astype(np.float32)
    large = max_exact + (np.log(d / np.float32(max_exact)) / np.float32(
        math.log(MAX_DISTANCE / max_exact)) * np.float32(N_BUCKETS - max_exact)).astype(np.int32)
    large = np.minimum(large, N_BUCKETS - 1)
    return np.where(dist < max_exact, dist, large).astype(np.int32)


def _swa_kernel(rb_ref, sink_ref, shift_ref, bucket_ref, q_ref, k_ref, kh_ref, v_ref, vh_ref, o_ref,
                bias_ref, kk_ref, vv_ref, *, online):
    tq = q_ref.shape[1]
    first_step = (pl.program_id(0) == 0) & (pl.program_id(1) == 0)

    @pl.when(first_step)
    def _():
        bucket = bucket_ref[...]
        qi = lax.broadcasted_iota(jnp.int32, bucket.shape, 0) + WINDOW
        ki = lax.broadcasted_iota(jnp.int32, bucket.shape, 1)
        dist = qi - ki
        ok = (dist >= 0) & (dist < WINDOW)
        for h in range(SWA_Q_HEADS):
            bias = jnp.zeros(bucket.shape, F32)
            for bk in range(N_BUCKETS):
                bias = jnp.where(bucket == bk, rb_ref[bk, h], bias)
            bias_ref[h] = jnp.where(ok, bias - shift_ref[h], NEG_INF)

    kk_ref[0:WINDOW, :] = kh_ref[0]
    kk_ref[WINDOW:, :] = k_ref[0]
    vv_ref[0:WINDOW, :] = vh_ref[0]
    vv_ref[WINDOW:, :] = v_ref[0]
    seq_start = pl.program_id(1) == 0
    prev_half = lax.broadcasted_iota(jnp.int32, (WINDOW, 2 * WINDOW), 1) < WINDOW
    first = lax.broadcasted_iota(jnp.int32, (WINDOW, LANES), 1) < HEAD_DIM

    for sb in range(tq // WINDOW):
        r0 = sb * WINDOW
        kk = kk_ref[r0:r0 + 2 * WINDOW, :]
        vv = vv_ref[r0:r0 + 2 * WINDOW, :]
        for blk in range(SWA_Q_HEADS // 2):
            qb = q_ref[0, r0:r0 + WINDOW, blk * LANES:(blk + 1) * LANES]
            outs = []
            for half in range(2):
                h = SWA_Q_ORDER[2 * blk + half]
                keep = first if half == 0 else jnp.logical_not(first)
                q = jnp.where(keep, qb, jnp.zeros_like(qb))
                s = lax.dot_general(q, kk, (((1,), (1,)), ((), ())), preferred_element_type=F32)
                s = s + bias_ref[h]
                if sb == 0:
                    s = jnp.where(seq_start & prev_half, NEG_INF, s)
                sink = jnp.full((1, 1), sink_ref[h] - shift_ref[h], F32)
                if online:
                    m = jnp.maximum(jnp.max(s, axis=-1, keepdims=True), sink)
                    s, sink = s - m, sink - m
                pv = jnp.dot(jnp.exp2(s).astype(BF16), vv, preferred_element_type=F32)
                outs.append(pv[:, :LANES] / (pv[:, LANES:LANES + 1] + jnp.exp2(sink)))
            o_ref[0, r0:r0 + WINDOW, blk * LANES:(blk + 1) * LANES] = jnp.where(
                first, outs[0], outs[1]).astype(BF16)


def _swa(rel_bias2, sink2, shift, qa, ka, va, *, online):
    b, t, _ = qa.shape
    tq = SWA_TILE
    per = tq // WINDOW
    kw, vw = ka.shape[-1], va.shape[-1]
    smem = pl.BlockSpec(memory_space=pltpu.SMEM)

    def cur(width):
        return pl.BlockSpec((1, tq, width), lambda bi, ti: (bi, ti, 0))

    def halo(width):
        return pl.BlockSpec((1, WINDOW, width),
                            lambda bi, ti: (bi, jnp.maximum(ti * per - 1, 0), 0))

    return pl.pallas_call(
        functools.partial(_swa_kernel, online=online),
        grid=(b, t // tq),
        in_specs=[smem, smem, smem, _resident((WINDOW, 2 * WINDOW)),
                  cur(qa.shape[-1]), cur(kw), halo(kw), cur(vw), halo(vw)],
        out_specs=cur(qa.shape[-1]),
        out_shape=jax.ShapeDtypeStruct(qa.shape, BF16),
        scratch_shapes=[pltpu.VMEM((SWA_Q_HEADS, WINDOW, 2 * WINDOW), F32),
                        pltpu.VMEM((WINDOW + tq, kw), BF16),
                        pltpu.VMEM((WINDOW + tq, vw), BF16)],
        compiler_params=_params(2),
        name="swa_online" if online else "swa",
    )(rel_bias2, sink2, shift, jnp.asarray(_t5_bucket_table()), qa, ka, ka, va, va)


def _swa_shift(q_gain, k_gain, rel_bias, sink):
    qk = HEAD_DIM ** 0.5 * jnp.max(jnp.abs(q_gain)) * jnp.max(jnp.abs(k_gain))
    hi = jnp.maximum(qk + jnp.max(rel_bias, axis=0), sink)
    lo = jnp.maximum(rel_bias[0] - qk, sink)
    usable = jnp.all((hi - lo) * LOG2E < SWA_MAX_SPREAD)
    return jnp.where(usable, hi, 0.0), usable


def _fox_kernel(q_ref, k_ref, vt_ref, o_ref, *, online):
    tq = q_ref.shape[1]
    qi = pl.program_id(2)
    qp = q_ref[0]
    lane = lax.broadcasted_iota(jnp.int32, qp.shape, 1)
    zero = jnp.zeros_like(qp)
    x0 = LANES + pl.program_id(1) * (2 * FF_REP)
    q_heads = [
        jnp.where((lane < HEAD_DIM) | ((lane >= x0) & (lane < x0 + FF_REP)), qp, zero),
        jnp.where(((lane >= HEAD_DIM) & (lane < LANES)) |
                  ((lane >= x0 + FF_REP) & (lane < x0 + 2 * FF_REP)), qp, zero)]

    chains = [(hd, c) for c in range(tq // FOX_QC) for hd in range(2)]

    def run(blocks, carry):
        state = dict(enumerate(carry))
        items = []
        for j, diag in blocks:
            k0 = pl.multiple_of(j * FOX_TK, FOX_TK)
            ks = k_ref[0, pl.ds(k0, FOX_TK), :]
            vts = [vt_ref[0, hd * FOX_VROWS:(hd + 1) * FOX_VROWS, pl.ds(k0, FOX_TK)]
                   for hd in range(2)]
            key_lo = 0 if diag is None else diag * FOX_TK
            for ci, (hd, c) in enumerate(chains):
                if diag is None or key_lo <= (c + 1) * FOX_QC - 1:
                    masked = diag is not None and key_lo + FOX_TK - 1 > c * FOX_QC
                    items.append((ci, ks, vts[hd], key_lo if masked else None))

        def scores(item):
            ci, ks, _, mask_lo = item
            hd, c = chains[ci]
            st = lax.dot_general(ks, q_heads[hd][c * FOX_QC:(c + 1) * FOX_QC],
                                 (((1,), (1,)), ((), ())), preferred_element_type=F32)
            if mask_lo is not None:
                key = lax.broadcasted_iota(jnp.int32, st.shape, 0) + mask_lo
                qry = lax.broadcasted_iota(jnp.int32, st.shape, 1) + c * FOX_QC
                st = jnp.where(key <= qry, st, NEG_INF)
            return st

        def softmax(item, st):
            m, acc = state[item[0]]
            if not online:
                return m, acc, jnp.exp2(st).astype(BF16)
            m_new = jnp.maximum(m, jnp.max(st, axis=0, keepdims=True))
            return m_new, jnp.exp2(m - m_new) * acc, jnp.exp2(st - m_new).astype(BF16)

        def weighted(item, part):
            m_new, acc, p = part
            state[item[0]] = (m_new, acc + jnp.dot(item[2], p, preferred_element_type=F32))

        st_next = scores(items[0])
        part_prev = None
        for pos, item in enumerate(items):
            st_cur = st_next
            if pos + 1 < len(items):
                st_next = scores(items[pos + 1])
            part = softmax(item, st_cur)
            if part_prev is not None:
                weighted(items[pos - 1], part_prev)
            part_prev = part
        weighted(items[-1], part_prev)
        return tuple(state[ci] for ci in range(len(chains)))

    init = tuple((jnp.full((1, FOX_QC), NEG_INF, F32), jnp.zeros((FOX_VROWS, FOX_QC), F32))
                 for _ in chains)
    per = tq // FOX_TK
    assert per % FOX_UNROLL == 0
    res = lax.fori_loop(
        0, qi * (per // FOX_UNROLL),
        lambda it, c: run([(it * FOX_UNROLL + u, None) for u in range(FOX_UNROLL)], c), init)
    res = run([(qi * per + d, d) for d in range(per)], res)
    for c in range(tq // FOX_QC):
        heads = [acc[:HEAD_DIM] / acc[HEAD_DIM:HEAD_DIM + 1] for _, acc in res[2 * c:2 * c + 2]]
        out_t = jnp.concatenate(heads, axis=0)
        o_ref[0, c * FOX_QC:(c + 1) * FOX_QC, :] = out_t.T.astype(BF16)


def _fox(qf, kf, vft, *, online):
    b, t, _ = qf.shape
    assert FOX_TQ % FOX_TK == 0 and FOX_TQ % FOX_QC == 0
    return pl.pallas_call(
        functools.partial(_fox_kernel, online=online),
        grid=(b, FOX_PAIRS, t // FOX_TQ),
        in_specs=[pl.BlockSpec((1, FOX_TQ, 2 * LANES), lambda bi, pi, qi: (bi, qi, pi)),
                  pl.BlockSpec((1, t, 2 * LANES), lambda bi, pi, qi: (bi, 0, pi)),
                  pl.BlockSpec((1, 2 * FOX_VROWS, t), lambda bi, pi, qi: (bi, pi, 0))],
        out_specs=pl.BlockSpec((1, FOX_TQ, LANES), lambda bi, pi, qi: (bi, qi, pi)),
        out_shape=jax.ShapeDtypeStruct((b, t, FOX_HEADS * HEAD_DIM), BF16),
        compiler_params=_params(3),
        name="fox_online" if online else "fox",
    )(qf, kf, vft)


def _fox_shift(q_gain, k_gain):
    bound = (LOG2E * HEAD_DIM ** 0.5) * jnp.max(jnp.abs(q_gain)) * jnp.max(jnp.abs(k_gain))
    usable = bound < FOX_MAX_SHIFT
    return jnp.where(usable, bound, 0.0), usable


def _merge_kernel(x_ref, uc_ref, us_ref, oa_ref, of_ref, wb_ref, wgate_ref, wout_ref, o_ref):
    x = x_ref[0]
    half_inv = 0.5 * lax.rsqrt(jnp.mean(x * x, axis=-1, keepdims=True) + EPS)
    xb = x.astype(BF16)
    twice_merged = jnp.zeros(x.shape, F32)
    for i, u_ref in enumerate((uc_ref, us_ref, oa_ref, of_ref)):
        p = jnp.dot(u_ref[0], wb_ref[i], preferred_element_type=F32)
        half_gate = lax.dot_general(xb, wgate_ref[i * D_MODEL:(i + 1) * D_MODEL, :],
                                    (((1,), (1,)), ((), ())), preferred_element_type=F32) * half_inv
        twice_merged = twice_merged + (p + jnp.tanh(half_gate) * p)
    merged = (0.5 * twice_merged).astype(BF16)
    o_ref[0] = x + jnp.dot(merged, wout_ref[...], preferred_element_type=F32)


def _merge(x, layer, uc, us, oa, of, wb, wgate, wout):
    b, t, d = x.shape
    tm = ROW_TILE

    def rows(width):
        return pl.BlockSpec((1, tm, width), lambda bi, ti: (bi, ti, 0))

    return pl.pallas_call(
        _merge_kernel,
        grid=(b, t // tm),
        in_specs=[rows(d), rows(CONV_CH), rows(SC_CH),
                  rows(SWA_Q_HEADS * HEAD_DIM), rows(FOX_HEADS * HEAD_DIM),
                  _resident(wb.shape[1:], layer), _resident(wgate.shape[1:], layer),
                  _resident(wout.shape[1:], layer)],
        out_specs=rows(d),
        out_shape=jax.ShapeDtypeStruct((b, t, d), F32),
        compiler_params=_params(2),
        name="merge",
    )(x, uc, us, oa, of, wb, wgate, wout)


def _fgate_cols(ff, axis=-1):
    rep = jnp.repeat(ff, FF_REP, axis=axis)
    pad = [(0, 0)] * ff.ndim
    pad[axis] = (0, LANES - FOX_HEADS * FF_REP)
    return jnp.pad(rep, pad)


def _prep_proj(w_in, mix_gain):
    wt = jnp.transpose(w_in, (2, 0, 1))
    aq0 = COL_AQ
    ff0 = COL_FF
    heads = [wt[aq0 + h * HEAD_DIM: aq0 + (h + 1) * HEAD_DIM] for h in SWA_Q_ORDER]
    w_proj = jnp.concatenate([wt[:aq0]] + heads + [wt[COL_AK:ff0],
                                                   _fgate_cols(wt[ff0:ff0 + FOX_HEADS], axis=0)], axis=0)
    w_gate = wt[ff0 + FOX_HEADS:]
    return ((jnp.transpose(w_proj, (1, 0, 2)) * mix_gain[:, None, :]).astype(BF16),
            (jnp.transpose(w_gate, (1, 0, 2)) * mix_gain[:, None, :]).astype(BF16))


def _prep_qk_gains(aq, ak, fq, fk):
    scale = HEAD_DIM ** -0.5 * LOG2E
    ones = jnp.ones((aq.shape[0], SWA_KV_HEADS * HEAD_DIM), F32)
    row = jnp.concatenate([jnp.tile(aq * scale, (1, SWA_Q_HEADS)), jnp.tile(ak, (1, SWA_KV_HEADS)),
                           ones, jnp.tile(fq * scale, (1, FOX_HEADS)),
                           jnp.tile(fk, (1, FOX_HEADS))],
                          axis=-1)
    return row[:, None, :]


def kernel(x, rel_bias, ffn1_norm, ffn1_w_gate, ffn1_w_up, ffn1_w_down, mix_norm, w_in, b_forget, conf_dw, conf_dw_b, conf_ln_g, conf_ln_b, conf_w_out, sc_conv, sc_w_out, swa_q_norm, swa_k_norm, swa_sink, swa_w_o, fox_q_norm, fox_k_norm, fox_w_o, w_out, ffn2_norm, ffn2_w_gate, ffn2_w_up, ffn2_w_down):
    b, t, d = x.shape
    depth = w_in.shape[0]

    w_proj, w_gate = _prep_proj(w_in, mix_norm)
    qk_gains = _prep_qk_gains(swa_q_norm, swa_k_norm, fox_q_norm, fox_k_norm)
    bf = _fgate_cols(b_forget)[:, None, :]
    swa_rows = jnp.concatenate([swa_w_o[:, h * HEAD_DIM:(h + 1) * HEAD_DIM] for h in SWA_Q_ORDER],
                               axis=1)
    w_branch = jnp.stack([conf_w_out, sc_w_out, swa_rows, fox_w_o], axis=1).astype(BF16)
    w_out_b = w_out.astype(BF16)
    ffn1 = (ffn1_norm, ffn1_w_gate, ffn1_w_up, ffn1_w_down)
    ffn2 = (ffn2_norm, ffn2_w_gate, ffn2_w_up, ffn2_w_down)

    for l in range(depth):
        x = _ffn(x.reshape(b * t, d), l, *ffn1).reshape(b, t, d)
        shift, shifted = _fox_shift(fox_q_norm[l], fox_k_norm[l])
        zc, zs, qa, ka, va, qf, kf, vft = _inproj(x, l, w_proj, qk_gains, bf, shift)
        uc, us = _conv(zc, zs, l, conf_dw, conf_dw_b, conf_ln_g, conf_ln_b, sc_conv)
        a_shift, a_shifted = _swa_shift(swa_q_norm[l], swa_k_norm[l], rel_bias, swa_sink[l])
        oa = lax.cond(a_shifted, functools.partial(_swa, online=False),
                      functools.partial(_swa, online=True),
                      rel_bias * LOG2E, swa_sink[l] * LOG2E, a_shift * LOG2E, qa, ka, va)
        of = lax.cond(shifted, functools.partial(_fox, online=False),
                      functools.partial(_fox, online=True), qf, kf, vft)
        x = _merge(x, l, uc, us, oa, of, w_branch, w_gate, w_out_b)
        x = _ffn(x.reshape(b * t, d), l, *ffn2).reshape(b, t, d)
    return x
```

```python
import functools
import math

import numpy as np
import jax
import jax.numpy as jnp
from jax import lax
from jax.experimental import pallas as pl
from jax.experimental.pallas import tpu as pltpu

F32 = jnp.float32
BF16 = jnp.bfloat16

D_MODEL = 1024
HEAD_DIM = 64
CONV_CH = 256
CONV_K = 31
SC_CH = 256
SC_K = 3
SWA_Q_HEADS = 4
SWA_KV_HEADS = 2
WINDOW = 128
FOX_HEADS = 4
N_BUCKETS = 32
MAX_DISTANCE = 128
D_FF = 2816
N_BRANCH = 4
EPS = 1e-6
NEG_INF = -1e30
LOG2E = math.log2(math.e)

LANES = 128
SUBLANES = 8
VMEM_LIMIT = 56 * 1024 * 1024

COL_CONV = 0
COL_SC = COL_CONV + 2 * CONV_CH
COL_AQ = COL_SC + 3 * SC_CH
COL_AK = COL_AQ + SWA_Q_HEADS * HEAD_DIM
COL_AV = COL_AK + SWA_KV_HEADS * HEAD_DIM
COL_FQ = COL_AV + SWA_KV_HEADS * HEAD_DIM
COL_FK = COL_FQ + FOX_HEADS * HEAD_DIM
COL_FV = COL_FK + FOX_HEADS * HEAD_DIM
COL_FF = COL_FV + FOX_HEADS * HEAD_DIM
FOX_PAIRS = FOX_HEADS // 2
FF_REP = 9
PROJ_COLS = COL_FF + LANES
SWA_Q_ORDER = (0, 2, 1, 3)

ROW_TILE = 512
CONV_HALO = 32
SC_HALO = 8
CONV_CHUNK = 64
CONV_TILE = 1024
SWA_TILE = 2048
FOX_TQ = 2048
FOX_TK = 512
FOX_QC = 256
FOX_VROWS = 80
FOX_UNROLL = 4
SWA_MAX_SPREAD = 100.0
FOX_MAX_SHIFT = 56.0


def _params(n_axes, flags=None):
    return pltpu.CompilerParams(dimension_semantics=("arbitrary",) * n_axes,
                                vmem_limit_bytes=VMEM_LIMIT, flags=flags)


def _resident(shape, layer=None):
    if layer is None:
        return pl.BlockSpec(shape, lambda *_: (0,) * len(shape), pipeline_mode=pl.Buffered(1))
    return pl.BlockSpec((None,) + tuple(shape), lambda *_: (layer,) + (0,) * len(shape),
                        pipeline_mode=pl.Buffered(1))


def _sigmoid(x):
    return 1.0 / (1.0 + jnp.exp2(x * -LOG2E))


def _ffn_kernel(x_ref, wg_ref, wu_ref, wd_ref, o_ref):
    x = x_ref[...]
    inv = lax.rsqrt(jnp.mean(x * x, axis=-1, keepdims=True) + EPS)
    xb = x.astype(BF16)
    half_gate = jnp.dot(xb, wg_ref[...], preferred_element_type=F32) * (0.5 * inv)
    up = jnp.dot(xb, wu_ref[...], preferred_element_type=F32)
    act = ((half_gate + half_gate * jnp.tanh(half_gate)) * up).astype(BF16)
    y = jnp.dot(act, wd_ref[...], preferred_element_type=F32)
    o_ref[...] = x + (0.5 * inv) * y


def _ffn(x2d, layer, wg, wu, wd):
    n, d = x2d.shape
    row = pl.BlockSpec((ROW_TILE, d), lambda i: (i, 0))
    return pl.pallas_call(
        _ffn_kernel,
        grid=(n // ROW_TILE,),
        in_specs=[row, _resident(wg.shape[1:], layer), _resident(wu.shape[1:], layer),
                  _resident(wd.shape[1:], layer)],
        out_specs=row,
        out_shape=jax.ShapeDtypeStruct((n, d), F32),
        compiler_params=_params(1),
        name="ffn",
    )(x2d, wg, wu, wd)


def _split3(v):
    hi = v.astype(BF16).astype(F32)
    r = v - hi
    mid = r.astype(BF16).astype(F32)
    return hi, mid, r - mid


def _pair_rms(zb, first, eps):
    sq = zb * zb
    s0 = jnp.sum(jnp.where(first, sq, 0.0), axis=-1, keepdims=True)
    s1 = jnp.sum(jnp.where(first, 0.0, sq), axis=-1, keepdims=True)
    inv = jnp.where(first, lax.rsqrt(s0 * (1.0 / HEAD_DIM) + eps),
                    lax.rsqrt(s1 * (1.0 / HEAD_DIM) + eps))
    return zb * inv


def _inproj_kernel(x_ref, w_ref, qkg_ref, bf_ref, shift_ref,
                   zc_ref, zs_ref, qa_ref, ka_ref, va_ref, qf_ref, kf_ref, vft_ref, carry_ref):
    tm = x_ref.shape[1]

    @pl.when(pl.program_id(1) == 0)
    def _():
        carry_ref[...] = jnp.zeros_like(carry_ref)

    x = x_ref[0]
    ms = jnp.mean(x * x, axis=-1, keepdims=True) + EPS
    inv = lax.rsqrt(ms)
    qk_eps = EPS * ms
    h = x.astype(BF16)
    lane = lax.broadcasted_iota(jnp.int32, (tm, LANES), 1)
    first = lane < HEAD_DIM
    slot = lane % FF_REP
    used = lane < FOX_HEADS * FF_REP
    ext = {}

    def normed(z, col0, i):
        g0 = col0 - COL_AQ + i * LANES
        return _pair_rms(z[:, i * LANES:(i + 1) * LANES], first, qk_eps) * qkg_ref[:, g0:g0 + LANES]

    def forget_terms(z):
        ff = z * inv + bf_ref[...]
        logf = jnp.minimum(ff, 0.0) - jnp.log1p(jnp.exp(-jnp.abs(ff)))
        rows = lax.broadcasted_iota(jnp.int32, logf.shape, 0)
        cum = logf
        step = 1
        while step < tm:
            cum = cum + jnp.where(rows >= step, pltpu.roll(cum, step, axis=0), 0.0)
            step *= 2
        cum = cum + carry_ref[...]
        carry_ref[...] = cum[tm - 1:tm, :]
        cum = cum * LOG2E
        s_hi, s_mid, s_lo = _split3(shift_ref[...])
        hi, mid, lo = _split3(cum)
        q_extra = jnp.where(slot == 0, hi, jnp.where(slot == 1, mid, jnp.where(
            slot == 2, lo, jnp.where(slot < 6, 1.0, jnp.where(
                slot == 6, -s_hi, jnp.where(slot == 7, -s_mid, -s_lo))))))
        k_extra = jnp.where(slot == 3, -hi, jnp.where(slot == 4, -mid, jnp.where(
            slot == 5, -lo, 1.0)))
        ext["q"] = jnp.where(used, q_extra, 0.0).astype(BF16)
        ext["k"] = jnp.where(used, k_extra, 0.0).astype(BF16)

    def fox_qk(z):
        for p in range(FOX_PAIRS):
            base = 2 * p * LANES
            qf_ref[0, :, base:base + LANES] = normed(z, COL_FQ, p).astype(BF16)
            qf_ref[0, :, base + LANES:base + 2 * LANES] = ext["q"]
            kf_ref[0, :, base:base + LANES] = normed(z, COL_FQ, FOX_PAIRS + p).astype(BF16)
            kf_ref[0, :, base + LANES:base + 2 * LANES] = ext["k"]

    def swa_qk(z):
        n_q = SWA_Q_HEADS // 2
        for i in range(n_q):
            qa_ref[0, :, i * LANES:(i + 1) * LANES] = normed(z, COL_AQ, i).astype(BF16)
        ka_ref[0] = normed(z, COL_AQ, n_q).astype(BF16)

    def fox_v(z):
        vt = (z * inv).T
        ones_row = (lax.broadcasted_iota(jnp.int32, (FOX_VROWS - HEAD_DIM, tm), 0) == 0).astype(F32)
        for i in range(FOX_HEADS):
            vft_ref[0, i * FOX_VROWS:(i + 1) * FOX_VROWS, :] = jnp.concatenate(
                [vt[i * HEAD_DIM:(i + 1) * HEAD_DIM], ones_row], axis=0).astype(BF16)

    def swa_v(z):
        va_ref[0, :, 0:LANES] = (z * inv).astype(BF16)
        va_ref[0, :, LANES:2 * LANES] = (lane == 0).astype(BF16)

    def conv_in(z):
        zc_ref[0] = z * inv

    def sc_in(z):
        zs_ref[0] = z * inv

    z = lax.dot_general(h, w_ref[...], (((1,), (1,)), ((), ())),
                        preferred_element_type=F32)
    for c0, c1, epilogue in ((COL_FF, PROJ_COLS, forget_terms), (COL_FQ, COL_FV, fox_qk),
                             (COL_AQ, COL_AV, swa_qk), (COL_FV, COL_FF, fox_v),
                             (COL_AV, COL_FQ, swa_v), (COL_CONV, COL_SC, conv_in),
                             (COL_SC, COL_AQ, sc_in)):
        epilogue(z[:, c0:c1])


def _inproj(x, layer, w, qkg, bf, shift):
    b, t, d = x.shape
    tm = ROW_TILE

    def rows(width):
        return pl.BlockSpec((1, tm, width), lambda bi, ti: (bi, ti, 0))

    def oshape(width, dtype=BF16):
        return jax.ShapeDtypeStruct((b, t, width), dtype)

    n_v = FOX_HEADS * FOX_VROWS
    return pl.pallas_call(
        _inproj_kernel,
        grid=(b, t // tm),
        in_specs=[rows(d), _resident(w.shape[1:], layer),
                  _resident(qkg.shape[1:], layer), _resident(bf.shape[1:], layer),
                  _resident((1, LANES))],
        out_specs=[rows(2 * CONV_CH), rows(3 * SC_CH),
                   rows(SWA_Q_HEADS * HEAD_DIM), rows(SWA_KV_HEADS * HEAD_DIM),
                   rows(2 * LANES),
                   rows(2 * FOX_PAIRS * LANES), rows(2 * FOX_PAIRS * LANES),
                   pl.BlockSpec((1, n_v, tm), lambda bi, ti: (bi, 0, ti))],
        out_shape=[oshape(2 * CONV_CH, F32), oshape(3 * SC_CH, F32),
                   oshape(SWA_Q_HEADS * HEAD_DIM), oshape(SWA_KV_HEADS * HEAD_DIM),
                   oshape(2 * LANES),
                   oshape(2 * FOX_PAIRS * LANES), oshape(2 * FOX_PAIRS * LANES),
                   jax.ShapeDtypeStruct((b, n_v, t), BF16)],
        scratch_shapes=[pltpu.VMEM((1, LANES), F32)],
        compiler_params=_params(2),
        name="inproj",
    )(x, w, qkg, bf, jnp.full((1, LANES), shift, F32))


def _fill_shifted(ext_ref, base, n_taps):
    n = ext_ref.shape[1] - SUBLANES
    for r in sorted({(base + k) % SUBLANES for k in range(n_taps)} - {0}):
        ext_ref[r, 0:n, :] = ext_ref[0, r:r + n, :]


def _taps(ext_ref, w_ref, r0, base, n_taps, acc):
    groups = CONV_CHUNK // SUBLANES
    acc = acc.reshape(groups, SUBLANES, acc.shape[-1])
    for k in range(n_taps):
        r = (base + k) % SUBLANES
        start = r0 + base + k - r
        u = ext_ref[r, start:start + CONV_CHUNK, :]
        acc = acc + w_ref[k][None] * u.reshape(groups, SUBLANES, u.shape[-1])
    return acc.reshape(CONV_CHUNK, acc.shape[-1])


def _conv_kernel(zc_ref, zch_ref, zs_ref, zsh_ref, dw_ref, dwb_ref, lng_ref, lnb_ref, scw_ref,
                 uc_ref, us_ref, ext_ref, ext2_ref):
    tt = zc_ref.shape[1]
    keep = (pl.program_id(1) > 0).astype(F32)

    def glu(v):
        return v[:, :CONV_CH] * _sigmoid(v[:, CONV_CH:])

    ext_ref[0, 0:CONV_HALO, :] = glu(zch_ref[0]) * keep
    ext_ref[0, CONV_HALO:, :] = glu(zc_ref[0])
    zsh = zsh_ref[0]
    ext2_ref[0, 0:SC_HALO, :] = zsh[:, SC_CH:2 * SC_CH] * zsh[:, 2 * SC_CH:] * keep
    zs = zs_ref[0]
    ext2_ref[0, SC_HALO:, :] = zs[:, SC_CH:2 * SC_CH] * zs[:, 2 * SC_CH:]
    conv_base = CONV_HALO - (CONV_K - 1)
    sc_base = SC_HALO - (SC_K - 1)
    _fill_shifted(ext_ref, conv_base, CONV_K)
    _fill_shifted(ext2_ref, sc_base, SC_K)

    for r0 in range(0, tt, CONV_CHUNK):
        acc = _taps(ext_ref, dw_ref, r0, conv_base, CONV_K,
                    jnp.broadcast_to(dwb_ref[...], (CONV_CHUNK, CONV_CH)))
        mu = jnp.mean(acc, axis=-1, keepdims=True)
        cen = acc - mu
        var = jnp.mean(cen * cen, axis=-1, keepdims=True)
        y = cen * lax.rsqrt(var + EPS) * lng_ref[...] + lnb_ref[...]
        uc_ref[0, r0:r0 + CONV_CHUNK, :] = (y * _sigmoid(y)).astype(BF16)

        acc2 = _taps(ext2_ref, scw_ref, r0, sc_base, SC_K, jnp.zeros((CONV_CHUNK, SC_CH), F32))
        us_ref[0, r0:r0 + CONV_CHUNK, :] = (zs[r0:r0 + CONV_CHUNK, :SC_CH] * acc2).astype(BF16)


def _conv(zc, zs, layer, dw, dwb, lng, lnb, scw):
    b, t, _ = zc.shape
    tt = CONV_TILE

    def rows(width):
        return pl.BlockSpec((1, tt, width), lambda bi, ti: (bi, ti, 0))

    def halo(rows_, width):
        per = tt // rows_
        return pl.BlockSpec((1, rows_, width),
                            lambda bi, ti: (bi, jnp.maximum(ti * per - 1, 0), 0))

    def vec(a):
        return a.reshape(a.shape[0], 1, a.shape[1])

    def taps(w):
        return jnp.broadcast_to(w[:, :, None, :], w.shape[:2] + (SUBLANES, w.shape[2]))

    return pl.pallas_call(
        _conv_kernel,
        grid=(b, t // tt),
        in_specs=[rows(2 * CONV_CH), halo(CONV_HALO, 2 * CONV_CH),
                  rows(3 * SC_CH), halo(SC_HALO, 3 * SC_CH),
                  _resident((CONV_K, SUBLANES, CONV_CH), layer), _resident((1, CONV_CH), layer),
                  _resident((1, CONV_CH), layer), _resident((1, CONV_CH), layer),
                  _resident((SC_K, SUBLANES, SC_CH), layer)],
        out_specs=[rows(CONV_CH), rows(SC_CH)],
        out_shape=[jax.ShapeDtypeStruct((b, t, CONV_CH), BF16),
                   jax.ShapeDtypeStruct((b, t, SC_CH), BF16)],
        scratch_shapes=[pltpu.VMEM((SUBLANES, CONV_HALO + tt, CONV_CH), F32),
                        pltpu.VMEM((SUBLANES, SC_HALO + tt, SC_CH), F32)],
        compiler_params=_params(2),
        name="conv",
    )(zc, zc, zs, zs, taps(dw), vec(dwb), vec(lng), vec(lnb), taps(scw))


def _t5_bucket_table():
    max_exact = N_BUCKETS // 2
    dist = np.maximum(np.arange(WINDOW)[:, None] + WINDOW - np.arange(2 * WINDOW)[None, :], 0)
    d = np.maximum(dist, 1).astype(np.float32)
    large = max_exact + (np.log(d / np.float32(max_exact)) / np.float32(
        math.log(MAX_DISTANCE / max_exact)) * np.float32(N_BUCKETS - max_exact)).astype(np.int32)
    large = np.minimum(large, N_BUCKETS - 1)
    return np.where(dist < max_exact, dist, large).astype(np.int32)


def _swa_kernel(rb_ref, sink_ref, shift_ref, bucket_ref, q_ref, k_ref, kh_ref, v_ref, vh_ref, o_ref,
                bias_ref, kk_ref, vv_ref, *, online):
    tq = q_ref.shape[1]
    first_step = (pl.program_id(0) == 0) & (pl.program_id(1) == 0)

    @pl.when(first_step)
    def _():
        bucket = bucket_ref[...]
        qi = lax.broadcasted_iota(jnp.int32, bucket.shape, 0) + WINDOW
        ki = lax.broadcasted_iota(jnp.int32, bucket.shape, 1)
        dist = qi - ki
        ok = (dist >= 0) & (dist < WINDOW)
        for h in range(SWA_Q_HEADS):
            bias = jnp.zeros(bucket.shape, F32)
            for bk in range(N_BUCKETS):
                bias = jnp.where(bucket == bk, rb_ref[bk, h], bias)
            bias_ref[h] = jnp.where(ok, bias - shift_ref[h], NEG_INF)

    kk_ref[0:WINDOW, :] = kh_ref[0]
    kk_ref[WINDOW:, :] = k_ref[0]
    vv_ref[0:WINDOW, :] = vh_ref[0]
    vv_ref[WINDOW:, :] = v_ref[0]
    seq_start = pl.program_id(1) == 0
    prev_half = lax.broadcasted_iota(jnp.int32, (WINDOW, 2 * WINDOW), 1) < WINDOW
    first = lax.broadcasted_iota(jnp.int32, (WINDOW, LANES), 1) < HEAD_DIM

    for sb in range(tq // WINDOW):
        r0 = sb * WINDOW
        kk = kk_ref[r0:r0 + 2 * WINDOW, :]
        vv = vv_ref[r0:r0 + 2 * WINDOW, :]
        for blk in range(SWA_Q_HEADS // 2):
            qb = q_ref[0, r0:r0 + WINDOW, blk * LANES:(blk + 1) * LANES]
            outs = []
            for half in range(2):
                h = SWA_Q_ORDER[2 * blk + half]
                keep = first if half == 0 else jnp.logical_not(first)
                q = jnp.where(keep, qb, jnp.zeros_like(qb))
                s = lax.dot_general(q, kk, (((1,), (1,)), ((), ())), preferred_element_type=F32)
                s = s + bias_ref[h]
                if sb == 0:
                    s = jnp.where(seq_start & prev_half, NEG_INF, s)
                sink = jnp.full((1, 1), sink_ref[h] - shift_ref[h], F32)
                if online:
                    m = jnp.maximum(jnp.max(s, axis=-1, keepdims=True), sink)
                    s, sink = s - m, sink - m
                pv = jnp.dot(jnp.exp2(s).astype(BF16), vv, preferred_element_type=F32)
                outs.append(pv[:, :LANES] / (pv[:, LANES:LANES + 1] + jnp.exp2(sink)))
            o_ref[0, r0:r0 + WINDOW, blk * LANES:(blk + 1) * LANES] = jnp.where(
                first, outs[0], outs[1]).astype(BF16)


def _swa(rel_bias2, sink2, shift, qa, ka, va, *, online):
    b, t, _ = qa.shape
    tq = SWA_TILE
    per = tq // WINDOW
    kw, vw = ka.shape[-1], va.shape[-1]
    smem = pl.BlockSpec(memory_space=pltpu.SMEM)

    def cur(width):
        return pl.BlockSpec((1, tq, width), lambda bi, ti: (bi, ti, 0))

    def halo(width):
        return pl.BlockSpec((1, WINDOW, width),
                            lambda bi, ti: (bi, jnp.maximum(ti * per - 1, 0), 0))

    return pl.pallas_call(
        functools.partial(_swa_kernel, online=online),
        grid=(b, t // tq),
        in_specs=[smem, smem, smem, _resident((WINDOW, 2 * WINDOW)),
                  cur(qa.shape[-1]), cur(kw), halo(kw), cur(vw), halo(vw)],
        out_specs=cur(qa.shape[-1]),
        out_shape=jax.ShapeDtypeStruct(qa.shape, BF16),
        scratch_shapes=[pltpu.VMEM((SWA_Q_HEADS, WINDOW, 2 * WINDOW), F32),
                        pltpu.VMEM((WINDOW + tq, kw), BF16),
                        pltpu.VMEM((WINDOW + tq, vw), BF16)],
        compiler_params=_params(2),
        name="swa_online" if online else "swa",
    )(rel_bias2, sink2, shift, jnp.asarray(_t5_bucket_table()), qa, ka, ka, va, va)


def _swa_shift(q_gain, k_gain, rel_bias, sink):
    qk = HEAD_DIM ** 0.5 * jnp.max(jnp.abs(q_gain)) * jnp.max(jnp.abs(k_gain))
    hi = jnp.maximum(qk + jnp.max(rel_bias, axis=0), sink)
    lo = jnp.maximum(rel_bias[0] - qk, sink)
    usable = jnp.all((hi - lo) * LOG2E < SWA_MAX_SPREAD)
    return jnp.where(usable, hi, 0.0), usable


def _fox_kernel(q_ref, k_ref, vt_ref, o_ref, *, online):
    tq = q_ref.shape[1]
    qi = pl.program_id(2)
    qp = q_ref[0]
    lane = lax.broadcasted_iota(jnp.int32, qp.shape, 1)
    zero = jnp.zeros_like(qp)
    x0 = LANES + pl.program_id(1) * (2 * FF_REP)
    q_heads = [
        jnp.where((lane < HEAD_DIM) | ((lane >= x0) & (lane < x0 + FF_REP)), qp, zero),
        jnp.where(((lane >= HEAD_DIM) & (lane < LANES)) |
                  ((lane >= x0 + FF_REP) & (lane < x0 + 2 * FF_REP)), qp, zero)]

    chains = [(hd, c) for c in range(tq // FOX_QC) for hd in range(2)]

    def run(blocks, carry):
        state = dict(enumerate(carry))
        items = []
        for j, diag in blocks:
            k0 = pl.multiple_of(j * FOX_TK, FOX_TK)
            ks = k_ref[0, pl.ds(k0, FOX_TK), :]
            vts = [vt_ref[0, hd * FOX_VROWS:(hd + 1) * FOX_VROWS, pl.ds(k0, FOX_TK)]
                   for hd in range(2)]
            key_lo = 0 if diag is None else diag * FOX_TK
            for ci, (hd, c) in enumerate(chains):
                if diag is None or key_lo <= (c + 1) * FOX_QC - 1:
                    masked = diag is not None and key_lo + FOX_TK - 1 > c * FOX_QC
                    items.append((ci, ks, vts[hd], key_lo if masked else None))

        def scores(item):
            ci, ks, _, mask_lo = item
            hd, c = chains[ci]
            st = lax.dot_general(ks, q_heads[hd][c * FOX_QC:(c + 1) * FOX_QC],
                                 (((1,), (1,)), ((), ())), preferred_element_type=F32)
            if mask_lo is not None:
                key = lax.broadcasted_iota(jnp.int32, st.shape, 0) + mask_lo
                qry = lax.broadcasted_iota(jnp.int32, st.shape, 1) + c * FOX_QC
                st = jnp.where(key <= qry, st, NEG_INF)
            return st

        def softmax(item, st):
            m, acc = state[item[0]]
            if not online:
                return m, acc, jnp.exp2(st).astype(BF16)
            m_new = jnp.maximum(m, jnp.max(st, axis=0, keepdims=True))
            return m_new, jnp.exp2(m - m_new) * acc, jnp.exp2(st - m_new).astype(BF16)

        def weighted(item, part):
            m_new, acc, p = part
            state[item[0]] = (m_new, acc + jnp.dot(item[2], p, preferred_element_type=F32))

        st_next = scores(items[0])
        part_prev = None
        for pos, item in enumerate(items):
            st_cur = st_next
            if pos + 1 < len(items):
                st_next = scores(items[pos + 1])
            part = softmax(item, st_cur)
            if part_prev is not None:
                weighted(items[pos - 1], part_prev)
            part_prev = part
        weighted(items[-1], part_prev)
        return tuple(state[ci] for ci in range(len(chains)))

    init = tuple((jnp.full((1, FOX_QC), NEG_INF, F32), jnp.zeros((FOX_VROWS, FOX_QC), F32))
                 for _ in chains)
    per = tq // FOX_TK
    assert per % FOX_UNROLL == 0
    res = lax.fori_loop(
        0, qi * (per // FOX_UNROLL),
        lambda it, c: run([(it * FOX_UNROLL + u, None) for u in range(FOX_UNROLL)], c), init)
    res = run([(qi * per + d, d) for d in range(per)], res)
    for c in range(tq // FOX_QC):
        heads = [acc[:HEAD_DIM] / acc[HEAD_DIM:HEAD_DIM + 1] for _, acc in res[2 * c:2 * c + 2]]
        out_t = jnp.concatenate(heads, axis=0)
        o_ref[0, c * FOX_QC:(c + 1) * FOX_QC, :] = out_t.T.astype(BF16)


def _fox(qf, kf, vft, *, online):
    b, t, _ = qf.shape
    assert FOX_TQ % FOX_TK == 0 and FOX_TQ % FOX_QC == 0
    return pl.pallas_call(
        functools.partial(_fox_kernel, online=online),
        grid=(b, FOX_PAIRS, t // FOX_TQ),
        in_specs=[pl.BlockSpec((1, FOX_TQ, 2 * LANES), lambda bi, pi, qi: (bi, qi, pi)),
                  pl.BlockSpec((1, t, 2 * LANES), lambda bi, pi, qi: (bi, 0, pi)),
                  pl.BlockSpec((1, 2 * FOX_VROWS, t), lambda bi, pi, qi: (bi, pi, 0))],
        out_specs=pl.BlockSpec((1, FOX_TQ, LANES), lambda bi, pi, qi: (bi, qi, pi)),
        out_shape=jax.ShapeDtypeStruct((b, t, FOX_HEADS * HEAD_DIM), BF16),
        compiler_params=_params(3),
        name="fox_online" if online else "fox",
    )(qf, kf, vft)


def _fox_shift(q_gain, k_gain):
    bound = (LOG2E * HEAD_DIM ** 0.5) * jnp.max(jnp.abs(q_gain)) * jnp.max(jnp.abs(k_gain))
    usable = bound < FOX_MAX_SHIFT
    return jnp.where(usable, bound, 0.0), usable


def _merge_kernel(x_ref, uc_ref, us_ref, oa_ref, of_ref, wb_ref, wgate_ref, wout_ref, o_ref):
    x = x_ref[0]
    half_inv = 0.5 * lax.rsqrt(jnp.mean(x * x, axis=-1, keepdims=True) + EPS)
    xb = x.astype(BF16)
    twice_merged = jnp.zeros(x.shape, F32)
    for i, u_ref in enumerate((uc_ref, us_ref, oa_ref, of_ref)):
        p = jnp.dot(u_ref[0], wb_ref[i], preferred_element_type=F32)
        half_gate = lax.dot_general(xb, wgate_ref[i * D_MODEL:(i + 1) * D_MODEL, :],
                                    (((1,), (1,)), ((), ())), preferred_element_type=F32) * half_inv
        twice_merged = twice_merged + (p + jnp.tanh(half_gate) * p)
    merged = (0.5 * twice_merged).astype(BF16)
    o_ref[0] = x + jnp.dot(merged, wout_ref[...], preferred_element_type=F32)


def _merge(x, layer, uc, us, oa, of, wb, wgate, wout):
    b, t, d = x.shape
    tm = ROW_TILE

    def rows(width):
        return pl.BlockSpec((1, tm, width), lambda bi, ti: (bi, ti, 0))

    return pl.pallas_call(
        _merge_kernel,
        grid=(b, t // tm),
        in_specs=[rows(d), rows(CONV_CH), rows(SC_CH),
                  rows(SWA_Q_HEADS * HEAD_DIM), rows(FOX_HEADS * HEAD_DIM),
                  _resident(wb.shape[1:], layer), _resident(wgate.shape[1:], layer),
                  _resident(wout.shape[1:], layer)],
        out_specs=rows(d),
        out_shape=jax.ShapeDtypeStruct((b, t, d), F32),
        compiler_params=_params(2),
        name="merge",
    )(x, uc, us, oa, of, wb, wgate, wout)


def _fgate_cols(ff, axis=-1):
    rep = jnp.repeat(ff, FF_REP, axis=axis)
    pad = [(0, 0)] * ff.ndim
    pad[axis] = (0, LANES - FOX_HEADS * FF_REP)
    return jnp.pad(rep, pad)


def _prep_proj(w_in, mix_gain):
    wt = jnp.transpose(w_in, (2, 0, 1))
    aq0 = COL_AQ
    ff0 = COL_FF
    heads = [wt[aq0 + h * HEAD_DIM: aq0 + (h + 1) * HEAD_DIM] for h in SWA_Q_ORDER]
    w_proj = jnp.concatenate([wt[:aq0]] + heads + [wt[COL_AK:ff0],
                                                   _fgate_cols(wt[ff0:ff0 + FOX_HEADS], axis=0)], axis=0)
    w_gate = wt[ff0 + FOX_HEADS:]
    return ((jnp.transpose(w_proj, (1, 0, 2)) * mix_gain[:, None, :]).astype(BF16),
            (jnp.transpose(w_gate, (1, 0, 2)) * mix_gain[:, None, :]).astype(BF16))


def _prep_ffn(norm_gain, w_gate, w_up, w_down):
    g = norm_gain[:, :, None]
    return (g * w_gate).astype(BF16), (g * w_up).astype(BF16), w_down.astype(BF16)


def _prep_qk_gains(aq, ak, fq, fk):
    scale = HEAD_DIM ** -0.5 * LOG2E
    ones = jnp.ones((aq.shape[0], SWA_KV_HEADS * HEAD_DIM), F32)
    row = jnp.concatenate([jnp.tile(aq * scale, (1, SWA_Q_HEADS)), jnp.tile(ak, (1, SWA_KV_HEADS)),
                           ones, jnp.tile(fq * scale, (1, FOX_HEADS)),
                           jnp.tile(fk, (1, FOX_HEADS))],
                          axis=-1)
    return row[:, None, :]


def kernel(x, rel_bias, ffn1_norm, ffn1_w_gate, ffn1_w_up, ffn1_w_down, mix_norm, w_in, b_forget, conf_dw, conf_dw_b, conf_ln_g, conf_ln_b, conf_w_out, sc_conv, sc_w_out, swa_q_norm, swa_k_norm, swa_sink, swa_w_o, fox_q_norm, fox_k_norm, fox_w_o, w_out, ffn2_norm, ffn2_w_gate, ffn2_w_up, ffn2_w_down):
    b, t, d = x.shape
    depth = w_in.shape[0]

    w_proj, w_gate = _prep_proj(w_in, mix_norm)
    qk_gains = _prep_qk_gains(swa_q_norm, swa_k_norm, fox_q_norm, fox_k_norm)
    bf = _fgate_cols(b_forget)[:, None, :]
    swa_rows = jnp.concatenate([swa_w_o[:, h * HEAD_DIM:(h + 1) * HEAD_DIM] for h in SWA_Q_ORDER],
                               axis=1)
    w_branch = jnp.stack([conf_w_out, sc_w_out, swa_rows, fox_w_o], axis=1).astype(BF16)
    w_out_b = w_out.astype(BF16)
    ffn1 = _prep_ffn(ffn1_norm, ffn1_w_gate, ffn1_w_up, ffn1_w_down)
    ffn2 = _prep_ffn(ffn2_norm, ffn2_w_gate, ffn2_w_up, ffn2_w_down)

    for l in range(depth):
        x = _ffn(x.reshape(b * t, d), l, *ffn1).reshape(b, t, d)
        shift, shifted = _fox_shift(fox_q_norm[l], fox_k_norm[l])
        zc, zs, qa, ka, va, qf, kf, vft = _inproj(x, l, w_proj, qk_gains, bf, shift)
        uc, us = _conv(zc, zs, l, conf_dw, conf_dw_b, conf_ln_g, conf_ln_b, sc_conv)
        a_shift, a_shifted = _swa_shift(swa_q_norm[l], swa_k_norm[l], rel_bias, swa_sink[l])
        oa = lax.cond(a_shifted, functools.partial(_swa, online=False),
                      functools.partial(_swa, online=True),
                      rel_bias * LOG2E, swa_sink[l] * LOG2E, a_shift * LOG2E, qa, ka, va)
        of = lax.cond(shifted, functools.partial(_fox, online=False),
                      functools.partial(_fox, online=True), qf, kf, vft)
        x = _merge(x, l, uc, us, oa, of, w_branch, w_gate, w_out_b)
        x = _ffn(x.reshape(b * t, d), l, *ffn2).reshape(b, t, d)
    return x
```

```python
import functools
import math

import numpy as np
import jax
import jax.numpy as jnp
from jax import lax
from jax.experimental import pallas as pl
from jax.experimental.pallas import tpu as pltpu

F32 = jnp.float32
BF16 = jnp.bfloat16

D_MODEL = 1024
HEAD_DIM = 64
CONV_CH = 256
CONV_K = 31
SC_CH = 256
SC_K = 3
SWA_Q_HEADS = 4
SWA_KV_HEADS = 2
WINDOW = 128
FOX_HEADS = 4
N_BUCKETS = 32
MAX_DISTANCE = 128
D_FF = 2816
N_BRANCH = 4
EPS = 1e-6
NEG_INF = -1e30
LOG2E = math.log2(math.e)

LANES = 128
SUBLANES = 8
VMEM_LIMIT = 56 * 1024 * 1024

COL_CONV = 0
COL_SC = COL_CONV + 2 * CONV_CH
COL_AQ = COL_SC + 3 * SC_CH
COL_AK = COL_AQ + SWA_Q_HEADS * HEAD_DIM
COL_AV = COL_AK + SWA_KV_HEADS * HEAD_DIM
COL_FQ = COL_AV + SWA_KV_HEADS * HEAD_DIM
COL_FK = COL_FQ + FOX_HEADS * HEAD_DIM
COL_FV = COL_FK + FOX_HEADS * HEAD_DIM
COL_FF = COL_FV + FOX_HEADS * HEAD_DIM
FOX_PAIRS = FOX_HEADS // 2
FF_REP = 9
PROJ_COLS = COL_FF + LANES
SWA_Q_ORDER = (0, 2, 1, 3)

ROW_TILE = 512
CONV_HALO = 32
SC_HALO = 8
CONV_CHUNK = 64
INPROJ_TILE = 1024
MERGE_TILE = 1024
CONV_TILE = 1024
SWA_TILE = 2048
FOX_TQ = 2048
FOX_TK = 512
FOX_QC = 256
FOX_VROWS = 80
FOX_UNROLL = 4
SWA_MAX_SPREAD = 100.0
FOX_MAX_SHIFT = 56.0


def _params(n_axes, flags=None):
    return pltpu.CompilerParams(dimension_semantics=("arbitrary",) * n_axes,
                                vmem_limit_bytes=VMEM_LIMIT, flags=flags)


def _resident(shape, layer=None):
    if layer is None:
        return pl.BlockSpec(shape, lambda *_: (0,) * len(shape), pipeline_mode=pl.Buffered(1))
    return pl.BlockSpec((None,) + tuple(shape), lambda *_: (layer,) + (0,) * len(shape),
                        pipeline_mode=pl.Buffered(1))


def _sigmoid(x):
    return 1.0 / (1.0 + jnp.exp2(x * -LOG2E))


def _ffn_kernel(x_ref, wg_ref, wu_ref, wd_ref, o_ref):
    x = x_ref[...]
    inv = lax.rsqrt(jnp.mean(x * x, axis=-1, keepdims=True) + EPS)
    xb = x.astype(BF16)
    half_gate = jnp.dot(xb, wg_ref[...], preferred_element_type=F32) * (0.5 * inv)
    up = jnp.dot(xb, wu_ref[...], preferred_element_type=F32)
    act = ((half_gate + half_gate * jnp.tanh(half_gate)) * up).astype(BF16)
    y = jnp.dot(act, wd_ref[...], preferred_element_type=F32)
    o_ref[...] = x + (0.5 * inv) * y


def _ffn(x2d, layer, wg, wu, wd):
    n, d = x2d.shape
    row = pl.BlockSpec((ROW_TILE, d), lambda i: (i, 0))
    return pl.pallas_call(
        _ffn_kernel,
        grid=(n // ROW_TILE,),
        in_specs=[row, _resident(wg.shape[1:], layer), _resident(wu.shape[1:], layer),
                  _resident(wd.shape[1:], layer)],
        out_specs=row,
        out_shape=jax.ShapeDtypeStruct((n, d), F32),
        compiler_params=_params(1),
        name="ffn",
    )(x2d, wg, wu, wd)


def _split3(v):
    hi = v.astype(BF16).astype(F32)
    r = v - hi
    mid = r.astype(BF16).astype(F32)
    return hi, mid, r - mid


def _pair_rms(zb, first, eps):
    sq = zb * zb
    s0 = jnp.sum(jnp.where(first, sq, 0.0), axis=-1, keepdims=True)
    s1 = jnp.sum(jnp.where(first, 0.0, sq), axis=-1, keepdims=True)
    inv = jnp.where(first, lax.rsqrt(s0 * (1.0 / HEAD_DIM) + eps),
                    lax.rsqrt(s1 * (1.0 / HEAD_DIM) + eps))
    return zb * inv


def _inproj_kernel(x_ref, w_ref, qkg_ref, bf_ref, shift_ref,
                   zc_ref, zs_ref, qa_ref, ka_ref, va_ref, qf_ref, kf_ref, vft_ref, carry_ref):
    tm = x_ref.shape[1]

    @pl.when(pl.program_id(1) == 0)
    def _():
        carry_ref[...] = jnp.zeros_like(carry_ref)

    x = x_ref[0]
    ms = jnp.mean(x * x, axis=-1, keepdims=True) + EPS
    inv = lax.rsqrt(ms)
    qk_eps = EPS * ms
    h = x.astype(BF16)
    lane = lax.broadcasted_iota(jnp.int32, (tm, LANES), 1)
    first = lane < HEAD_DIM
    slot = lane % FF_REP
    used = lane < FOX_HEADS * FF_REP
    ext = {}

    def normed(z, col0, i):
        g0 = col0 - COL_AQ + i * LANES
        return _pair_rms(z[:, i * LANES:(i + 1) * LANES], first, qk_eps) * qkg_ref[:, g0:g0 + LANES]

    def forget_terms(z):
        ff = z * inv + bf_ref[...]
        logf = jnp.minimum(ff, 0.0) - jnp.log1p(jnp.exp(-jnp.abs(ff)))
        rows = lax.broadcasted_iota(jnp.int32, logf.shape, 0)
        cum = logf
        step = 1
        while step < tm:
            cum = cum + jnp.where(rows >= step, pltpu.roll(cum, step, axis=0), 0.0)
            step *= 2
        cum = cum + carry_ref[...]
        carry_ref[...] = cum[tm - 1:tm, :]
        cum = cum * LOG2E
        s_hi, s_mid, s_lo = _split3(shift_ref[...])
        hi, mid, lo = _split3(cum)
        q_extra = jnp.where(slot == 0, hi, jnp.where(slot == 1, mid, jnp.where(
            slot == 2, lo, jnp.where(slot < 6, 1.0, jnp.where(
                slot == 6, -s_hi, jnp.where(slot == 7, -s_mid, -s_lo))))))
        k_extra = jnp.where(slot == 3, -hi, jnp.where(slot == 4, -mid, jnp.where(
            slot == 5, -lo, 1.0)))
        ext["q"] = jnp.where(used, q_extra, 0.0).astype(BF16)
        ext["k"] = jnp.where(used, k_extra, 0.0).astype(BF16)

    def fox_qk(z):
        for p in range(FOX_PAIRS):
            base = 2 * p * LANES
            qf_ref[0, :, base:base + LANES] = normed(z, COL_FQ, p).astype(BF16)
            qf_ref[0, :, base + LANES:base + 2 * LANES] = ext["q"]
            kf_ref[0, :, base:base + LANES] = normed(z, COL_FQ, FOX_PAIRS + p).astype(BF16)
            kf_ref[0, :, base + LANES:base + 2 * LANES] = ext["k"]

    def swa_qk(z):
        n_q = SWA_Q_HEADS // 2
        for i in range(n_q):
            qa_ref[0, :, i * LANES:(i + 1) * LANES] = normed(z, COL_AQ, i).astype(BF16)
        ka_ref[0] = normed(z, COL_AQ, n_q).astype(BF16)

    def fox_v(z):
        vt = (z * inv).T
        ones_row = (lax.broadcasted_iota(jnp.int32, (FOX_VROWS - HEAD_DIM, tm), 0) == 0).astype(F32)
        for i in range(FOX_HEADS):
            vft_ref[0, i * FOX_VROWS:(i + 1) * FOX_VROWS, :] = jnp.concatenate(
                [vt[i * HEAD_DIM:(i + 1) * HEAD_DIM], ones_row], axis=0).astype(BF16)

    def swa_v(z):
        va_ref[0, :, 0:LANES] = (z * inv).astype(BF16)
        va_ref[0, :, LANES:2 * LANES] = (lane == 0).astype(BF16)

    def conv_in(z):
        zc_ref[0] = z * inv

    def sc_in(z):
        zs_ref[0] = z * inv

    z = lax.dot_general(h, w_ref[...], (((1,), (1,)), ((), ())),
                        preferred_element_type=F32)
    for c0, c1, epilogue in ((COL_FF, PROJ_COLS, forget_terms), (COL_FQ, COL_FV, fox_qk),
                             (COL_AQ, COL_AV, swa_qk), (COL_FV, COL_FF, fox_v),
                             (COL_AV, COL_FQ, swa_v), (COL_CONV, COL_SC, conv_in),
                             (COL_SC, COL_AQ, sc_in)):
        epilogue(z[:, c0:c1])


def _inproj(x, layer, w, qkg, bf, shift):
    b, t, d = x.shape
    tm = INPROJ_TILE

    def rows(width):
        return pl.BlockSpec((1, tm, width), lambda bi, ti: (bi, ti, 0))

    def oshape(width, dtype=BF16):
        return jax.ShapeDtypeStruct((b, t, width), dtype)

    n_v = FOX_HEADS * FOX_VROWS
    return pl.pallas_call(
        _inproj_kernel,
        grid=(b, t // tm),
        in_specs=[rows(d), _resident(w.shape[1:], layer),
                  _resident(qkg.shape[1:], layer), _resident(bf.shape[1:], layer),
                  _resident((1, LANES))],
        out_specs=[rows(2 * CONV_CH), rows(3 * SC_CH),
                   rows(SWA_Q_HEADS * HEAD_DIM), rows(SWA_KV_HEADS * HEAD_DIM),
                   rows(2 * LANES),
                   rows(2 * FOX_PAIRS * LANES), rows(2 * FOX_PAIRS * LANES),
                   pl.BlockSpec((1, n_v, tm), lambda bi, ti: (bi, 0, ti))],
        out_shape=[oshape(2 * CONV_CH, F32), oshape(3 * SC_CH, F32),
                   oshape(SWA_Q_HEADS * HEAD_DIM), oshape(SWA_KV_HEADS * HEAD_DIM),
                   oshape(2 * LANES),
                   oshape(2 * FOX_PAIRS * LANES), oshape(2 * FOX_PAIRS * LANES),
                   jax.ShapeDtypeStruct((b, n_v, t), BF16)],
        scratch_shapes=[pltpu.VMEM((1, LANES), F32)],
        compiler_params=_params(2),
        name="inproj",
    )(x, w, qkg, bf, jnp.full((1, LANES), shift, F32))


def _fill_shifted(ext_ref, base, n_taps):
    n = ext_ref.shape[1] - SUBLANES
    for r in sorted({(base + k) % SUBLANES for k in range(n_taps)} - {0}):
        ext_ref[r, 0:n, :] = ext_ref[0, r:r + n, :]


def _taps(ext_ref, w_ref, r0, base, n_taps, acc):
    groups = CONV_CHUNK // SUBLANES
    acc = acc.reshape(groups, SUBLANES, acc.shape[-1])
    for k in range(n_taps):
        r = (base + k) % SUBLANES
        start = r0 + base + k - r
        u = ext_ref[r, start:start + CONV_CHUNK, :]
        acc = acc + w_ref[k][None] * u.reshape(groups, SUBLANES, u.shape[-1])
    return acc.reshape(CONV_CHUNK, acc.shape[-1])


def _conv_kernel(zc_ref, zch_ref, zs_ref, zsh_ref, dw_ref, dwb_ref, lng_ref, lnb_ref, scw_ref,
                 uc_ref, us_ref, ext_ref, ext2_ref):
    tt = zc_ref.shape[1]
    keep = (pl.program_id(1) > 0).astype(F32)

    def glu(v):
        return v[:, :CONV_CH] * _sigmoid(v[:, CONV_CH:])

    ext_ref[0, 0:CONV_HALO, :] = glu(zch_ref[0]) * keep
    ext_ref[0, CONV_HALO:, :] = glu(zc_ref[0])
    zsh = zsh_ref[0]
    ext2_ref[0, 0:SC_HALO, :] = zsh[:, SC_CH:2 * SC_CH] * zsh[:, 2 * SC_CH:] * keep
    zs = zs_ref[0]
    ext2_ref[0, SC_HALO:, :] = zs[:, SC_CH:2 * SC_CH] * zs[:, 2 * SC_CH:]
    conv_base = CONV_HALO - (CONV_K - 1)
    sc_base = SC_HALO - (SC_K - 1)
    _fill_shifted(ext_ref, conv_base, CONV_K)
    _fill_shifted(ext2_ref, sc_base, SC_K)

    for r0 in range(0, tt, CONV_CHUNK):
        acc = _taps(ext_ref, dw_ref, r0, conv_base, CONV_K,
                    jnp.broadcast_to(dwb_ref[...], (CONV_CHUNK, CONV_CH)))
        mu = jnp.mean(acc, axis=-1, keepdims=True)
        cen = acc - mu
        var = jnp.mean(cen * cen, axis=-1, keepdims=True)
        y = cen * lax.rsqrt(var + EPS) * lng_ref[...] + lnb_ref[...]
        uc_ref[0, r0:r0 + CONV_CHUNK, :] = (y * _sigmoid(y)).astype(BF16)

        acc2 = _taps(ext2_ref, scw_ref, r0, sc_base, SC_K, jnp.zeros((CONV_CHUNK, SC_CH), F32))
        us_ref[0, r0:r0 + CONV_CHUNK, :] = (zs[r0:r0 + CONV_CHUNK, :SC_CH] * acc2).astype(BF16)


def _conv(zc, zs, layer, dw, dwb, lng, lnb, scw):
    b, t, _ = zc.shape
    tt = CONV_TILE

    def rows(width):
        return pl.BlockSpec((1, tt, width), lambda bi, ti: (bi, ti, 0))

    def halo(rows_, width):
        per = tt // rows_
        return pl.BlockSpec((1, rows_, width),
                            lambda bi, ti: (bi, jnp.maximum(ti * per - 1, 0), 0))

    def vec(a):
        return a.reshape(a.shape[0], 1, a.shape[1])

    def taps(w):
        return jnp.broadcast_to(w[:, :, None, :], w.shape[:2] + (SUBLANES, w.shape[2]))

    return pl.pallas_call(
        _conv_kernel,
        grid=(b, t // tt),
        in_specs=[rows(2 * CONV_CH), halo(CONV_HALO, 2 * CONV_CH),
                  rows(3 * SC_CH), halo(SC_HALO, 3 * SC_CH),
                  _resident((CONV_K, SUBLANES, CONV_CH), layer), _resident((1, CONV_CH), layer),
                  _resident((1, CONV_CH), layer), _resident((1, CONV_CH), layer),
                  _resident((SC_K, SUBLANES, SC_CH), layer)],
        out_specs=[rows(CONV_CH), rows(SC_CH)],
        out_shape=[jax.ShapeDtypeStruct((b, t, CONV_CH), BF16),
                   jax.ShapeDtypeStruct((b, t, SC_CH), BF16)],
        scratch_shapes=[pltpu.VMEM((SUBLANES, CONV_HALO + tt, CONV_CH), F32),
                        pltpu.VMEM((SUBLANES, SC_HALO + tt, SC_CH), F32)],
        compiler_params=_params(2),
        name="conv",
    )(zc, zc, zs, zs, taps(dw), vec(dwb), vec(lng), vec(lnb), taps(scw))


def _t5_bucket_table():
    max_exact = N_BUCKETS // 2
    dist = np.maximum(np.arange(WINDOW)[:, None] + WINDOW - np.arange(2 * WINDOW)[None, :], 0)
    d = np.maximum(dist, 1).astype(np.float32)
    large = max_exact + (np.log(d / np.float32(max_exact)) / np.float32(
        math.log(MAX_DISTANCE / max_exact)) * np.float32(N_BUCKETS - max_exact)).astype(np.int32)
    large = np.minimum(large, N_BUCKETS - 1)
    return np.where(dist < max_exact, dist, large).astype(np.int32)


def _swa_kernel(rb_ref, sink_ref, shift_ref, bucket_ref, q_ref, k_ref, kh_ref, v_ref, vh_ref, o_ref,
                bias_ref, kk_ref, vv_ref, *, online):
    tq = q_ref.shape[1]
    first_step = (pl.program_id(0) == 0) & (pl.program_id(1) == 0)

    @pl.when(first_step)
    def _():
        bucket = bucket_ref[...]
        qi = lax.broadcasted_iota(jnp.int32, bucket.shape, 0) + WINDOW
        ki = lax.broadcasted_iota(jnp.int32, bucket.shape, 1)
        dist = qi - ki
        ok = (dist >= 0) & (dist < WINDOW)
        for h in range(SWA_Q_HEADS):
            bias = jnp.zeros(bucket.shape, F32)
            for bk in range(N_BUCKETS):
                bias = jnp.where(bucket == bk, rb_ref[bk, h], bias)
            bias_ref[h] = jnp.where(ok, bias - shift_ref[h], NEG_INF)

    kk_ref[0:WINDOW, :] = kh_ref[0]
    kk_ref[WINDOW:, :] = k_ref[0]
    vv_ref[0:WINDOW, :] = vh_ref[0]
    vv_ref[WINDOW:, :] = v_ref[0]
    seq_start = pl.program_id(1) == 0
    prev_half = lax.broadcasted_iota(jnp.int32, (WINDOW, 2 * WINDOW), 1) < WINDOW
    first = lax.broadcasted_iota(jnp.int32, (WINDOW, LANES), 1) < HEAD_DIM

    for sb in range(tq // WINDOW):
        r0 = sb * WINDOW
        kk = kk_ref[r0:r0 + 2 * WINDOW, :]
        vv = vv_ref[r0:r0 + 2 * WINDOW, :]
        for blk in range(SWA_Q_HEADS // 2):
            qb = q_ref[0, r0:r0 + WINDOW, blk * LANES:(blk + 1) * LANES]
            outs = []
            for half in range(2):
                h = SWA_Q_ORDER[2 * blk + half]
                keep = first if half == 0 else jnp.logical_not(first)
                q = jnp.where(keep, qb, jnp.zeros_like(qb))
                s = lax.dot_general(q, kk, (((1,), (1,)), ((), ())), preferred_element_type=F32)
                s = s + bias_ref[h]
                if sb == 0:
                    s = jnp.where(seq_start & prev_half, NEG_INF, s)
                sink = jnp.full((1, 1), sink_ref[h] - shift_ref[h], F32)
                if online:
                    m = jnp.maximum(jnp.max(s, axis=-1, keepdims=True), sink)
                    s, sink = s - m, sink - m
                pv = jnp.dot(jnp.exp2(s).astype(BF16), vv, preferred_element_type=F32)
                outs.append(pv[:, :LANES] / (pv[:, LANES:LANES + 1] + jnp.exp2(sink)))
            o_ref[0, r0:r0 + WINDOW, blk * LANES:(blk + 1) * LANES] = jnp.where(
                first, outs[0], outs[1]).astype(BF16)


def _swa(rel_bias2, sink2, shift, qa, ka, va, *, online):
    b, t, _ = qa.shape
    tq = SWA_TILE
    per = tq // WINDOW
    kw, vw = ka.shape[-1], va.shape[-1]
    smem = pl.BlockSpec(memory_space=pltpu.SMEM)

    def cur(width):
        return pl.BlockSpec((1, tq, width), lambda bi, ti: (bi, ti, 0))

    def halo(width):
        return pl.BlockSpec((1, WINDOW, width),
                            lambda bi, ti: (bi, jnp.maximum(ti * per - 1, 0), 0))

    return pl.pallas_call(
        functools.partial(_swa_kernel, online=online),
        grid=(b, t // tq),
        in_specs=[smem, smem, smem, _resident((WINDOW, 2 * WINDOW)),
                  cur(qa.shape[-1]), cur(kw), halo(kw), cur(vw), halo(vw)],
        out_specs=cur(qa.shape[-1]),
        out_shape=jax.ShapeDtypeStruct(qa.shape, BF16),
        scratch_shapes=[pltpu.VMEM((SWA_Q_HEADS, WINDOW, 2 * WINDOW), F32),
                        pltpu.VMEM((WINDOW + tq, kw), BF16),
                        pltpu.VMEM((WINDOW + tq, vw), BF16)],
        compiler_params=_params(2),
        name="swa_online" if online else "swa",
    )(rel_bias2, sink2, shift, jnp.asarray(_t5_bucket_table()), qa, ka, ka, va, va)


def _swa_shift(q_gain, k_gain, rel_bias, sink):
    qk = HEAD_DIM ** 0.5 * jnp.max(jnp.abs(q_gain)) * jnp.max(jnp.abs(k_gain))
    hi = jnp.maximum(qk + jnp.max(rel_bias, axis=0), sink)
    lo = jnp.maximum(rel_bias[0] - qk, sink)
    usable = jnp.all((hi - lo) * LOG2E < SWA_MAX_SPREAD)
    return jnp.where(usable, hi, 0.0), usable


def _fox_kernel(q_ref, k_ref, vt_ref, o_ref, *, online):
    tq = q_ref.shape[1]
    qi = pl.program_id(2)
    qp = q_ref[0]
    lane = lax.broadcasted_iota(jnp.int32, qp.shape, 1)
    zero = jnp.zeros_like(qp)
    x0 = LANES + pl.program_id(1) * (2 * FF_REP)
    q_heads = [
        jnp.where((lane < HEAD_DIM) | ((lane >= x0) & (lane < x0 + FF_REP)), qp, zero),
        jnp.where(((lane >= HEAD_DIM) & (lane < LANES)) |
                  ((lane >= x0 + FF_REP) & (lane < x0 + 2 * FF_REP)), qp, zero)]

    chains = [(hd, c) for c in range(tq // FOX_QC) for hd in range(2)]

    def run(blocks, carry):
        state = dict(enumerate(carry))
        items = []
        for j, diag in blocks:
            k0 = pl.multiple_of(j * FOX_TK, FOX_TK)
            ks = k_ref[0, pl.ds(k0, FOX_TK), :]
            vts = [vt_ref[0, hd * FOX_VROWS:(hd + 1) * FOX_VROWS, pl.ds(k0, FOX_TK)]
                   for hd in range(2)]
            key_lo = 0 if diag is None else diag * FOX_TK
            for ci, (hd, c) in enumerate(chains):
                if diag is None or key_lo <= (c + 1) * FOX_QC - 1:
                    masked = diag is not None and key_lo + FOX_TK - 1 > c * FOX_QC
                    items.append((ci, ks, vts[hd], key_lo if masked else None))

        def scores(item):
            ci, ks, _, mask_lo = item
            hd, c = chains[ci]
            st = lax.dot_general(ks, q_heads[hd][c * FOX_QC:(c + 1) * FOX_QC],
                                 (((1,), (1,)), ((), ())), preferred_element_type=F32)
            if mask_lo is not None:
                key = lax.broadcasted_iota(jnp.int32, st.shape, 0) + mask_lo
                qry = lax.broadcasted_iota(jnp.int32, st.shape, 1) + c * FOX_QC
                st = jnp.where(key <= qry, st, NEG_INF)
            return st

        def softmax(item, st):
            m, acc = state[item[0]]
            if not online:
                return m, acc, jnp.exp2(st).astype(BF16)
            m_new = jnp.maximum(m, jnp.max(st, axis=0, keepdims=True))
            return m_new, jnp.exp2(m - m_new) * acc, jnp.exp2(st - m_new).astype(BF16)

        def weighted(item, part):
            m_new, acc, p = part
            state[item[0]] = (m_new, acc + jnp.dot(item[2], p, preferred_element_type=F32))

        st_next = scores(items[0])
        part_prev = None
        for pos, item in enumerate(items):
            st_cur = st_next
            if pos + 1 < len(items):
                st_next = scores(items[pos + 1])
            part = softmax(item, st_cur)
            if part_prev is not None:
                weighted(items[pos - 1], part_prev)
            part_prev = part
        weighted(items[-1], part_prev)
        return tuple(state[ci] for ci in range(len(chains)))

    init = tuple((jnp.full((1, FOX_QC), NEG_INF, F32), jnp.zeros((FOX_VROWS, FOX_QC), F32))
                 for _ in chains)
    per = tq // FOX_TK
    assert per % FOX_UNROLL == 0
    res = lax.fori_loop(
        0, qi * (per // FOX_UNROLL),
        lambda it, c: run([(it * FOX_UNROLL + u, None) for u in range(FOX_UNROLL)], c), init)
    res = run([(qi * per + d, d) for d in range(per)], res)
    for c in range(tq // FOX_QC):
        heads = [acc[:HEAD_DIM] / acc[HEAD_DIM:HEAD_DIM + 1] for _, acc in res[2 * c:2 * c + 2]]
        out_t = jnp.concatenate(heads, axis=0)
        o_ref[0, c * FOX_QC:(c + 1) * FOX_QC, :] = out_t.T.astype(BF16)


def _fox(qf, kf, vft, *, online):
    b, t, _ = qf.shape
    assert FOX_TQ % FOX_TK == 0 and FOX_TQ % FOX_QC == 0
    return pl.pallas_call(
        functools.partial(_fox_kernel, online=online),
        grid=(b, FOX_PAIRS, t // FOX_TQ),
        in_specs=[pl.BlockSpec((1, FOX_TQ, 2 * LANES), lambda bi, pi, qi: (bi, qi, pi)),
                  pl.BlockSpec((1, t, 2 * LANES), lambda bi, pi, qi: (bi, 0, pi)),
                  pl.BlockSpec((1, 2 * FOX_VROWS, t), lambda bi, pi, qi: (bi, pi, 0))],
        out_specs=pl.BlockSpec((1, FOX_TQ, LANES), lambda bi, pi, qi: (bi, qi, pi)),
        out_shape=jax.ShapeDtypeStruct((b, t, FOX_HEADS * HEAD_DIM), BF16),
        compiler_params=_params(3),
        name="fox_online" if online else "fox",
    )(qf, kf, vft)


def _fox_shift(q_gain, k_gain):
    bound = (LOG2E * HEAD_DIM ** 0.5) * jnp.max(jnp.abs(q_gain)) * jnp.max(jnp.abs(k_gain))
    usable = bound < FOX_MAX_SHIFT
    return jnp.where(usable, bound, 0.0), usable


def _merge_kernel(x_ref, uc_ref, us_ref, oa_ref, of_ref, wb_ref, wgate_ref, wout_ref, o_ref):
    x = x_ref[0]
    half_inv = 0.5 * lax.rsqrt(jnp.mean(x * x, axis=-1, keepdims=True) + EPS)
    xb = x.astype(BF16)
    twice_merged = jnp.zeros(x.shape, F32)
    for i, u_ref in enumerate((uc_ref, us_ref, oa_ref, of_ref)):
        p = jnp.dot(u_ref[0], wb_ref[i], preferred_element_type=F32)
        half_gate = lax.dot_general(xb, wgate_ref[i * D_MODEL:(i + 1) * D_MODEL, :],
                                    (((1,), (1,)), ((), ())), preferred_element_type=F32) * half_inv
        twice_merged = twice_merged + (p + jnp.tanh(half_gate) * p)
    merged = (0.5 * twice_merged).astype(BF16)
    o_ref[0] = x + jnp.dot(merged, wout_ref[...], preferred_element_type=F32)


def _merge(x, layer, uc, us, oa, of, wb, wgate, wout):
    b, t, d = x.shape
    tm = MERGE_TILE

    def rows(width):
        return pl.BlockSpec((1, tm, width), lambda bi, ti: (bi, ti, 0))

    return pl.pallas_call(
        _merge_kernel,
        grid=(b, t // tm),
        in_specs=[rows(d), rows(CONV_CH), rows(SC_CH),
                  rows(SWA_Q_HEADS * HEAD_DIM), rows(FOX_HEADS * HEAD_DIM),
                  _resident(wb.shape[1:], layer), _resident(wgate.shape[1:], layer),
                  _resident(wout.shape[1:], layer)],
        out_specs=rows(d),
        out_shape=jax.ShapeDtypeStruct((b, t, d), F32),
        compiler_params=_params(2),
        name="merge",
    )(x, uc, us, oa, of, wb, wgate, wout)


def _fgate_cols(ff, axis=-1):
    rep = jnp.repeat(ff, FF_REP, axis=axis)
    pad = [(0, 0)] * ff.ndim
    pad[axis] = (0, LANES - FOX_HEADS * FF_REP)
    return jnp.pad(rep, pad)


def _prep_proj(w_in, mix_gain):
    wt = jnp.transpose(w_in, (2, 0, 1))
    aq0 = COL_AQ
    ff0 = COL_FF
    heads = [wt[aq0 + h * HEAD_DIM: aq0 + (h + 1) * HEAD_DIM] for h in SWA_Q_ORDER]
    w_proj = jnp.concatenate([wt[:aq0]] + heads + [wt[COL_AK:ff0],
                                                   _fgate_cols(wt[ff0:ff0 + FOX_HEADS], axis=0)], axis=0)
    w_gate = wt[ff0 + FOX_HEADS:]
    return ((jnp.transpose(w_proj, (1, 0, 2)) * mix_gain[:, None, :]).astype(BF16),
            (jnp.transpose(w_gate, (1, 0, 2)) * mix_gain[:, None, :]).astype(BF16))


def _prep_ffn(norm_gain, w_gate, w_up, w_down):
    g = norm_gain[:, :, None]
    return (g * w_gate).astype(BF16), (g * w_up).astype(BF16), w_down.astype(BF16)


def _prep_qk_gains(aq, ak, fq, fk):
    scale = HEAD_DIM ** -0.5 * LOG2E
    ones = jnp.ones((aq.shape[0], SWA_KV_HEADS * HEAD_DIM), F32)
    row = jnp.concatenate([jnp.tile(aq * scale, (1, SWA_Q_HEADS)), jnp.tile(ak, (1, SWA_KV_HEADS)),
                           ones, jnp.tile(fq * scale, (1, FOX_HEADS)),
                           jnp.tile(fk, (1, FOX_HEADS))],
                          axis=-1)
    return row[:, None, :]


def kernel(x, rel_bias, ffn1_norm, ffn1_w_gate, ffn1_w_up, ffn1_w_down, mix_norm, w_in, b_forget, conf_dw, conf_dw_b, conf_ln_g, conf_ln_b, conf_w_out, sc_conv, sc_w_out, swa_q_norm, swa_k_norm, swa_sink, swa_w_o, fox_q_norm, fox_k_norm, fox_w_o, w_out, ffn2_norm, ffn2_w_gate, ffn2_w_up, ffn2_w_down):
    b, t, d = x.shape
    depth = w_in.shape[0]

    w_proj, w_gate = _prep_proj(w_in, mix_norm)
    qk_gains = _prep_qk_gains(swa_q_norm, swa_k_norm, fox_q_norm, fox_k_norm)
    bf = _fgate_cols(b_forget)[:, None, :]
    swa_rows = jnp.concatenate([swa_w_o[:, h * HEAD_DIM:(h + 1) * HEAD_DIM] for h in SWA_Q_ORDER],
                               axis=1)
    w_branch = jnp.stack([conf_w_out, sc_w_out, swa_rows, fox_w_o], axis=1).astype(BF16)
    w_out_b = w_out.astype(BF16)
    ffn1 = _prep_ffn(ffn1_norm, ffn1_w_gate, ffn1_w_up, ffn1_w_down)
    ffn2 = _prep_ffn(ffn2_norm, ffn2_w_gate, ffn2_w_up, ffn2_w_down)

    for l in range(depth):
        x = _ffn(x.reshape(b * t, d), l, *ffn1).reshape(b, t, d)
        shift, shifted = _fox_shift(fox_q_norm[l], fox_k_norm[l])
        zc, zs, qa, ka, va, qf, kf, vft = _inproj(x, l, w_proj, qk_gains, bf, shift)
        uc, us = _conv(zc, zs, l, conf_dw, conf_dw_b, conf_ln_g, conf_ln_b, sc_conv)
        a_shift, a_shifted = _swa_shift(swa_q_norm[l], swa_k_norm[l], rel_bias, swa_sink[l])
        oa = lax.cond(a_shifted, functools.partial(_swa, online=False),
                      functools.partial(_swa, online=True),
                      rel_bias * LOG2E, swa_sink[l] * LOG2E, a_shift * LOG2E, qa, ka, va)
        of = lax.cond(shifted, functools.partial(_fox, online=False),
                      functools.partial(_fox, online=True), qf, kf, vft)
        x = _merge(x, l, uc, us, oa, of, w_branch, w_gate, w_out_b)
        x = _ffn(x.reshape(b * t, d), l, *ffn2).reshape(b, t, d)
    return x
```

```python
import functools
import math

import numpy as np
import jax
import jax.numpy as jnp
from jax import lax
from jax.experimental import pallas as pl
from jax.experimental.pallas import tpu as pltpu

F32 = jnp.float32
BF16 = jnp.bfloat16

D_MODEL = 1024
HEAD_DIM = 64
CONV_CH = 256
CONV_K = 31
SC_CH = 256
SC_K = 3
SWA_Q_HEADS = 4
SWA_KV_HEADS = 2
WINDOW = 128
FOX_HEADS = 4
N_BUCKETS = 32
MAX_DISTANCE = 128
D_FF = 2816
N_BRANCH = 4
EPS = 1e-6
NEG_INF = -1e30
LOG2E = math.log2(math.e)

LANES = 128
SUBLANES = 8
VMEM_LIMIT = 56 * 1024 * 1024

COL_CONV = 0
COL_SC = COL_CONV + 2 * CONV_CH
COL_AQ = COL_SC + 3 * SC_CH
COL_AK = COL_AQ + SWA_Q_HEADS * HEAD_DIM
COL_AV = COL_AK + SWA_KV_HEADS * HEAD_DIM
COL_FQ = COL_AV + SWA_KV_HEADS * HEAD_DIM
COL_FK = COL_FQ + FOX_HEADS * HEAD_DIM
COL_FV = COL_FK + FOX_HEADS * HEAD_DIM
COL_FF = COL_FV + FOX_HEADS * HEAD_DIM
FOX_PAIRS = FOX_HEADS // 2
FF_REP = 9
PROJ_COLS = COL_FF + LANES
SWA_Q_ORDER = (0, 2, 1, 3)

ROW_TILE = 1024
CONV_HALO = 32
SC_HALO = 8
CONV_CHUNK = 64
INPROJ_TILE = 1024
MERGE_TILE = 1024
CONV_TILE = 1024
SWA_TILE = 2048
FOX_TQ = 2048
FOX_TK = 512
FOX_QC = 256
FOX_VROWS = 80
FOX_UNROLL = 4
SWA_MAX_SPREAD = 100.0
FOX_MAX_SHIFT = 56.0


def _params(n_axes, flags=None):
    return pltpu.CompilerParams(dimension_semantics=("arbitrary",) * n_axes,
                                vmem_limit_bytes=VMEM_LIMIT, flags=flags)


def _resident(shape, layer=None):
    if layer is None:
        return pl.BlockSpec(shape, lambda *_: (0,) * len(shape), pipeline_mode=pl.Buffered(1))
    return pl.BlockSpec((None,) + tuple(shape), lambda *_: (layer,) + (0,) * len(shape),
                        pipeline_mode=pl.Buffered(1))


def _sigmoid(x):
    return 1.0 / (1.0 + jnp.exp2(x * -LOG2E))


def _ffn_kernel(x_ref, wg_ref, wu_ref, wd_ref, o_ref):
    x = x_ref[...]
    inv = lax.rsqrt(jnp.mean(x * x, axis=-1, keepdims=True) + EPS)
    xb = x.astype(BF16)
    half_gate = jnp.dot(xb, wg_ref[...], preferred_element_type=F32) * (0.5 * inv)
    up = jnp.dot(xb, wu_ref[...], preferred_element_type=F32)
    act = ((half_gate + half_gate * jnp.tanh(half_gate)) * up).astype(BF16)
    y = jnp.dot(act, wd_ref[...], preferred_element_type=F32)
    o_ref[...] = x + (0.5 * inv) * y


def _ffn(x2d, layer, wg, wu, wd):
    n, d = x2d.shape
    row = pl.BlockSpec((ROW_TILE, d), lambda i: (i, 0))
    return pl.pallas_call(
        _ffn_kernel,
        grid=(n // ROW_TILE,),
        in_specs=[row, _resident(wg.shape[1:], layer), _resident(wu.shape[1:], layer),
                  _resident(wd.shape[1:], layer)],
        out_specs=row,
        out_shape=jax.ShapeDtypeStruct((n, d), F32),
        compiler_params=_params(1),
        name="ffn",
    )(x2d, wg, wu, wd)


def _split3(v):
    hi = v.astype(BF16).astype(F32)
    r = v - hi
    mid = r.astype(BF16).astype(F32)
    return hi, mid, r - mid


def _pair_rms(zb, first, eps):
    sq = zb * zb
    s0 = jnp.sum(jnp.where(first, sq, 0.0), axis=-1, keepdims=True)
    s1 = jnp.sum(jnp.where(first, 0.0, sq), axis=-1, keepdims=True)
    inv = jnp.where(first, lax.rsqrt(s0 * (1.0 / HEAD_DIM) + eps),
                    lax.rsqrt(s1 * (1.0 / HEAD_DIM) + eps))
    return zb * inv


def _inproj_kernel(x_ref, w_ref, qkg_ref, bf_ref, shift_ref,
                   zc_ref, zs_ref, qa_ref, ka_ref, va_ref, qf_ref, kf_ref, vft_ref, carry_ref):
    tm = x_ref.shape[1]

    @pl.when(pl.program_id(1) == 0)
    def _():
        carry_ref[...] = jnp.zeros_like(carry_ref)

    x = x_ref[0]
    ms = jnp.mean(x * x, axis=-1, keepdims=True) + EPS
    inv = lax.rsqrt(ms)
    qk_eps = EPS * ms
    h = x.astype(BF16)
    lane = lax.broadcasted_iota(jnp.int32, (tm, LANES), 1)
    first = lane < HEAD_DIM
    slot = lane % FF_REP
    used = lane < FOX_HEADS * FF_REP
    ext = {}

    def normed(z, col0, i):
        g0 = col0 - COL_AQ + i * LANES
        return _pair_rms(z[:, i * LANES:(i + 1) * LANES], first, qk_eps) * qkg_ref[:, g0:g0 + LANES]

    def forget_terms(z):
        ff = z * inv + bf_ref[...]
        logf = jnp.minimum(ff, 0.0) - jnp.log1p(jnp.exp(-jnp.abs(ff)))
        rows = lax.broadcasted_iota(jnp.int32, logf.shape, 0)
        cum = logf
        step = 1
        while step < tm:
            cum = cum + jnp.where(rows >= step, pltpu.roll(cum, step, axis=0), 0.0)
            step *= 2
        cum = cum + carry_ref[...]
        carry_ref[...] = cum[tm - 1:tm, :]
        cum = cum * LOG2E
        s_hi, s_mid, s_lo = _split3(shift_ref[...])
        hi, mid, lo = _split3(cum)
        q_extra = jnp.where(slot == 0, hi, jnp.where(slot == 1, mid, jnp.where(
            slot == 2, lo, jnp.where(slot < 6, 1.0, jnp.where(
                slot == 6, -s_hi, jnp.where(slot == 7, -s_mid, -s_lo))))))
        k_extra = jnp.where(slot == 3, -hi, jnp.where(slot == 4, -mid, jnp.where(
            slot == 5, -lo, 1.0)))
        ext["q"] = jnp.where(used, q_extra, 0.0).astype(BF16)
        ext["k"] = jnp.where(used, k_extra, 0.0).astype(BF16)

    def fox_qk(z):
        for p in range(FOX_PAIRS):
            base = 2 * p * LANES
            qf_ref[0, :, base:base + LANES] = normed(z, COL_FQ, p).astype(BF16)
            qf_ref[0, :, base + LANES:base + 2 * LANES] = ext["q"]
            kf_ref[0, :, base:base + LANES] = normed(z, COL_FQ, FOX_PAIRS + p).astype(BF16)
            kf_ref[0, :, base + LANES:base + 2 * LANES] = ext["k"]

    def swa_qk(z):
        n_q = SWA_Q_HEADS // 2
        for i in range(n_q):
            qa_ref[0, :, i * LANES:(i + 1) * LANES] = normed(z, COL_AQ, i).astype(BF16)
        ka_ref[0] = normed(z, COL_AQ, n_q).astype(BF16)

    def fox_v(z):
        vt = (z * inv).T
        ones_row = (lax.broadcasted_iota(jnp.int32, (FOX_VROWS - HEAD_DIM, tm), 0) == 0).astype(F32)
        for i in range(FOX_HEADS):
            vft_ref[0, i * FOX_VROWS:(i + 1) * FOX_VROWS, :] = jnp.concatenate(
                [vt[i * HEAD_DIM:(i + 1) * HEAD_DIM], ones_row], axis=0).astype(BF16)

    def swa_v(z):
        va_ref[0, :, 0:LANES] = (z * inv).astype(BF16)
        va_ref[0, :, LANES:2 * LANES] = (lane == 0).astype(BF16)

    def conv_in(z):
        zc_ref[0] = z * inv

    def sc_in(z):
        zs_ref[0] = z * inv

    z = lax.dot_general(h, w_ref[...], (((1,), (1,)), ((), ())),
                        preferred_element_type=F32)
    for c0, c1, epilogue in ((COL_FF, PROJ_COLS, forget_terms), (COL_FQ, COL_FV, fox_qk),
                             (COL_AQ, COL_AV, swa_qk), (COL_FV, COL_FF, fox_v),
                             (COL_AV, COL_FQ, swa_v), (COL_CONV, COL_SC, conv_in),
                             (COL_SC, COL_AQ, sc_in)):
        epilogue(z[:, c0:c1])


def _inproj(x, layer, w, qkg, bf, shift):
    b, t, d = x.shape
    tm = INPROJ_TILE

    def rows(width):
        return pl.BlockSpec((1, tm, width), lambda bi, ti: (bi, ti, 0))

    def oshape(width, dtype=BF16):
        return jax.ShapeDtypeStruct((b, t, width), dtype)

    n_v = FOX_HEADS * FOX_VROWS
    return pl.pallas_call(
        _inproj_kernel,
        grid=(b, t // tm),
        in_specs=[rows(d), _resident(w.shape[1:], layer),
                  _resident(qkg.shape[1:], layer), _resident(bf.shape[1:], layer),
                  _resident((1, LANES))],
        out_specs=[rows(2 * CONV_CH), rows(3 * SC_CH),
                   rows(SWA_Q_HEADS * HEAD_DIM), rows(SWA_KV_HEADS * HEAD_DIM),
                   rows(2 * LANES),
                   rows(2 * FOX_PAIRS * LANES), rows(2 * FOX_PAIRS * LANES),
                   pl.BlockSpec((1, n_v, tm), lambda bi, ti: (bi, 0, ti))],
        out_shape=[oshape(2 * CONV_CH, F32), oshape(3 * SC_CH, F32),
                   oshape(SWA_Q_HEADS * HEAD_DIM), oshape(SWA_KV_HEADS * HEAD_DIM),
                   oshape(2 * LANES),
                   oshape(2 * FOX_PAIRS * LANES), oshape(2 * FOX_PAIRS * LANES),
                   jax.ShapeDtypeStruct((b, n_v, t), BF16)],
        scratch_shapes=[pltpu.VMEM((1, LANES), F32)],
        compiler_params=_params(2),
        name="inproj",
    )(x, w, qkg, bf, jnp.full((1, LANES), shift, F32))


def _fill_shifted(ext_ref, base, n_taps):
    n = ext_ref.shape[1] - SUBLANES
    for r in sorted({(base + k) % SUBLANES for k in range(n_taps)} - {0}):
        ext_ref[r, 0:n, :] = ext_ref[0, r:r + n, :]


def _taps(ext_ref, w_ref, r0, base, n_taps, acc):
    groups = CONV_CHUNK // SUBLANES
    acc = acc.reshape(groups, SUBLANES, acc.shape[-1])
    for k in range(n_taps):
        r = (base + k) % SUBLANES
        start = r0 + base + k - r
        u = ext_ref[r, start:start + CONV_CHUNK, :]
        acc = acc + w_ref[k][None] * u.reshape(groups, SUBLANES, u.shape[-1])
    return acc.reshape(CONV_CHUNK, acc.shape[-1])


def _conv_kernel(zc_ref, zch_ref, zs_ref, zsh_ref, dw_ref, dwb_ref, lng_ref, lnb_ref, scw_ref,
                 uc_ref, us_ref, ext_ref, ext2_ref):
    tt = zc_ref.shape[1]
    keep = (pl.program_id(1) > 0).astype(F32)

    def glu(v):
        return v[:, :CONV_CH] * _sigmoid(v[:, CONV_CH:])

    ext_ref[0, 0:CONV_HALO, :] = glu(zch_ref[0]) * keep
    ext_ref[0, CONV_HALO:, :] = glu(zc_ref[0])
    zsh = zsh_ref[0]
    ext2_ref[0, 0:SC_HALO, :] = zsh[:, SC_CH:2 * SC_CH] * zsh[:, 2 * SC_CH:] * keep
    zs = zs_ref[0]
    ext2_ref[0, SC_HALO:, :] = zs[:, SC_CH:2 * SC_CH] * zs[:, 2 * SC_CH:]
    conv_base = CONV_HALO - (CONV_K - 1)
    sc_base = SC_HALO - (SC_K - 1)
    _fill_shifted(ext_ref, conv_base, CONV_K)
    _fill_shifted(ext2_ref, sc_base, SC_K)

    for r0 in range(0, tt, CONV_CHUNK):
        acc = _taps(ext_ref, dw_ref, r0, conv_base, CONV_K,
                    jnp.broadcast_to(dwb_ref[...], (CONV_CHUNK, CONV_CH)))
        mu = jnp.mean(acc, axis=-1, keepdims=True)
        cen = acc - mu
        var = jnp.mean(cen * cen, axis=-1, keepdims=True)
        y = cen * lax.rsqrt(var + EPS) * lng_ref[...] + lnb_ref[...]
        uc_ref[0, r0:r0 + CONV_CHUNK, :] = (y * _sigmoid(y)).astype(BF16)

        acc2 = _taps(ext2_ref, scw_ref, r0, sc_base, SC_K, jnp.zeros((CONV_CHUNK, SC_CH), F32))
        us_ref[0, r0:r0 + CONV_CHUNK, :] = (zs[r0:r0 + CONV_CHUNK, :SC_CH] * acc2).astype(BF16)


def _conv(zc, zs, layer, dw, dwb, lng, lnb, scw):
    b, t, _ = zc.shape
    tt = CONV_TILE

    def rows(width):
        return pl.BlockSpec((1, tt, width), lambda bi, ti: (bi, ti, 0))

    def halo(rows_, width):
        per = tt // rows_
        return pl.BlockSpec((1, rows_, width),
                            lambda bi, ti: (bi, jnp.maximum(ti * per - 1, 0), 0))

    def vec(a):
        return a.reshape(a.shape[0], 1, a.shape[1])

    def taps(w):
        return jnp.broadcast_to(w[:, :, None, :], w.shape[:2] + (SUBLANES, w.shape[2]))

    return pl.pallas_call(
        _conv_kernel,
        grid=(b, t // tt),
        in_specs=[rows(2 * CONV_CH), halo(CONV_HALO, 2 * CONV_CH),
                  rows(3 * SC_CH), halo(SC_HALO, 3 * SC_CH),
                  _resident((CONV_K, SUBLANES, CONV_CH), layer), _resident((1, CONV_CH), layer),
                  _resident((1, CONV_CH), layer), _resident((1, CONV_CH), layer),
                  _resident((SC_K, SUBLANES, SC_CH), layer)],
        out_specs=[rows(CONV_CH), rows(SC_CH)],
        out_shape=[jax.ShapeDtypeStruct((b, t, CONV_CH), BF16),
                   jax.ShapeDtypeStruct((b, t, SC_CH), BF16)],
        scratch_shapes=[pltpu.VMEM((SUBLANES, CONV_HALO + tt, CONV_CH), F32),
                        pltpu.VMEM((SUBLANES, SC_HALO + tt, SC_CH), F32)],
        compiler_params=_params(2),
        name="conv",
    )(zc, zc, zs, zs, taps(dw), vec(dwb), vec(lng), vec(lnb), taps(scw))


def _t5_bucket_table():
    max_exact = N_BUCKETS // 2
    dist = np.maximum(np.arange(WINDOW)[:, None] + WINDOW - np.arange(2 * WINDOW)[None, :], 0)
    d = np.maximum(dist, 1).astype(np.float32)
    large = max_exact + (np.log(d / np.float32(max_exact)) / np.float32(
        math.log(MAX_DISTANCE / max_exact)) * np.float32(N_BUCKETS - max_exact)).astype(np.int32)
    large = np.minimum(large, N_BUCKETS - 1)
    return np.where(dist < max_exact, dist, large).astype(np.int32)


def _swa_kernel(rb_ref, sink_ref, shift_ref, bucket_ref, q_ref, k_ref, kh_ref, v_ref, vh_ref, o_ref,
                bias_ref, kk_ref, vv_ref, *, online):
    tq = q_ref.shape[1]
    first_step = (pl.program_id(0) == 0) & (pl.program_id(1) == 0)

    @pl.when(first_step)
    def _():
        bucket = bucket_ref[...]
        qi = lax.broadcasted_iota(jnp.int32, bucket.shape, 0) + WINDOW
        ki = lax.broadcasted_iota(jnp.int32, bucket.shape, 1)
        dist = qi - ki
        ok = (dist >= 0) & (dist < WINDOW)
        for h in range(SWA_Q_HEADS):
            bias = jnp.zeros(bucket.shape, F32)
            for bk in range(N_BUCKETS):
                bias = jnp.where(bucket == bk, rb_ref[bk, h], bias)
            bias_ref[h] = jnp.where(ok, bias - shift_ref[h], NEG_INF)

    kk_ref[0:WINDOW, :] = kh_ref[0]
    kk_ref[WINDOW:, :] = k_ref[0]
    vv_ref[0:WINDOW, :] = vh_ref[0]
    vv_ref[WINDOW:, :] = v_ref[0]
    seq_start = pl.program_id(1) == 0
    prev_half = lax.broadcasted_iota(jnp.int32, (WINDOW, 2 * WINDOW), 1) < WINDOW
    first = lax.broadcasted_iota(jnp.int32, (WINDOW, LANES), 1) < HEAD_DIM

    for sb in range(tq // WINDOW):
        r0 = sb * WINDOW
        kk = kk_ref[r0:r0 + 2 * WINDOW, :]
        vv = vv_ref[r0:r0 + 2 * WINDOW, :]
        for blk in range(SWA_Q_HEADS // 2):
            qb = q_ref[0, r0:r0 + WINDOW, blk * LANES:(blk + 1) * LANES]
            outs = []
            for half in range(2):
                h = SWA_Q_ORDER[2 * blk + half]
                keep = first if half == 0 else jnp.logical_not(first)
                q = jnp.where(keep, qb, jnp.zeros_like(qb))
                s = lax.dot_general(q, kk, (((1,), (1,)), ((), ())), preferred_element_type=F32)
                s = s + bias_ref[h]
                if sb == 0:
                    s = jnp.where(seq_start & prev_half, NEG_INF, s)
                sink = jnp.full((1, 1), sink_ref[h] - shift_ref[h], F32)
                if online:
                    m = jnp.maximum(jnp.max(s, axis=-1, keepdims=True), sink)
                    s, sink = s - m, sink - m
                pv = jnp.dot(jnp.exp2(s).astype(BF16), vv, preferred_element_type=F32)
                outs.append(pv[:, :LANES] / (pv[:, LANES:LANES + 1] + jnp.exp2(sink)))
            o_ref[0, r0:r0 + WINDOW, blk * LANES:(blk + 1) * LANES] = jnp.where(
                first, outs[0], outs[1]).astype(BF16)


def _swa(rel_bias2, sink2, shift, qa, ka, va, *, online):
    b, t, _ = qa.shape
    tq = SWA_TILE
    per = tq // WINDOW
    kw, vw = ka.shape[-1], va.shape[-1]
    smem = pl.BlockSpec(memory_space=pltpu.SMEM)

    def cur(width):
        return pl.BlockSpec((1, tq, width), lambda bi, ti: (bi, ti, 0))

    def halo(width):
        return pl.BlockSpec((1, WINDOW, width),
                            lambda bi, ti: (bi, jnp.maximum(ti * per - 1, 0), 0))

    return pl.pallas_call(
        functools.partial(_swa_kernel, online=online),
        grid=(b, t // tq),
        in_specs=[smem, smem, smem, _resident((WINDOW, 2 * WINDOW)),
                  cur(qa.shape[-1]), cur(kw), halo(kw), cur(vw), halo(vw)],
        out_specs=cur(qa.shape[-1]),
        out_shape=jax.ShapeDtypeStruct(qa.shape, BF16),
        scratch_shapes=[pltpu.VMEM((SWA_Q_HEADS, WINDOW, 2 * WINDOW), F32),
                        pltpu.VMEM((WINDOW + tq, kw), BF16),
                        pltpu.VMEM((WINDOW + tq, vw), BF16)],
        compiler_params=_params(2),
        name="swa_online" if online else "swa",
    )(rel_bias2, sink2, shift, jnp.asarray(_t5_bucket_table()), qa, ka, ka, va, va)


def _swa_shift(q_gain, k_gain, rel_bias, sink):
    qk = HEAD_DIM ** 0.5 * jnp.max(jnp.abs(q_gain)) * jnp.max(jnp.abs(k_gain))
    hi = jnp.maximum(qk + jnp.max(rel_bias, axis=0), sink)
    lo = jnp.maximum(rel_bias[0] - qk, sink)
    usable = jnp.all((hi - lo) * LOG2E < SWA_MAX_SPREAD)
    return jnp.where(usable, hi, 0.0), usable


def _fox_kernel(q_ref, k_ref, vt_ref, o_ref, *, online):
    tq = q_ref.shape[1]
    qi = pl.program_id(2)
    qp = q_ref[0]
    lane = lax.broadcasted_iota(jnp.int32, qp.shape, 1)
    zero = jnp.zeros_like(qp)
    x0 = LANES + pl.program_id(1) * (2 * FF_REP)
    q_heads = [
        jnp.where((lane < HEAD_DIM) | ((lane >= x0) & (lane < x0 + FF_REP)), qp, zero),
        jnp.where(((lane >= HEAD_DIM) & (lane < LANES)) |
                  ((lane >= x0 + FF_REP) & (lane < x0 + 2 * FF_REP)), qp, zero)]

    chains = [(hd, c) for c in range(tq // FOX_QC) for hd in range(2)]

    def run(blocks, carry):
        state = dict(enumerate(carry))
        items = []
        for j, diag in blocks:
            k0 = pl.multiple_of(j * FOX_TK, FOX_TK)
            ks = k_ref[0, pl.ds(k0, FOX_TK), :]
            vts = [vt_ref[0, hd * FOX_VROWS:(hd + 1) * FOX_VROWS, pl.ds(k0, FOX_TK)]
                   for hd in range(2)]
            key_lo = 0 if diag is None else diag * FOX_TK
            for ci, (hd, c) in enumerate(chains):
                if diag is None or key_lo <= (c + 1) * FOX_QC - 1:
                    masked = diag is not None and key_lo + FOX_TK - 1 > c * FOX_QC
                    items.append((ci, ks, vts[hd], key_lo if masked else None))

        def scores(item):
            ci, ks, _, mask_lo = item
            hd, c = chains[ci]
            st = lax.dot_general(ks, q_heads[hd][c * FOX_QC:(c + 1) * FOX_QC],
                                 (((1,), (1,)), ((), ())), preferred_element_type=F32)
            if mask_lo is not None:
                key = lax.broadcasted_iota(jnp.int32, st.shape, 0) + mask_lo
                qry = lax.broadcasted_iota(jnp.int32, st.shape, 1) + c * FOX_QC
                st = jnp.where(key <= qry, st, NEG_INF)
            return st

        def softmax(item, st):
            m, acc = state[item[0]]
            if not online:
                return m, acc, jnp.exp2(st).astype(BF16)
            m_new = jnp.maximum(m, jnp.max(st, axis=0, keepdims=True))
            return m_new, jnp.exp2(m - m_new) * acc, jnp.exp2(st - m_new).astype(BF16)

        def weighted(item, part):
            m_new, acc, p = part
            state[item[0]] = (m_new, acc + jnp.dot(item[2], p, preferred_element_type=F32))

        st_next = scores(items[0])
        part_prev = None
        for pos, item in enumerate(items):
            st_cur = st_next
            if pos + 1 < len(items):
                st_next = scores(items[pos + 1])
            part = softmax(item, st_cur)
            if part_prev is not None:
                weighted(items[pos - 1], part_prev)
            part_prev = part
        weighted(items[-1], part_prev)
        return tuple(state[ci] for ci in range(len(chains)))

    init = tuple((jnp.full((1, FOX_QC), NEG_INF, F32), jnp.zeros((FOX_VROWS, FOX_QC), F32))
                 for _ in chains)
    per = tq // FOX_TK
    assert per % FOX_UNROLL == 0
    res = lax.fori_loop(
        0, qi * (per // FOX_UNROLL),
        lambda it, c: run([(it * FOX_UNROLL + u, None) for u in range(FOX_UNROLL)], c), init)
    res = run([(qi * per + d, d) for d in range(per)], res)
    for c in range(tq // FOX_QC):
        heads = [acc[:HEAD_DIM] / acc[HEAD_DIM:HEAD_DIM + 1] for _, acc in res[2 * c:2 * c + 2]]
        out_t = jnp.concatenate(heads, axis=0)
        o_ref[0, c * FOX_QC:(c + 1) * FOX_QC, :] = out_t.T.astype(BF16)


def _fox(qf, kf, vft, *, online):
    b, t, _ = qf.shape
    assert FOX_TQ % FOX_TK == 0 and FOX_TQ % FOX_QC == 0
    return pl.pallas_call(
        functools.partial(_fox_kernel, online=online),
        grid=(b, FOX_PAIRS, t // FOX_TQ),
        in_specs=[pl.BlockSpec((1, FOX_TQ, 2 * LANES), lambda bi, pi, qi: (bi, qi, pi)),
                  pl.BlockSpec((1, t, 2 * LANES), lambda bi, pi, qi: (bi, 0, pi)),
                  pl.BlockSpec((1, 2 * FOX_VROWS, t), lambda bi, pi, qi: (bi, pi, 0))],
        out_specs=pl.BlockSpec((1, FOX_TQ, LANES), lambda bi, pi, qi: (bi, qi, pi)),
        out_shape=jax.ShapeDtypeStruct((b, t, FOX_HEADS * HEAD_DIM), BF16),
        compiler_params=_params(3),
        name="fox_online" if online else "fox",
    )(qf, kf, vft)


def _fox_shift(q_gain, k_gain):
    bound = (LOG2E * HEAD_DIM ** 0.5) * jnp.max(jnp.abs(q_gain)) * jnp.max(jnp.abs(k_gain))
    usable = bound < FOX_MAX_SHIFT
    return jnp.where(usable, bound, 0.0), usable


def _merge_kernel(x_ref, uc_ref, us_ref, oa_ref, of_ref, wb_ref, wgate_ref, wout_ref, o_ref):
    x = x_ref[0]
    half_inv = 0.5 * lax.rsqrt(jnp.mean(x * x, axis=-1, keepdims=True) + EPS)
    xb = x.astype(BF16)
    twice_merged = jnp.zeros(x.shape, F32)
    for i, u_ref in enumerate((uc_ref, us_ref, oa_ref, of_ref)):
        p = jnp.dot(u_ref[0], wb_ref[i], preferred_element_type=F32)
        half_gate = lax.dot_general(xb, wgate_ref[i * D_MODEL:(i + 1) * D_MODEL, :],
                                    (((1,), (1,)), ((), ())), preferred_element_type=F32) * half_inv
        twice_merged = twice_merged + (p + jnp.tanh(half_gate) * p)
    merged = (0.5 * twice_merged).astype(BF16)
    o_ref[0] = x + jnp.dot(merged, wout_ref[...], preferred_element_type=F32)


def _merge(x, layer, uc, us, oa, of, wb, wgate, wout):
    b, t, d = x.shape
    tm = MERGE_TILE

    def rows(width):
        return pl.BlockSpec((1, tm, width), lambda bi, ti: (bi, ti, 0))

    return pl.pallas_call(
        _merge_kernel,
        grid=(b, t // tm),
        in_specs=[rows(d), rows(CONV_CH), rows(SC_CH),
                  rows(SWA_Q_HEADS * HEAD_DIM), rows(FOX_HEADS * HEAD_DIM),
                  _resident(wb.shape[1:], layer), _resident(wgate.shape[1:], layer),
                  _resident(wout.shape[1:], layer)],
        out_specs=rows(d),
        out_shape=jax.ShapeDtypeStruct((b, t, d), F32),
        compiler_params=_params(2),
        name="merge",
    )(x, uc, us, oa, of, wb, wgate, wout)


def _fgate_cols(ff, axis=-1):
    rep = jnp.repeat(ff, FF_REP, axis=axis)
    pad = [(0, 0)] * ff.ndim
    pad[axis] = (0, LANES - FOX_HEADS * FF_REP)
    return jnp.pad(rep, pad)


def _prep_proj(w_in, mix_gain):
    wt = jnp.transpose(w_in, (2, 0, 1))
    aq0 = COL_AQ
    ff0 = COL_FF
    heads = [wt[aq0 + h * HEAD_DIM: aq0 + (h + 1) * HEAD_DIM] for h in SWA_Q_ORDER]
    w_proj = jnp.concatenate([wt[:aq0]] + heads + [wt[COL_AK:ff0],
                                                   _fgate_cols(wt[ff0:ff0 + FOX_HEADS], axis=0)], axis=0)
    w_gate = wt[ff0 + FOX_HEADS:]
    return ((jnp.transpose(w_proj, (1, 0, 2)) * mix_gain[:, None, :]).astype(BF16),
            (jnp.transpose(w_gate, (1, 0, 2)) * mix_gain[:, None, :]).astype(BF16))


def _prep_ffn(norm_gain, w_gate, w_up, w_down):
    g = norm_gain[:, :, None]
    return (g * w_gate).astype(BF16), (g * w_up).astype(BF16), w_down.astype(BF16)


def _prep_qk_gains(aq, ak, fq, fk):
    scale = HEAD_DIM ** -0.5 * LOG2E
    ones = jnp.ones((aq.shape[0], SWA_KV_HEADS * HEAD_DIM), F32)
    row = jnp.concatenate([jnp.tile(aq * scale, (1, SWA_Q_HEADS)), jnp.tile(ak, (1, SWA_KV_HEADS)),
                           ones, jnp.tile(fq * scale, (1, FOX_HEADS)),
                           jnp.tile(fk, (1, FOX_HEADS))],
                          axis=-1)
    return row[:, None, :]


def kernel(x, rel_bias, ffn1_norm, ffn1_w_gate, ffn1_w_up, ffn1_w_down, mix_norm, w_in, b_forget, conf_dw, conf_dw_b, conf_ln_g, conf_ln_b, conf_w_out, sc_conv, sc_w_out, swa_q_norm, swa_k_norm, swa_sink, swa_w_o, fox_q_norm, fox_k_norm, fox_w_o, w_out, ffn2_norm, ffn2_w_gate, ffn2_w_up, ffn2_w_down):
    b, t, d = x.shape
    depth = w_in.shape[0]

    w_proj, w_gate = _prep_proj(w_in, mix_norm)
    qk_gains = _prep_qk_gains(swa_q_norm, swa_k_norm, fox_q_norm, fox_k_norm)
    bf = _fgate_cols(b_forget)[:, None, :]
    swa_rows = jnp.concatenate([swa_w_o[:, h * HEAD_DIM:(h + 1) * HEAD_DIM] for h in SWA_Q_ORDER],
                               axis=1)
    w_branch = jnp.stack([conf_w_out, sc_w_out, swa_rows, fox_w_o], axis=1).astype(BF16)
    w_out_b = w_out.astype(BF16)
    ffn1 = _prep_ffn(ffn1_norm, ffn1_w_gate, ffn1_w_up, ffn1_w_down)
    ffn2 = _prep_ffn(ffn2_norm, ffn2_w_gate, ffn2_w_up, ffn2_w_down)

    for l in range(depth):
        x = _ffn(x.reshape(b * t, d), l, *ffn1).reshape(b, t, d)
        shift, shifted = _fox_shift(fox_q_norm[l], fox_k_norm[l])
        zc, zs, qa, ka, va, qf, kf, vft = _inproj(x, l, w_proj, qk_gains, bf, shift)
        uc, us = _conv(zc, zs, l, conf_dw, conf_dw_b, conf_ln_g, conf_ln_b, sc_conv)
        a_shift, a_shifted = _swa_shift(swa_q_norm[l], swa_k_norm[l], rel_bias, swa_sink[l])
        oa = lax.cond(a_shifted, functools.partial(_swa, online=False),
                      functools.partial(_swa, online=True),
                      rel_bias * LOG2E, swa_sink[l] * LOG2E, a_shift * LOG2E, qa, ka, va)
        of = lax.cond(shifted, functools.partial(_fox, online=False),
                      functools.partial(_fox, online=True), qf, kf, vft)
        x = _merge(x, l, uc, us, oa, of, w_branch, w_gate, w_out_b)
        x = _ffn(x.reshape(b * t, d), l, *ffn2).reshape(b, t, d)
    return x
```

```python
import functools
import math

import numpy as np
import jax
import jax.numpy as jnp
from jax import lax
from jax.experimental import pallas as pl
from jax.experimental.pallas import tpu as pltpu

F32 = jnp.float32
BF16 = jnp.bfloat16

D_MODEL = 1024
HEAD_DIM = 64
CONV_CH = 256
CONV_K = 31
SC_CH = 256
SC_K = 3
SWA_Q_HEADS = 4
SWA_KV_HEADS = 2
WINDOW = 128
FOX_HEADS = 4
N_BUCKETS = 32
MAX_DISTANCE = 128
D_FF = 2816
N_BRANCH = 4
EPS = 1e-6
NEG_INF = -1e30
LOG2E = math.log2(math.e)

LANES = 128
SUBLANES = 8
VMEM_LIMIT = 56 * 1024 * 1024

COL_CONV = 0
COL_SC = COL_CONV + 2 * CONV_CH
COL_AQ = COL_SC + 3 * SC_CH
COL_AK = COL_AQ + SWA_Q_HEADS * HEAD_DIM
COL_AV = COL_AK + SWA_KV_HEADS * HEAD_DIM
COL_FQ = COL_AV + SWA_KV_HEADS * HEAD_DIM
COL_FK = COL_FQ + FOX_HEADS * HEAD_DIM
COL_FV = COL_FK + FOX_HEADS * HEAD_DIM
COL_FF = COL_FV + FOX_HEADS * HEAD_DIM
FOX_PAIRS = FOX_HEADS // 2
FF_REP = 9
PROJ_COLS = COL_FF + LANES
SWA_Q_ORDER = (0, 2, 1, 3)

ROW_TILE = 1024
CONV_HALO = 32
SC_HALO = 8
CONV_CHUNK = 64
INPROJ_TILE = 1024
MERGE_TILE = 1024
CONV_TILE = 1024
SWA_TILE = 2048
FOX_TQ = 2048
FOX_TK = 512
FOX_QC = 256
FOX_VROWS = 80
FOX_UNROLL = 4
SWA_MAX_SPREAD = 100.0
FOX_MAX_SHIFT = 56.0


def _params(n_axes, flags=None):
    return pltpu.CompilerParams(dimension_semantics=("arbitrary",) * n_axes,
                                vmem_limit_bytes=VMEM_LIMIT, flags=flags)


def _resident(shape, layer=None):
    if layer is None:
        return pl.BlockSpec(shape, lambda *_: (0,) * len(shape), pipeline_mode=pl.Buffered(1))
    return pl.BlockSpec((None,) + tuple(shape), lambda *_: (layer,) + (0,) * len(shape),
                        pipeline_mode=pl.Buffered(1))


def _sigmoid(x):
    return 1.0 / (1.0 + jnp.exp2(x * -LOG2E))


def _ffn_kernel(x_ref, wg_ref, wu_ref, wd_ref, o_ref):
    x = x_ref[...]
    inv = lax.rsqrt(jnp.mean(x * x, axis=-1, keepdims=True) + EPS)
    xb = x.astype(BF16)
    half_gate = jnp.dot(xb, wg_ref[...], preferred_element_type=F32) * (0.5 * inv)
    up = jnp.dot(xb, wu_ref[...], preferred_element_type=F32)
    act = ((half_gate + half_gate * jnp.tanh(half_gate)) * up).astype(BF16)
    y = jnp.dot(act, wd_ref[...], preferred_element_type=F32)
    o_ref[...] = x + (0.5 * inv) * y


def _ffn(x2d, layer, wg, wu, wd):
    n, d = x2d.shape
    row = pl.BlockSpec((ROW_TILE, d), lambda i: (i, 0))
    return pl.pallas_call(
        _ffn_kernel,
        grid=(n // ROW_TILE,),
        in_specs=[row, _resident(wg.shape[1:], layer), _resident(wu.shape[1:], layer),
                  _resident(wd.shape[1:], layer)],
        out_specs=row,
        out_shape=jax.ShapeDtypeStruct((n, d), F32),
        compiler_params=_params(1),
        name="ffn",
    )(x2d, wg, wu, wd)


def _split3(v):
    hi = v.astype(BF16).astype(F32)
    r = v - hi
    mid = r.astype(BF16).astype(F32)
    return hi, mid, r - mid


def _pair_rms(zb, first, eps):
    sq = zb * zb
    s0 = jnp.sum(jnp.where(first, sq, 0.0), axis=-1, keepdims=True)
    s1 = jnp.sum(jnp.where(first, 0.0, sq), axis=-1, keepdims=True)
    inv = jnp.where(first, lax.rsqrt(s0 * (1.0 / HEAD_DIM) + eps),
                    lax.rsqrt(s1 * (1.0 / HEAD_DIM) + eps))
    return zb * inv


def _inproj_kernel(x_ref, w_ref, qkg_ref, bf_ref, shift_ref,
                   zc_ref, zs_ref, qa_ref, ka_ref, va_ref, qf_ref, kf_ref, vft_ref, carry_ref):
    tm = x_ref.shape[1]

    @pl.when(pl.program_id(1) == 0)
    def _():
        carry_ref[...] = jnp.zeros_like(carry_ref)

    x = x_ref[0]
    ms = jnp.mean(x * x, axis=-1, keepdims=True) + EPS
    inv = lax.rsqrt(ms)
    qk_eps = EPS * ms
    h = x.astype(BF16)
    lane = lax.broadcasted_iota(jnp.int32, (tm, LANES), 1)
    first = lane < HEAD_DIM
    slot = lane % FF_REP
    used = lane < FOX_HEADS * FF_REP
    ext = {}

    def normed(z, col0, i):
        g0 = col0 - COL_AQ + i * LANES
        return _pair_rms(z[:, i * LANES:(i + 1) * LANES], first, qk_eps) * qkg_ref[:, g0:g0 + LANES]

    def forget_terms(z):
        ff = z * inv + bf_ref[...]
        logf = jnp.minimum(ff, 0.0) - jnp.log1p(jnp.exp(-jnp.abs(ff)))
        rows = lax.broadcasted_iota(jnp.int32, logf.shape, 0)
        cum = logf
        step = 1
        while step < tm:
            cum = cum + jnp.where(rows >= step, pltpu.roll(cum, step, axis=0), 0.0)
            step *= 2
        cum = cum + carry_ref[...]
        carry_ref[...] = cum[tm - 1:tm, :]
        cum = cum * LOG2E
        s_hi, s_mid, s_lo = _split3(shift_ref[...])
        hi, mid, lo = _split3(cum)
        q_extra = jnp.where(slot == 0, hi, jnp.where(slot == 1, mid, jnp.where(
            slot == 2, lo, jnp.where(slot < 6, 1.0, jnp.where(
                slot == 6, -s_hi, jnp.where(slot == 7, -s_mid, -s_lo))))))
        k_extra = jnp.where(slot == 3, -hi, jnp.where(slot == 4, -mid, jnp.where(
            slot == 5, -lo, 1.0)))
        ext["q"] = jnp.where(used, q_extra, 0.0).astype(BF16)
        ext["k"] = jnp.where(used, k_extra, 0.0).astype(BF16)

    def fox_qk(z):
        for p in range(FOX_PAIRS):
            base = 2 * p * LANES
            qf_ref[0, :, base:base + LANES] = normed(z, COL_FQ, p).astype(BF16)
            qf_ref[0, :, base + LANES:base + 2 * LANES] = ext["q"]
            kf_ref[0, :, base:base + LANES] = normed(z, COL_FQ, FOX_PAIRS + p).astype(BF16)
            kf_ref[0, :, base + LANES:base + 2 * LANES] = ext["k"]

    def swa_qk(z):
        n_q = SWA_Q_HEADS // 2
        for i in range(n_q):
            qa_ref[0, :, i * LANES:(i + 1) * LANES] = normed(z, COL_AQ, i).astype(BF16)
        ka_ref[0] = normed(z, COL_AQ, n_q).astype(BF16)

    def fox_v(z):
        vt = (z * inv).T
        ones_row = (lax.broadcasted_iota(jnp.int32, (FOX_VROWS - HEAD_DIM, tm), 0) == 0).astype(F32)
        for i in range(FOX_HEADS):
            vft_ref[0, i * FOX_VROWS:(i + 1) * FOX_VROWS, :] = jnp.concatenate(
                [vt[i * HEAD_DIM:(i + 1) * HEAD_DIM], ones_row], axis=0).astype(BF16)

    def swa_v(z):
        va_ref[0, :, 0:LANES] = (z * inv).astype(BF16)
        va_ref[0, :, LANES:2 * LANES] = (lane == 0).astype(BF16)

    def conv_in(z):
        zc_ref[0] = z * inv

    def sc_in(z):
        zs_ref[0] = z * inv

    z = lax.dot_general(h, w_ref[...], (((1,), (1,)), ((), ())),
                        preferred_element_type=F32)
    for c0, c1, epilogue in ((COL_FF, PROJ_COLS, forget_terms), (COL_FQ, COL_FV, fox_qk),
                             (COL_AQ, COL_AV, swa_qk), (COL_FV, COL_FF, fox_v),
                             (COL_AV, COL_FQ, swa_v), (COL_CONV, COL_SC, conv_in),
                             (COL_SC, COL_AQ, sc_in)):
        epilogue(z[:, c0:c1])


def _inproj(x, layer, w, qkg, bf, shift):
    b, t, d = x.shape
    tm = INPROJ_TILE

    def rows(width):
        return pl.BlockSpec((1, tm, width), lambda bi, ti: (bi, ti, 0))

    def oshape(width, dtype=BF16):
        return jax.ShapeDtypeStruct((b, t, width), dtype)

    n_v = FOX_HEADS * FOX_VROWS
    return pl.pallas_call(
        _inproj_kernel,
        grid=(b, t // tm),
        in_specs=[rows(d), _resident(w.shape[1:], layer),
                  _resident(qkg.shape[1:], layer), _resident(bf.shape[1:], layer),
                  _resident((1, LANES))],
        out_specs=[rows(2 * CONV_CH), rows(3 * SC_CH),
                   rows(SWA_Q_HEADS * HEAD_DIM), rows(SWA_KV_HEADS * HEAD_DIM),
                   rows(2 * LANES),
                   rows(2 * FOX_PAIRS * LANES), rows(2 * FOX_PAIRS * LANES),
                   pl.BlockSpec((1, n_v, tm), lambda bi, ti: (bi, 0, ti))],
        out_shape=[oshape(2 * CONV_CH, F32), oshape(3 * SC_CH, F32),
                   oshape(SWA_Q_HEADS * HEAD_DIM), oshape(SWA_KV_HEADS * HEAD_DIM),
                   oshape(2 * LANES),
                   oshape(2 * FOX_PAIRS * LANES), oshape(2 * FOX_PAIRS * LANES),
                   jax.ShapeDtypeStruct((b, n_v, t), BF16)],
        scratch_shapes=[pltpu.VMEM((1, LANES), F32)],
        compiler_params=_params(2),
        name="inproj",
    )(x, w, qkg, bf, jnp.full((1, LANES), shift, F32))


def _fill_shifted(ext_ref, base, n_taps):
    n = ext_ref.shape[1] - SUBLANES
    for r in sorted({(base + k) % SUBLANES for k in range(n_taps)} - {0}):
        ext_ref[r, 0:n, :] = ext_ref[0, r:r + n, :]


def _taps(ext_ref, w_ref, r0, base, n_taps, acc):
    groups = CONV_CHUNK // SUBLANES
    acc = acc.reshape(groups, SUBLANES, acc.shape[-1])
    for k in range(n_taps):
        r = (base + k) % SUBLANES
        start = r0 + base + k - r
        u = ext_ref[r, start:start + CONV_CHUNK, :]
        acc = acc + w_ref[k][None] * u.reshape(groups, SUBLANES, u.shape[-1])
    return acc.reshape(CONV_CHUNK, acc.shape[-1])


def _conv_kernel(zc_ref, zch_ref, zs_ref, zsh_ref, dw_ref, dwb_ref, lng_ref, lnb_ref, scw_ref,
                 uc_ref, us_ref, ext_ref, ext2_ref):
    tt = zc_ref.shape[1]
    keep = (pl.program_id(1) > 0).astype(F32)

    def glu(v):
        return v[:, :CONV_CH] * _sigmoid(v[:, CONV_CH:])

    ext_ref[0, 0:CONV_HALO, :] = glu(zch_ref[0]) * keep
    zsh = zsh_ref[0]
    ext2_ref[0, 0:SC_HALO, :] = zsh[:, SC_CH:2 * SC_CH] * zsh[:, 2 * SC_CH:] * keep
    for r0 in range(0, tt, CONV_CHUNK):
        ext_ref[0, CONV_HALO + r0:CONV_HALO + r0 + CONV_CHUNK, :] = glu(
            zc_ref[0, r0:r0 + CONV_CHUNK, :])
        ext2_ref[0, SC_HALO + r0:SC_HALO + r0 + CONV_CHUNK, :] = (
            zs_ref[0, r0:r0 + CONV_CHUNK, SC_CH:2 * SC_CH] * zs_ref[0, r0:r0 + CONV_CHUNK, 2 * SC_CH:])
    conv_base = CONV_HALO - (CONV_K - 1)
    sc_base = SC_HALO - (SC_K - 1)
    _fill_shifted(ext_ref, conv_base, CONV_K)
    _fill_shifted(ext2_ref, sc_base, SC_K)

    for r0 in range(0, tt, CONV_CHUNK):
        acc = _taps(ext_ref, dw_ref, r0, conv_base, CONV_K,
                    jnp.broadcast_to(dwb_ref[...], (CONV_CHUNK, CONV_CH)))
        mu = jnp.mean(acc, axis=-1, keepdims=True)
        cen = acc - mu
        var = jnp.mean(cen * cen, axis=-1, keepdims=True)
        y = cen * lax.rsqrt(var + EPS) * lng_ref[...] + lnb_ref[...]
        uc_ref[0, r0:r0 + CONV_CHUNK, :] = (y * _sigmoid(y)).astype(BF16)

        acc2 = _taps(ext2_ref, scw_ref, r0, sc_base, SC_K, jnp.zeros((CONV_CHUNK, SC_CH), F32))
        us_ref[0, r0:r0 + CONV_CHUNK, :] = (
            zs_ref[0, r0:r0 + CONV_CHUNK, :SC_CH] * acc2).astype(BF16)


def _conv(zc, zs, layer, dw, dwb, lng, lnb, scw):
    b, t, _ = zc.shape
    tt = CONV_TILE

    def rows(width):
        return pl.BlockSpec((1, tt, width), lambda bi, ti: (bi, ti, 0))

    def halo(rows_, width):
        per = tt // rows_
        return pl.BlockSpec((1, rows_, width),
                            lambda bi, ti: (bi, jnp.maximum(ti * per - 1, 0), 0))

    def vec(a):
        return a.reshape(a.shape[0], 1, a.shape[1])

    def taps(w):
        return jnp.broadcast_to(w[:, :, None, :], w.shape[:2] + (SUBLANES, w.shape[2]))

    return pl.pallas_call(
        _conv_kernel,
        grid=(b, t // tt),
        in_specs=[rows(2 * CONV_CH), halo(CONV_HALO, 2 * CONV_CH),
                  rows(3 * SC_CH), halo(SC_HALO, 3 * SC_CH),
                  _resident((CONV_K, SUBLANES, CONV_CH), layer), _resident((1, CONV_CH), layer),
                  _resident((1, CONV_CH), layer), _resident((1, CONV_CH), layer),
                  _resident((SC_K, SUBLANES, SC_CH), layer)],
        out_specs=[rows(CONV_CH), rows(SC_CH)],
        out_shape=[jax.ShapeDtypeStruct((b, t, CONV_CH), BF16),
                   jax.ShapeDtypeStruct((b, t, SC_CH), BF16)],
        scratch_shapes=[pltpu.VMEM((SUBLANES, CONV_HALO + tt, CONV_CH), F32),
                        pltpu.VMEM((SUBLANES, SC_HALO + tt, SC_CH), F32)],
        compiler_params=_params(2),
        name="conv",
    )(zc, zc, zs, zs, taps(dw), vec(dwb), vec(lng), vec(lnb), taps(scw))


def _t5_bucket_table():
    max_exact = N_BUCKETS // 2
    dist = np.maximum(np.arange(WINDOW)[:, None] + WINDOW - np.arange(2 * WINDOW)[None, :], 0)
    d = np.maximum(dist, 1).astype(np.float32)
    large = max_exact + (np.log(d / np.float32(max_exact)) / np.float32(
        math.log(MAX_DISTANCE / max_exact)) * np.float32(N_BUCKETS - max_exact)).astype(np.int32)
    large = np.minimum(large, N_BUCKETS - 1)
    return np.where(dist < max_exact, dist, large).astype(np.int32)


def _swa_kernel(rb_ref, sink_ref, shift_ref, bucket_ref, q_ref, k_ref, kh_ref, v_ref, vh_ref, o_ref,
                bias_ref, kk_ref, vv_ref, *, online):
    tq = q_ref.shape[1]
    first_step = (pl.program_id(0) == 0) & (pl.program_id(1) == 0)

    @pl.when(first_step)
    def _():
        bucket = bucket_ref[...]
        qi = lax.broadcasted_iota(jnp.int32, bucket.shape, 0) + WINDOW
        ki = lax.broadcasted_iota(jnp.int32, bucket.shape, 1)
        dist = qi - ki
        ok = (dist >= 0) & (dist < WINDOW)
        for h in range(SWA_Q_HEADS):
            bias = jnp.zeros(bucket.shape, F32)
            for bk in range(N_BUCKETS):
                bias = jnp.where(bucket == bk, rb_ref[bk, h], bias)
            bias_ref[h] = jnp.where(ok, bias - shift_ref[h], NEG_INF)

    kk_ref[0:WINDOW, :] = kh_ref[0]
    kk_ref[WINDOW:, :] = k_ref[0]
    vv_ref[0:WINDOW, :] = vh_ref[0]
    vv_ref[WINDOW:, :] = v_ref[0]
    seq_start = pl.program_id(1) == 0
    prev_half = lax.broadcasted_iota(jnp.int32, (WINDOW, 2 * WINDOW), 1) < WINDOW
    first = lax.broadcasted_iota(jnp.int32, (WINDOW, LANES), 1) < HEAD_DIM

    for sb in range(tq // WINDOW):
        r0 = sb * WINDOW
        kk = kk_ref[r0:r0 + 2 * WINDOW, :]
        vv = vv_ref[r0:r0 + 2 * WINDOW, :]
        for blk in range(SWA_Q_HEADS // 2):
            qb = q_ref[0, r0:r0 + WINDOW, blk * LANES:(blk + 1) * LANES]
            outs = []
            for half in range(2):
                h = SWA_Q_ORDER[2 * blk + half]
                keep = first if half == 0 else jnp.logical_not(first)
                q = jnp.where(keep, qb, jnp.zeros_like(qb))
                s = lax.dot_general(q, kk, (((1,), (1,)), ((), ())), preferred_element_type=F32)
                s = s + bias_ref[h]
                if sb == 0:
                    s = jnp.where(seq_start & prev_half, NEG_INF, s)
                sink = jnp.full((1, 1), sink_ref[h] - shift_ref[h], F32)
                if online:
                    m = jnp.maximum(jnp.max(s, axis=-1, keepdims=True), sink)
                    s, sink = s - m, sink - m
                pv = jnp.dot(jnp.exp2(s).astype(BF16), vv, preferred_element_type=F32)
                outs.append(pv[:, :LANES] / (pv[:, LANES:LANES + 1] + jnp.exp2(sink)))
            o_ref[0, r0:r0 + WINDOW, blk * LANES:(blk + 1) * LANES] = jnp.where(
                first, outs[0], outs[1]).astype(BF16)


def _swa(rel_bias2, sink2, shift, qa, ka, va, *, online):
    b, t, _ = qa.shape
    tq = SWA_TILE
    per = tq // WINDOW
    kw, vw = ka.shape[-1], va.shape[-1]
    smem = pl.BlockSpec(memory_space=pltpu.SMEM)

    def cur(width):
        return pl.BlockSpec((1, tq, width), lambda bi, ti: (bi, ti, 0))

    def halo(width):
        return pl.BlockSpec((1, WINDOW, width),
                            lambda bi, ti: (bi, jnp.maximum(ti * per - 1, 0), 0))

    return pl.pallas_call(
        functools.partial(_swa_kernel, online=online),
        grid=(b, t // tq),
        in_specs=[smem, smem, smem, _resident((WINDOW, 2 * WINDOW)),
                  cur(qa.shape[-1]), cur(kw), halo(kw), cur(vw), halo(vw)],
        out_specs=cur(qa.shape[-1]),
        out_shape=jax.ShapeDtypeStruct(qa.shape, BF16),
        scratch_shapes=[pltpu.VMEM((SWA_Q_HEADS, WINDOW, 2 * WINDOW), F32),
                        pltpu.VMEM((WINDOW + tq, kw), BF16),
                        pltpu.VMEM((WINDOW + tq, vw), BF16)],
        compiler_params=_params(2),
        name="swa_online" if online else "swa",
    )(rel_bias2, sink2, shift, jnp.asarray(_t5_bucket_table()), qa, ka, ka, va, va)


def _swa_shift(q_gain, k_gain, rel_bias, sink):
    qk = HEAD_DIM ** 0.5 * jnp.max(jnp.abs(q_gain)) * jnp.max(jnp.abs(k_gain))
    hi = jnp.maximum(qk + jnp.max(rel_bias, axis=0), sink)
    lo = jnp.maximum(rel_bias[0] - qk, sink)
    usable = jnp.all((hi - lo) * LOG2E < SWA_MAX_SPREAD)
    return jnp.where(usable, hi, 0.0), usable


def _fox_kernel(q_ref, k_ref, vt_ref, o_ref, *, online):
    tq = q_ref.shape[1]
    qi = pl.program_id(2)
    qp = q_ref[0]
    lane = lax.broadcasted_iota(jnp.int32, qp.shape, 1)
    zero = jnp.zeros_like(qp)
    x0 = LANES + pl.program_id(1) * (2 * FF_REP)
    q_heads = [
        jnp.where((lane < HEAD_DIM) | ((lane >= x0) & (lane < x0 + FF_REP)), qp, zero),
        jnp.where(((lane >= HEAD_DIM) & (lane < LANES)) |
                  ((lane >= x0 + FF_REP) & (lane < x0 + 2 * FF_REP)), qp, zero)]

    chains = [(hd, c) for c in range(tq // FOX_QC) for hd in range(2)]

    def run(blocks, carry):
        state = dict(enumerate(carry))
        items = []
        for j, diag in blocks:
            k0 = pl.multiple_of(j * FOX_TK, FOX_TK)
            ks = k_ref[0, pl.ds(k0, FOX_TK), :]
            vts = [vt_ref[0, hd * FOX_VROWS:(hd + 1) * FOX_VROWS, pl.ds(k0, FOX_TK)]
                   for hd in range(2)]
            key_lo = 0 if diag is None else diag * FOX_TK
            for ci, (hd, c) in enumerate(chains):
                if diag is None or key_lo <= (c + 1) * FOX_QC - 1:
                    masked = diag is not None and key_lo + FOX_TK - 1 > c * FOX_QC
                    items.append((ci, ks, vts[hd], key_lo if masked else None))

        def scores(item):
            ci, ks, _, mask_lo = item
            hd, c = chains[ci]
            st = lax.dot_general(ks, q_heads[hd][c * FOX_QC:(c + 1) * FOX_QC],
                                 (((1,), (1,)), ((), ())), preferred_element_type=F32)
            if mask_lo is not None:
                key = lax.broadcasted_iota(jnp.int32, st.shape, 0) + mask_lo
                qry = lax.broadcasted_iota(jnp.int32, st.shape, 1) + c * FOX_QC
                st = jnp.where(key <= qry, st, NEG_INF)
            return st

        def softmax(item, st):
            m, acc = state[item[0]]
            if not online:
                return m, acc, jnp.exp2(st).astype(BF16)
            m_new = jnp.maximum(m, jnp.max(st, axis=0, keepdims=True))
            return m_new, jnp.exp2(m - m_new) * acc, jnp.exp2(st - m_new).astype(BF16)

        def weighted(item, part):
            m_new, acc, p = part
            state[item[0]] = (m_new, acc + jnp.dot(item[2], p, preferred_element_type=F32))

        st_next = scores(items[0])
        part_prev = None
        for pos, item in enumerate(items):
            st_cur = st_next
            if pos + 1 < len(items):
                st_next = scores(items[pos + 1])
            part = softmax(item, st_cur)
            if part_prev is not None:
                weighted(items[pos - 1], part_prev)
            part_prev = part
        weighted(items[-1], part_prev)
        return tuple(state[ci] for ci in range(len(chains)))

    init = tuple((jnp.full((1, FOX_QC), NEG_INF, F32), jnp.zeros((FOX_VROWS, FOX_QC), F32))
                 for _ in chains)
    per = tq // FOX_TK
    assert per % FOX_UNROLL == 0
    res = lax.fori_loop(
        0, qi * (per // FOX_UNROLL),
        lambda it, c: run([(it * FOX_UNROLL + u, None) for u in range(FOX_UNROLL)], c), init)
    res = run([(qi * per + d, d) for d in range(per)], res)
    for c in range(tq // FOX_QC):
        heads = [acc[:HEAD_DIM] / acc[HEAD_DIM:HEAD_DIM + 1] for _, acc in res[2 * c:2 * c + 2]]
        out_t = jnp.concatenate(heads, axis=0)
        o_ref[0, c * FOX_QC:(c + 1) * FOX_QC, :] = out_t.T.astype(BF16)


def _fox(qf, kf, vft, *, online):
    b, t, _ = qf.shape
    assert FOX_TQ % FOX_TK == 0 and FOX_TQ % FOX_QC == 0
    return pl.pallas_call(
        functools.partial(_fox_kernel, online=online),
        grid=(b, FOX_PAIRS, t // FOX_TQ),
        in_specs=[pl.BlockSpec((1, FOX_TQ, 2 * LANES), lambda bi, pi, qi: (bi, qi, pi)),
                  pl.BlockSpec((1, t, 2 * LANES), lambda bi, pi, qi: (bi, 0, pi)),
                  pl.BlockSpec((1, 2 * FOX_VROWS, t), lambda bi, pi, qi: (bi, pi, 0))],
        out_specs=pl.BlockSpec((1, FOX_TQ, LANES), lambda bi, pi, qi: (bi, qi, pi)),
        out_shape=jax.ShapeDtypeStruct((b, t, FOX_HEADS * HEAD_DIM), BF16),
        compiler_params=_params(3),
        name="fox_online" if online else "fox",
    )(qf, kf, vft)


def _fox_shift(q_gain, k_gain):
    bound = (LOG2E * HEAD_DIM ** 0.5) * jnp.max(jnp.abs(q_gain)) * jnp.max(jnp.abs(k_gain))
    usable = bound < FOX_MAX_SHIFT
    return jnp.where(usable, bound, 0.0), usable


def _merge_kernel(x_ref, uc_ref, us_ref, oa_ref, of_ref, wb_ref, wgate_ref, wout_ref, o_ref):
    x = x_ref[0]
    half_inv = 0.5 * lax.rsqrt(jnp.mean(x * x, axis=-1, keepdims=True) + EPS)
    xb = x.astype(BF16)
    twice_merged = jnp.zeros(x.shape, F32)
    for i, u_ref in enumerate((uc_ref, us_ref, oa_ref, of_ref)):
        p = jnp.dot(u_ref[0], wb_ref[i], preferred_element_type=F32)
        half_gate = lax.dot_general(xb, wgate_ref[i * D_MODEL:(i + 1) * D_MODEL, :],
                                    (((1,), (1,)), ((), ())), preferred_element_type=F32) * half_inv
        twice_merged = twice_merged + (p + jnp.tanh(half_gate) * p)
    merged = (0.5 * twice_merged).astype(BF16)
    o_ref[0] = x + jnp.dot(merged, wout_ref[...], preferred_element_type=F32)


def _merge(x, layer, uc, us, oa, of, wb, wgate, wout):
    b, t, d = x.shape
    tm = MERGE_TILE

    def rows(width):
        return pl.BlockSpec((1, tm, width), lambda bi, ti: (bi, ti, 0))

    return pl.pallas_call(
        _merge_kernel,
        grid=(b, t // tm),
        in_specs=[rows(d), rows(CONV_CH), rows(SC_CH),
                  rows(SWA_Q_HEADS * HEAD_DIM), rows(FOX_HEADS * HEAD_DIM),
                  _resident(wb.shape[1:], layer), _resident(wgate.shape[1:], layer),
                  _resident(wout.shape[1:], layer)],
        out_specs=rows(d),
        out_shape=jax.ShapeDtypeStruct((b, t, d), F32),
        compiler_params=_params(2),
        name="merge",
    )(x, uc, us, oa, of, wb, wgate, wout)


def _fgate_cols(ff, axis=-1):
    rep = jnp.repeat(ff, FF_REP, axis=axis)
    pad = [(0, 0)] * ff.ndim
    pad[axis] = (0, LANES - FOX_HEADS * FF_REP)
    return jnp.pad(rep, pad)


def _prep_proj(w_in, mix_gain):
    wt = jnp.transpose(w_in, (2, 0, 1))
    aq0 = COL_AQ
    ff0 = COL_FF
    heads = [wt[aq0 + h * HEAD_DIM: aq0 + (h + 1) * HEAD_DIM] for h in SWA_Q_ORDER]
    w_proj = jnp.concatenate([wt[:aq0]] + heads + [wt[COL_AK:ff0],
                                                   _fgate_cols(wt[ff0:ff0 + FOX_HEADS], axis=0)], axis=0)
    w_gate = wt[ff0 + FOX_HEADS:]
    return ((jnp.transpose(w_proj, (1, 0, 2)) * mix_gain[:, None, :]).astype(BF16),
            (jnp.transpose(w_gate, (1, 0, 2)) * mix_gain[:, None, :]).astype(BF16))


def _prep_ffn(norm_gain, w_gate, w_up, w_down):
    g = norm_gain[:, :, None]
    return (g * w_gate).astype(BF16), (g * w_up).astype(BF16), w_down.astype(BF16)


def _prep_qk_gains(aq, ak, fq, fk):
    scale = HEAD_DIM ** -0.5 * LOG2E
    ones = jnp.ones((aq.shape[0], SWA_KV_HEADS * HEAD_DIM), F32)
    row = jnp.concatenate([jnp.tile(aq * scale, (1, SWA_Q_HEADS)), jnp.tile(ak, (1, SWA_KV_HEADS)),
                           ones, jnp.tile(fq * scale, (1, FOX_HEADS)),
                           jnp.tile(fk, (1, FOX_HEADS))],
                          axis=-1)
    return row[:, None, :]


def kernel(x, rel_bias, ffn1_norm, ffn1_w_gate, ffn1_w_up, ffn1_w_down, mix_norm, w_in, b_forget, conf_dw, conf_dw_b, conf_ln_g, conf_ln_b, conf_w_out, sc_conv, sc_w_out, swa_q_norm, swa_k_norm, swa_sink, swa_w_o, fox_q_norm, fox_k_norm, fox_w_o, w_out, ffn2_norm, ffn2_w_gate, ffn2_w_up, ffn2_w_down):
    b, t, d = x.shape
    depth = w_in.shape[0]

    w_proj, w_gate = _prep_proj(w_in, mix_norm)
    qk_gains = _prep_qk_gains(swa_q_norm, swa_k_norm, fox_q_norm, fox_k_norm)
    bf = _fgate_cols(b_forget)[:, None, :]
    swa_rows = jnp.concatenate([swa_w_o[:, h * HEAD_DIM:(h + 1) * HEAD_DIM] for h in SWA_Q_ORDER],
                               axis=1)
    w_branch = jnp.stack([conf_w_out, sc_w_out, swa_rows, fox_w_o], axis=1).astype(BF16)
    w_out_b = w_out.astype(BF16)
    ffn1 = _prep_ffn(ffn1_norm, ffn1_w_gate, ffn1_w_up, ffn1_w_down)
    ffn2 = _prep_ffn(ffn2_norm, ffn2_w_gate, ffn2_w_up, ffn2_w_down)

    for l in range(depth):
        x = _ffn(x.reshape(b * t, d), l, *ffn1).reshape(b, t, d)
        shift, shifted = _fox_shift(fox_q_norm[l], fox_k_norm[l])
        zc, zs, qa, ka, va, qf, kf, vft = _inproj(x, l, w_proj, qk_gains, bf, shift)
        uc, us = _conv(zc, zs, l, conf_dw, conf_dw_b, conf_ln_g, conf_ln_b, sc_conv)
        a_shift, a_shifted = _swa_shift(swa_q_norm[l], swa_k_norm[l], rel_bias, swa_sink[l])
        oa = lax.cond(a_shifted, functools.partial(_swa, online=False),
                      functools.partial(_swa, online=True),
                      rel_bias * LOG2E, swa_sink[l] * LOG2E, a_shift * LOG2E, qa, ka, va)
        of = lax.cond(shifted, functools.partial(_fox, online=False),
                      functools.partial(_fox, online=True), qf, kf, vft)
        x = _merge(x, l, uc, us, oa, of, w_branch, w_gate, w_out_b)
        x = _ffn(x.reshape(b * t, d), l, *ffn2).reshape(b, t, d)
    return x
```
